```python
import jax, jax.numpy as jnp
from jax import lax
import numpy as np

D_MODEL = 4096
BATCH = 8
SEQ = 2048
DEPTH = 1
DEC_BATCH = 8
DEC_SEQ = 64
PAST_LEN = 4096

CHUNK = 64
ATTN_Q_BLOCK = 128
EPS = 1e-6
SSD_WIDTH = D_MODEL // 2
SSD_HEAD_DIM = 64
SSD_HEADS = SSD_WIDTH // SSD_HEAD_DIM
SSD_GROUPS = 4
SSD_STATE = 128
CONV_W = 4
XBC_WIDTH = SSD_WIDTH + 2 * SSD_GROUPS * SSD_STATE
MLA_WIDTH = D_MODEL - SSD_WIDTH
MLA_V_DIM = 128
MLA_HEADS = MLA_WIDTH // MLA_V_DIM
MLA_NOPE = 128
MLA_ROPE = 64
MLA_QK_DIM = MLA_NOPE + MLA_ROPE
Q_LORA = D_MODEL // 4
KV_LORA = 512
ROPE_BASE = 10000.0
IN_SIZES = (SSD_WIDTH, XBC_WIDTH, SSD_HEADS, Q_LORA, KV_LORA, MLA_ROPE)
IN_WIDTH = sum(IN_SIZES)
IN_SPLITS = tuple(int(v) for v in np.cumsum(IN_SIZES)[:-1])
PEER_HEADS = 8
PEER_N_KEYS = 128
PEER_EXPERTS = PEER_N_KEYS * PEER_N_KEYS
PEER_TOPK = 16
PEER_QDIM = 256
PEER_HALF = PEER_QDIM // 2
PEER_BLOCK = 128

kernel_name = "hymba_ssd_mla_peer_stream_step"


def rms_norm(x, g):
    xf = x.astype(jnp.float32)
    y = xf * lax.rsqrt(jnp.mean(xf * xf, axis=-1, keepdims=True) + EPS)
    return (y * g.astype(jnp.float32)).astype(x.dtype)


def apply_rope(x, pos):
    half = MLA_ROPE // 2
    inv = 1.0 / (ROPE_BASE ** (jnp.arange(0, MLA_ROPE, 2, dtype=jnp.float32) / MLA_ROPE))
    ang = pos.astype(jnp.float32)[:, None] * inv[None, :]
    shape = (1, pos.shape[0]) + (1,) * (x.ndim - 3) + (half,)
    cos = jnp.cos(ang).reshape(shape)
    sin = jnp.sin(ang).reshape(shape)
    x1 = x[..., :half].astype(jnp.float32)
    x2 = x[..., half:].astype(jnp.float32)
    return jnp.concatenate([x1 * cos - x2 * sin, x2 * cos + x1 * sin], axis=-1).astype(x.dtype)


def chunk_causal_attention(q, k, v, q_pos, k_pos):
    b, lq, nh, dk = q.shape
    dv = v.shape[-1]
    scale = dk ** -0.5
    k_chunk = k_pos // CHUNK

    def block(args):
        qb, qpb = args
        s = jnp.einsum('bqhd,bkhd->bhqk', qb, k).astype(jnp.float32) * scale
        mask = (qpb // CHUNK)[:, None] >= k_chunk[None, :]
        p = jax.nn.softmax(jnp.where(mask, s, -jnp.inf), axis=-1).astype(v.dtype)
        return jnp.einsum('bhqk,bkhd->bqhd', p, v)

    if lq > ATTN_Q_BLOCK and lq % ATTN_Q_BLOCK == 0:
        nb = lq // ATTN_Q_BLOCK
        qs = q.reshape(b, nb, ATTN_Q_BLOCK, nh, dk).transpose(1, 0, 2, 3, 4)
        ps = q_pos.reshape(nb, ATTN_Q_BLOCK)
        out = lax.map(block, (qs, ps))
        return out.transpose(1, 0, 2, 3, 4).reshape(b, lq, nh, dv)
    return block((q, q_pos))


def ssd_scan(x, dt, A, bm, cm, h0):
    b, L, nh, p = x.shape
    g = SSD_GROUPS
    r = nh // g
    n = bm.shape[-1]
    q = min(CHUNK, L)
    c = L // q
    f32 = jnp.float32
    xf = x.astype(f32).reshape(b, c, q, g, r, p)
    dtc = dt.astype(f32).reshape(b, c, q, g, r)
    bc = bm.astype(f32).reshape(b, c, q, g, n)
    cc = cm.astype(f32).reshape(b, c, q, g, n)
    dt_t = dtc.transpose(0, 1, 3, 4, 2)
    acum = jnp.cumsum(dt_t * A.reshape(g, r)[None, None, :, :, None], axis=-1)
    causal = jnp.tril(jnp.ones((q, q), dtype=bool))
    lmat = jnp.exp(jnp.where(causal, acum[..., :, None] - acum[..., None, :], -jnp.inf))
    cb = jnp.einsum('bcign,bcjgn->bcgij', cc, bc)
    w = cb[:, :, :, None] * lmat * dt_t[..., None, :]
    y_diag = jnp.einsum('bcgrij,bcjgrp->bcigrp', w, xf)
    decay_states = jnp.exp(acum[..., -1:] - acum) * dt_t
    states = jnp.einsum('bcjgn,bcgrj,bcjgrp->bcgrpn', bc, decay_states, xf)
    chunk_decay = jnp.exp(acum[..., -1])

    def step(h, inp):
        s, d = inp
        return d[..., None, None] * h + s, h

    h_init = h0.astype(f32).reshape(b, g, r, p, n)
    h_last, h_prev = lax.scan(step, h_init, (jnp.moveaxis(states, 1, 0), jnp.moveaxis(chunk_decay, 1, 0)))
    h_prev = jnp.moveaxis(h_prev, 0, 1)
    y_off = jnp.einsum('bcign,bcgrpn->bcigrp', cc, h_prev) * jnp.exp(acum).transpose(0, 1, 4, 2, 3)[..., None]
    y = (y_diag + y_off).reshape(b, L, nh, p)
    return y, h_last.reshape(b, nh, p, n)


def peer_ffn(h, w_q, keys, u, v):
    b, L, d = h.shape
    t = h.reshape(b * L, d)
    ntok = b * L
    qry = (t @ w_q).reshape(ntok, PEER_HEADS, PEER_QDIM)
    s1 = jnp.einsum('thd,hkd->thk', qry[..., :PEER_HALF], keys[:, 0]).astype(jnp.float32)
    s2 = jnp.einsum('thd,hkd->thk', qry[..., PEER_HALF:], keys[:, 1]).astype(jnp.float32)
    v1, i1 = lax.top_k(s1, PEER_TOPK)
    v2, i2 = lax.top_k(s2, PEER_TOPK)
    cand = (v1[..., :, None] + v2[..., None, :]).reshape(ntok, PEER_HEADS, PEER_TOPK * PEER_TOPK)
    cidx = (i1[..., :, None] * PEER_N_KEYS + i2[..., None, :]).reshape(ntok, PEER_HEADS, PEER_TOPK * PEER_TOPK)
    sc, sel = lax.top_k(cand, PEER_TOPK)
    idx = jnp.take_along_axis(cidx, sel, axis=-1)
    gate = jax.nn.softmax(sc, axis=-1)
    pad = (-ntok) % PEER_BLOCK
    tp = jnp.pad(t, ((0, pad), (0, 0)))
    ip = jnp.pad(idx, ((0, pad), (0, 0), (0, 0)))
    gp = jnp.pad(gate, ((0, pad), (0, 0), (0, 0)))
    nb = (ntok + pad) // PEER_BLOCK

    def block(args):
        tb, ib, gb = args
        a = jnp.einsum('td,thkd->thk', tb, u[ib]).astype(jnp.float32)
        coef = (jax.nn.gelu(a, approximate=False) * gb).astype(tb.dtype)
        return jnp.einsum('thk,thkd->td', coef, v[ib])

    out = lax.map(block, (tp.reshape(nb, PEER_BLOCK, d),
                          ip.reshape(nb, PEER_BLOCK, PEER_HEADS, PEER_TOPK),
                          gp.reshape(nb, PEER_BLOCK, PEER_HEADS, PEER_TOPK)))
    return out.reshape(nb * PEER_BLOCK, d)[:ntok].reshape(b, L, d).astype(h.dtype)


def hybrid_layer(x, pos, past_ckv, past_krope, conv_hist, ssm_h0,
                 norm_mix, w_in, conv_w, conv_b, dt_bias, a_log, d_skip, ssd_norm,
                 q_a_norm, w_q_up, kv_a_norm, w_kv_up, q_norm, k_norm, attn_out_norm,
                 w_out, norm_ffn, peer_w_q, peer_keys, peer_u, peer_v):
    b, L, _ = x.shape
    h = rms_norm(x, norm_mix)
    proj = h @ w_in
    z, xbc, dt_raw, cq, ckv, kr = jnp.split(proj, IN_SPLITS, axis=-1)

    xp = jnp.concatenate([conv_hist.astype(xbc.dtype), xbc], axis=1)
    conv = conv_b + sum(xp[:, k:k + L] * conv_w[k] for k in range(CONV_W))
    conv_new = xp[:, xp.shape[1] - (CONV_W - 1):]
    xbc_act = jax.nn.silu(conv)
    xs, bm, cm = jnp.split(xbc_act, (SSD_WIDTH, SSD_WIDTH + SSD_GROUPS * SSD_STATE), axis=-1)
    xs = xs.reshape(b, L, SSD_HEADS, SSD_HEAD_DIM)
    bm = bm.reshape(b, L, SSD_GROUPS, SSD_STATE)
    cm = cm.reshape(b, L, SSD_GROUPS, SSD_STATE)
    dt = jax.nn.softplus(dt_raw.astype(jnp.float32) + dt_bias.astype(jnp.float32))
    A = -jnp.exp(a_log.astype(jnp.float32))
    y, h_last = ssd_scan(xs, dt, A, bm, cm, ssm_h0)
    y = (y + d_skip.astype(jnp.float32)[:, None] * xs.astype(jnp.float32)).astype(x.dtype)
    y_ssd = rms_norm(y.reshape(b, L, SSD_WIDTH) * jax.nn.silu(z), ssd_norm)

    q = (rms_norm(cq, q_a_norm) @ w_q_up).reshape(b, L, MLA_HEADS, MLA_QK_DIM)
    q = jnp.concatenate([q[..., :MLA_NOPE], apply_rope(q[..., MLA_NOPE:], pos)], axis=-1)
    c_new = rms_norm(ckv, kv_a_norm)
    kr_new = apply_rope(kr, pos)
    if past_ckv is None:
        c_all, kr_all, k_pos = c_new, kr_new, pos
    else:
        c_all = jnp.concatenate([past_ckv.astype(c_new.dtype), c_new], axis=1)
        kr_all = jnp.concatenate([past_krope.astype(kr_new.dtype), kr_new], axis=1)
        k_pos = jnp.arange(c_all.shape[1], dtype=jnp.int32)
    lk = c_all.shape[1]
    kv = (c_all @ w_kv_up).reshape(b, lk, MLA_HEADS, MLA_NOPE + MLA_V_DIM)
    k = jnp.concatenate([kv[..., :MLA_NOPE],
                         jnp.broadcast_to(kr_all[:, :, None, :], (b, lk, MLA_HEADS, MLA_ROPE))], axis=-1)
    v = kv[..., MLA_NOPE:]
    q = rms_norm(q, q_norm)
    k = rms_norm(k, k_norm)
    o = chunk_causal_attention(q, k, v, pos, k_pos).reshape(b, L, MLA_WIDTH)
    o = rms_norm(o, attn_out_norm)

    x = x + jnp.concatenate([y_ssd, o], axis=-1) @ w_out
    x = x + peer_ffn(rms_norm(x, norm_ffn), peer_w_q, peer_keys, peer_u, peer_v)
    return x, c_new, kr_new, h_last.astype(ssm_h0.dtype), conv_new


def setup_inputs(seed: int = 0) -> dict:
    key = jax.random.key(seed)
    ks = iter(jax.random.split(key, 32))
    f32 = jnp.float32

    def nrm(shape, scale):
        return jax.random.normal(next(ks), shape, f32) * scale

    def gain(shape):
        return 1.0 + 0.02 * jax.random.normal(next(ks), shape, f32)

    x_prompt = nrm((BATCH, SEQ, D_MODEL), 1.0)
    x_sample = nrm((DEC_BATCH, DEC_SEQ, D_MODEL), 1.0)
    cache_mla_ckv = nrm((DEPTH, DEC_BATCH, PAST_LEN, KV_LORA), 1.0)
    cache_mla_krope = nrm((DEPTH, DEC_BATCH, PAST_LEN, MLA_ROPE), 1.0)
    state_ssm = nrm((DEPTH, DEC_BATCH, SSD_HEADS, SSD_HEAD_DIM, SSD_STATE), 0.5)
    state_conv = nrm((DEPTH, DEC_BATCH, CONV_W - 1, XBC_WIDTH), 1.0)
    dt0 = jnp.exp(jax.random.uniform(next(ks), (DEPTH, SSD_HEADS), f32, np.log(1e-3), np.log(1e-1)))
    dt_bias = dt0 + jnp.log(-jnp.expm1(-dt0))
    a_log = jnp.log(jax.random.uniform(next(ks), (DEPTH, SSD_HEADS), f32, 1.0, 16.0))
    return {
        "x_prompt": x_prompt,
        "x_sample": x_sample,
        "cache_mla_ckv": cache_mla_ckv,
        "cache_mla_krope": cache_mla_krope,
        "state_ssm": state_ssm,
        "state_conv": state_conv,
        "norm_mix": gain((DEPTH, D_MODEL)),
        "w_in": nrm((DEPTH, D_MODEL, IN_WIDTH), D_MODEL ** -0.5),
        "conv_w": nrm((DEPTH, CONV_W, XBC_WIDTH), CONV_W ** -0.5),
        "conv_b": nrm((DEPTH, XBC_WIDTH), 0.02),
        "dt_bias": dt_bias,
        "a_log": a_log,
        "d_skip": gain((DEPTH, SSD_HEADS)),
        "ssd_norm": gain((DEPTH, SSD_WIDTH)),
        "q_a_norm": gain((DEPTH, Q_LORA)),
        "w_q_up": nrm((DEPTH, Q_LORA, MLA_HEADS * MLA_QK_DIM), Q_LORA ** -0.5),
        "kv_a_norm": gain((DEPTH, KV_LORA)),
        "w_kv_up": nrm((DEPTH, KV_LORA, MLA_HEADS * (MLA_NOPE + MLA_V_DIM)), KV_LORA ** -0.5),
        "q_norm": gain((DEPTH, MLA_QK_DIM)),
        "k_norm": gain((DEPTH, MLA_QK_DIM)),
        "attn_out_norm": gain((DEPTH, MLA_WIDTH)),
        "w_out": nrm((DEPTH, D_MODEL, D_MODEL), D_MODEL ** -0.5),
        "norm_ffn": gain((DEPTH, D_MODEL)),
        "peer_w_q": nrm((DEPTH, D_MODEL, PEER_HEADS * PEER_QDIM), D_MODEL ** -0.5),
        "peer_keys": nrm((DEPTH, PEER_HEADS, 2, PEER_N_KEYS, PEER_HALF), PEER_HALF ** -0.5),
        "peer_u": nrm((DEPTH, PEER_EXPERTS, D_MODEL), D_MODEL ** -0.5),
        "peer_v": nrm((DEPTH, PEER_EXPERTS, D_MODEL), PEER_HEADS ** -0.5),
    }


def reference(x_prompt, x_sample, cache_mla_ckv, cache_mla_krope, state_ssm, state_conv,
              norm_mix, w_in, conv_w, conv_b, dt_bias, a_log, d_skip, ssd_norm,
              q_a_norm, w_q_up, kv_a_norm, w_kv_up, q_norm, k_norm, attn_out_norm,
              w_out, norm_ffn, peer_w_q, peer_keys, peer_u, peer_v):
    bp, lp, _ = x_prompt.shape
    ls = x_sample.shape[1]
    past = cache_mla_ckv.shape[2]
    pos_p = jnp.arange(lp, dtype=jnp.int32)
    pos_s = past + jnp.arange(ls, dtype=jnp.int32)
    yp, ys = x_prompt, x_sample
    ckv_p, kr_p, ssm_p, conv_p = [], [], [], []
    ckv_s, kr_s, ssm_s, conv_s = [], [], [], []
    for l in range(DEPTH):
        lw = (norm_mix[l], w_in[l], conv_w[l], conv_b[l], dt_bias[l], a_log[l], d_skip[l], ssd_norm[l],
              q_a_norm[l], w_q_up[l], kv_a_norm[l], w_kv_up[l], q_norm[l], k_norm[l], attn_out_norm[l],
              w_out[l], norm_ffn[l], peer_w_q[l], peer_keys[l], peer_u[l], peer_v[l])
        zero_conv = jnp.zeros((bp, CONV_W - 1, XBC_WIDTH), x_prompt.dtype)
        zero_ssm = jnp.zeros((bp, SSD_HEADS, SSD_HEAD_DIM, SSD_STATE), state_ssm.dtype)
        yp, a, bb, c, d = hybrid_layer(yp, pos_p, None, None, zero_conv, zero_ssm, *lw)
        ckv_p.append(a); kr_p.append(bb); ssm_p.append(c); conv_p.append(d)
        ys, a, bb, c, d = hybrid_layer(ys, pos_s, cache_mla_ckv[l], cache_mla_krope[l],
                                       state_conv[l], state_ssm[l], *lw)
        ckv_s.append(a); kr_s.append(bb); ssm_s.append(c); conv_s.append(d)
    return (yp, ys,
            jnp.stack(ckv_p), jnp.stack(kr_p), jnp.stack(ssm_p), jnp.stack(conv_p),
            jnp.stack(ckv_s), jnp.stack(kr_s), jnp.stack(ssm_s), jnp.stack(conv_s))
```

```python
import functools
import math

import numpy as np
import jax
import jax.numpy as jnp
from jax import lax
from jax.experimental import pallas as pl
from jax.experimental.pallas import tpu as pltpu

F32 = jnp.float32
BF16 = jnp.bfloat16

EPS = 1e-6
CHUNK = 64
SSD_HEAD_DIM = 64
MLA_NOPE = 128
MLA_V_DIM = 128
ROPE_BASE = 10000.0
PEER_TOPK = 16
LANES = 128
VMEM_LIMIT = 56 * 1024 * 1024

NT_DIMS = (((1,), (1,)), ((), ()))
TN_DIMS = (((0,), (0,)), ((), ()))


def _pick(n, pref, mult):
    if n <= pref:
        return n
    t = (pref // mult) * mult
    while t >= mult:
        if n % t == 0:
            return t
        t -= mult
    return n


def _round_up(n, m):
    return (n + m - 1) // m * m


def _params(sem, vmem=VMEM_LIMIT):
    return pltpu.CompilerParams(dimension_semantics=sem, vmem_limit_bytes=vmem)


def _split3(x):
    hi = x.astype(BF16)
    r1 = x - hi.astype(F32)
    mid = r1.astype(BF16)
    lo = (r1 - mid.astype(F32)).astype(BF16)
    return hi, mid, lo


def _dot3(a01, x, dims=None):
    out = None
    for piece in _split3(x):
        if dims is None:
            d = jnp.dot(a01, piece, preferred_element_type=F32)
        else:
            d = lax.dot_general(a01, piece, dims, preferred_element_type=F32)
        out = d if out is None else out + d
    return out


def _dot3_right(x, b01, dims=None):
    out = None
    for piece in _split3(x):
        if dims is None:
            d = jnp.dot(piece, b01, preferred_element_type=F32)
        else:
            d = lax.dot_general(piece, b01, dims, preferred_element_type=F32)
        out = d if out is None else out + d
    return out


def _rms_cast_kernel(x_ref, g_ref, o_ref):
    x = x_ref[...].astype(F32)
    ms = jnp.mean(x * x, axis=-1, keepdims=True)
    o_ref[...] = (x * lax.rsqrt(ms + EPS) * g_ref[...]).astype(o_ref.dtype)


def _rms_cast(x2d, g, out_dtype=BF16, tm=256):
    t, d = x2d.shape
    tm = _pick(t, tm, 16)
    return pl.pallas_call(
        _rms_cast_kernel,
        grid=(t // tm,),
        in_specs=[pl.BlockSpec((tm, d), lambda i: (i, 0)), pl.BlockSpec((1, d), lambda i: (0, 0))],
        out_specs=pl.BlockSpec((tm, d), lambda i: (i, 0)),
        out_shape=jax.ShapeDtypeStruct((t, d), out_dtype),
        compiler_params=_params(("parallel",)),
        name="rms_cast",
    )(x2d, g.reshape(1, d).astype(F32))


def _mm_kernel(*refs, n_pairs, has_res):
    o_ref = refs[-1]
    acc = None
    for p in range(n_pairs):
        d = jnp.dot(refs[2 * p][...], refs[2 * p + 1][...], preferred_element_type=F32)
        acc = d if acc is None else acc + d
    if has_res:
        acc = refs[2 * n_pairs][...] + acc
    o_ref[...] = acc.astype(o_ref.dtype)


def _matmul(pairs, res=None, out_dtype=F32, tm=512, tn=512, name="matmul"):
    m = pairs[0][0].shape[0]
    n = pairs[0][1].shape[1]
    tm = _pick(m, tm, 16)
    tn = _pick(n, tn, LANES)
    in_specs, args = [], []
    for a, w in pairs:
        k = a.shape[1]
        in_specs += [pl.BlockSpec((tm, k), lambda i, j: (i, 0)), pl.BlockSpec((k, tn), lambda i, j: (0, j))]
        args += [a, w]
    if res is not None:
        in_specs.append(pl.BlockSpec((tm, tn), lambda i, j: (i, j)))
        args.append(res)
    return pl.pallas_call(
        functools.partial(_mm_kernel, n_pairs=len(pairs), has_res=res is not None),
        grid=(m // tm, n // tn),
        in_specs=in_specs,
        out_specs=pl.BlockSpec((tm, tn), lambda i, j: (i, j)),
        out_shape=jax.ShapeDtypeStruct((m, n), out_dtype),
        compiler_params=_params(("parallel", "arbitrary")),
        name=name,
    )(*args)


def _silu(x):
    return x * jax.nn.sigmoid(x)


def _softplus(x):
    return jnp.maximum(x, 0.0) + jnp.log1p(jnp.exp(-jnp.abs(x)))


def _ssd_kernel(z_ref, xs_ref, bm_ref, cm_ref, dt_ref,
                hx_ref, hb_ref, hc_ref, h0_ref,
                cwx_ref, cwb_ref, cwc_ref, cbx_ref, cbb_ref, cbc_ref,
                dtb_ref, alog_ref, dskip_ref, gn_ref,
                tri_ref, exp_ref, expl_ref, expt_ref,
                y_ref, hlast_ref, nx_ref, nb_ref, nc_ref,
                bx_ref, bb_ref, bc_ref, state_ref, ydiag_ref,
                *, Q, H, P, G, N):
    c = pl.program_id(1)
    nc = pl.num_programs(1)
    R = H // G
    W = H * P

    @pl.when(c == 0)
    def _init():
        for buf, hist in ((bx_ref, hx_ref), (bb_ref, hb_ref), (bc_ref, hc_ref)):
            buf[0:8, :] = jnp.zeros((8, buf.shape[1]), F32)
            buf[5:8, :] = hist[0]
        state_ref[...] = h0_ref[0]

    def conv(buf, x_ref, cw_ref, cb_ref):
        x = x_ref[0]
        buf[8:8 + Q, :] = x
        acc = buf[5:5 + Q, :] * cw_ref[0:1, :]
        acc = acc + buf[6:6 + Q, :] * cw_ref[1:2, :]
        acc = acc + buf[7:7 + Q, :] * cw_ref[2:3, :]
        acc = acc + x * cw_ref[3:4, :]
        buf[0:8, :] = buf[Q:Q + 8, :]
        return _silu(cb_ref[...] + acc)

    xs = conv(bx_ref, xs_ref, cwx_ref, cbx_ref)
    bm = conv(bb_ref, bm_ref, cwb_ref, cbb_ref)
    cm = conv(bc_ref, cm_ref, cwc_ref, cbc_ref)

    @pl.when(c == nc - 1)
    def _tail():
        nx_ref[0] = bx_ref[5:8, :]
        nb_ref[0] = bb_ref[5:8, :]
        nc_ref[0] = bc_ref[5:8, :]

    dt = _softplus(dt_ref[0] + dtb_ref[...])
    a_neg = -jnp.exp(alog_ref[...])
    dta = dt * a_neg
    tri = tri_ref[...]
    acum = _dot3(tri, dta)
    eye = (lax.broadcasted_iota(jnp.int32, (Q, Q), 0) == lax.broadcasted_iota(jnp.int32, (Q, Q), 1)).astype(BF16)
    acum_t = _dot3_right(acum, eye, TN_DIMS)
    dt_t = _dot3_right(dt, eye, TN_DIMS)

    acum_x = _dot3_right(acum, exp_ref[...])
    dt_x = _dot3_right(dt, exp_ref[...])
    acum_l = _dot3_right(acum, expl_ref[...])

    ii = lax.broadcasted_iota(jnp.int32, (Q, Q), 0)
    jj = lax.broadcasted_iota(jnp.int32, (Q, Q), 1)
    causal = ii >= jj
    xs_b = xs.astype(BF16)
    bm_b = bm.astype(BF16)
    cm_b = cm.astype(BF16)
    st_b = state_ref[...].astype(BF16)

    exp_acum_x = jnp.exp(acum_x)
    decay_x = jnp.exp(acum_x[Q - 1:Q, :] - acum_x) * dt_x
    xd_b = (xs * decay_x).astype(BF16)

    last_t = jnp.broadcast_to(acum_t[:, Q - 1:Q], (acum_t.shape[0], N))
    cd_rows = jnp.exp(_dot3(expt_ref[...], last_t))

    for g in range(G):
        cg = cm_b[:, g * N:(g + 1) * N]
        bg = bm_b[:, g * N:(g + 1) * N]
        cb = lax.dot_general(cg, bg, NT_DIMS, preferred_element_type=F32)
        for r in range(R):
            h = g * R + r
            dm = acum_l[:, h * Q:(h + 1) * Q] - acum_t[h:h + 1, :]
            lm = jnp.where(causal, jnp.exp(dm), 0.0)
            wm = (cb * lm * dt_t[h:h + 1, :]).astype(BF16)
            ydiag_ref[:, h * P:(h + 1) * P] = jnp.dot(wm, xs_b[:, h * P:(h + 1) * P],
                                                      preferred_element_type=F32)
        rows = slice(g * R * P, (g + 1) * R * P)
        y_off = lax.dot_general(cg, st_b[rows, :], NT_DIMS, preferred_element_type=F32)
        ydiag_ref[:, rows] = ydiag_ref[:, rows] + y_off * exp_acum_x[:, rows]
        s_new = lax.dot_general(xd_b[:, rows], bg, TN_DIMS, preferred_element_type=F32)
        state_ref[rows, :] = cd_rows[rows, :] * state_ref[rows, :] + s_new

    y = ydiag_ref[...] + dskip_ref[...] * xs
    yg = y * _silu(z_ref[0])
    ms = jnp.mean(yg * yg, axis=-1, keepdims=True)
    y_ref[0] = (yg * lax.rsqrt(ms + EPS) * gn_ref[...]).astype(y_ref.dtype)

    @pl.when(c == nc - 1)
    def _final():
        hlast_ref[0] = state_ref[...]


def _ssd(proj3, offs, hist, h0, prm, Q):
    b, l, _ = proj3.shape
    H, P, G, N = prm["H"], prm["P"], prm["G"], prm["N"]
    W, GN, HL = H * P, G * N, prm["HL"]
    nchunk = l // Q
    hx, hb, hc = hist
    full = lambda shape: pl.BlockSpec(shape, lambda i, j: (0,) * len(shape))
    per_b = lambda shape: pl.BlockSpec(shape, lambda i, j: (i,) + (0,) * (len(shape) - 1))
    col = lambda width, off: pl.BlockSpec((1, Q, width), lambda i, j, o=off // width: (i, j, o))
    in_specs = [
        col(W, offs["z"]), col(W, offs["xs"]), col(GN, offs["bm"]), col(GN, offs["cm"]), col(HL, offs["dt"]),
        per_b((1, 3, W)), per_b((1, 3, GN)), per_b((1, 3, GN)), per_b((1, W, N)),
        full((4, W)), full((4, GN)), full((4, GN)), full((1, W)), full((1, GN)), full((1, GN)),
        full((1, HL)), full((1, HL)), full((1, W)), full((1, W)),
        full((Q, Q)), full((HL, W)), full((HL, H * Q)), full((W, HL)),
    ]
    out_specs = [
        pl.BlockSpec((1, Q, W), lambda i, j: (i, j, 0)),
        per_b((1, W, N)), per_b((1, 3, W)), per_b((1, 3, GN)), per_b((1, 3, GN)),
    ]
    out_shape = [
        jax.ShapeDtypeStruct((b, l, W), BF16),
        jax.ShapeDtypeStruct((b, W, N), F32),
        jax.ShapeDtypeStruct((b, 3, W), F32),
        jax.ShapeDtypeStruct((b, 3, GN), F32),
        jax.ShapeDtypeStruct((b, 3, GN), F32),
    ]
    scratch = [
        pltpu.VMEM((Q + 8, W), F32), pltpu.VMEM((Q + 8, GN), F32), pltpu.VMEM((Q + 8, GN), F32),
        pltpu.VMEM((W, N), F32), pltpu.VMEM((Q, W), F32),
    ]
    return pl.pallas_call(
        functools.partial(_ssd_kernel, Q=Q, H=H, P=P, G=G, N=N),
        grid=(b, nchunk),
        in_specs=in_specs, out_specs=out_specs, out_shape=out_shape, scratch_shapes=scratch,
        compiler_params=_params(("parallel", "arbitrary")),
        name="ssd_scan",
    )(proj3, proj3, proj3, proj3, proj3, hx, hb, hc, h0,
      prm["cwx"], prm["cwb"], prm["cwc"], prm["cbx"], prm["cbb"], prm["cbc"],
      prm["dtb"], prm["alog"], prm["dskip_x"], prm["ssd_gn"],
      prm["tri"], prm["expand"], prm["expand_l"], prm["expand_t"])


def _latent_kernel(ckv_ref, krr_ref, krot_ref, g_ref, cos_ref, sin_ref, c_ref, kr2_ref, kr_ref, *, rope):
    x = ckv_ref[0]
    ms = jnp.mean(x * x, axis=-1, keepdims=True)
    c_ref[0] = x * lax.rsqrt(ms + EPS) * g_ref[...]
    kr2 = krr_ref[0] * cos_ref[...] + krot_ref[0] * sin_ref[...]
    kr2_ref[0] = kr2
    kr_ref[0] = kr2[:, :rope]


def _latent(proj3, offs, prm, cos4, sin4, tm=256):
    b, l, _ = proj3.shape
    kvl, rope = prm["KVL"], prm["ROPE"]
    tm = _pick(l, tm, 8)
    col = lambda width, off: pl.BlockSpec((1, tm, width), lambda i, j, o=off // width: (i, j, o))
    return pl.pallas_call(
        functools.partial(_latent_kernel, rope=rope),
        grid=(b, l // tm),
        in_specs=[col(kvl, offs["ckv"]), col(LANES, offs["krr"]), col(LANES, offs["krot"]),
                  pl.BlockSpec((1, kvl), lambda i, j: (0, 0)),
                  pl.BlockSpec((tm, LANES), lambda i, j: (j, 0)), pl.BlockSpec((tm, LANES), lambda i, j: (j, 0))],
        out_specs=[pl.BlockSpec((1, tm, kvl), lambda i, j: (i, j, 0)),
                   pl.BlockSpec((1, tm, LANES), lambda i, j: (i, j, 0)),
                   pl.BlockSpec((1, tm, rope), lambda i, j: (i, j, 0))],
        out_shape=[jax.ShapeDtypeStruct((b, l, kvl), F32), jax.ShapeDtypeStruct((b, l, LANES), F32),
                   jax.ShapeDtypeStruct((b, l, rope), F32)],
        compiler_params=_params(("parallel", "parallel")),
        name="mla_latent",
    )(proj3, proj3, proj3, prm["kv_a_norm"], cos4, sin4)


def _q_kernel(cq_ref, g_ref, w_ref, cos_ref, sin_ref, gn_ref, gr_ref, q_ref, *, MH, qk_dim):
    x = cq_ref[0]
    ms = jnp.mean(x * x, axis=-1, keepdims=True)
    xn = (x * lax.rsqrt(ms + EPS) * g_ref[...]).astype(BF16)
    qf = jnp.dot(xn, w_ref[...], preferred_element_type=F32)
    cos = cos_ref[...]
    sin = sin_ref[...]
    lane = lax.broadcasted_iota(jnp.int32, (1, LANES), 1)
    half_mask = [(lane < LANES // 2).astype(F32), (lane >= LANES // 2).astype(F32)]
    rope0 = MH * MLA_NOPE
    rot0 = rope0 + MH * (LANES // 2)
    for p in range(MH // 2):
        rp = (qf[:, rope0 + p * LANES: rope0 + (p + 1) * LANES] * cos
              + qf[:, rot0 + p * LANES: rot0 + (p + 1) * LANES] * sin)
        for e in (0, 1):
            h = 2 * p + e
            nope = qf[:, h * MLA_NOPE:(h + 1) * MLA_NOPE]
            rh = rp * half_mask[e]
            ssq = jnp.sum(nope * nope, axis=-1, keepdims=True) + jnp.sum(rh * rh, axis=-1, keepdims=True)
            inv = lax.rsqrt(ssq * (1.0 / qk_dim) + EPS)
            q_ref[0, :, h * 2 * LANES: h * 2 * LANES + LANES] = (nope * inv * gn_ref[...]).astype(q_ref.dtype)
            q_ref[0, :, h * 2 * LANES + LANES:(h + 1) * 2 * LANES] = (rh * inv * gr_ref[e:e + 1, :]).astype(q_ref.dtype)


def _q_proj(proj3, offs, prm, cos4, sin4, tm=256):
    b, l, _ = proj3.shape
    ql, mh = prm["QL"], prm["MH"]
    tm = _pick(l, tm, 16)
    wq = prm["wq"]
    return pl.pallas_call(
        functools.partial(_q_kernel, MH=mh, qk_dim=prm["QK"]),
        grid=(b, l // tm),
        in_specs=[pl.BlockSpec((1, tm, ql), lambda i, j, o=offs["cq"] // ql: (i, j, o)),
                  pl.BlockSpec((1, ql), lambda i, j: (0, 0)),
                  pl.BlockSpec(wq.shape, lambda i, j: (0, 0)),
                  pl.BlockSpec((tm, LANES), lambda i, j: (j, 0)), pl.BlockSpec((tm, LANES), lambda i, j: (j, 0)),
                  pl.BlockSpec((1, LANES), lambda i, j: (0, 0)), pl.BlockSpec((2, LANES), lambda i, j: (0, 0))],
        out_specs=pl.BlockSpec((1, tm, mh * 2 * LANES), lambda i, j: (i, j, 0)),
        out_shape=jax.ShapeDtypeStruct((b, l, mh * 2 * LANES), BF16),
        compiler_params=_params(("parallel", "parallel")),
        name="mla_q",
    )(proj3, prm["q_a_norm"], wq, cos4, sin4, prm["q_gn"], prm["q_gr"])


def _kv_kernel(c_ref, kr2_ref, w_ref, gn_ref, gr_ref, k_ref, v_ref, *, MH, qk_dim):
    c = c_ref[0].astype(BF16)
    kv = jnp.dot(c, w_ref[...], preferred_element_type=F32)
    kr2 = kr2_ref[0]
    lane = lax.broadcasted_iota(jnp.int32, (1, LANES), 1)
    kr_lo = kr2 * (lane < LANES // 2).astype(F32)
    kr_ss = jnp.sum(kr_lo * kr_lo, axis=-1, keepdims=True)
    for h in range(MH):
        nope = kv[:, h * MLA_NOPE:(h + 1) * MLA_NOPE]
        ssq = jnp.sum(nope * nope, axis=-1, keepdims=True) + kr_ss
        inv = lax.rsqrt(ssq * (1.0 / qk_dim) + EPS)
        k_ref[0, :, h * 2 * LANES: h * 2 * LANES + LANES] = (nope * inv * gn_ref[...]).astype(k_ref.dtype)
        k_ref[0, :, h * 2 * LANES + LANES:(h + 1) * 2 * LANES] = (kr2 * inv * gr_ref[h % 2:h % 2 + 1, :]).astype(k_ref.dtype)
    v_ref[0] = kv[:, MH * MLA_NOPE:].astype(v_ref.dtype)


def _kv_proj(c_all, kr2_all, prm, tm=256):
    b, lk, kvl = c_all.shape
    mh = prm["MH"]
    tm = _pick(lk, tm, 16)
    wkv = prm["wkv"]
    return pl.pallas_call(
        functools.partial(_kv_kernel, MH=mh, qk_dim=prm["QK"]),
        grid=(b, lk // tm),
        in_specs=[pl.BlockSpec((1, tm, kvl), lambda i, j: (i, j, 0)),
                  pl.BlockSpec((1, tm, LANES), lambda i, j: (i, j, 0)),
                  pl.BlockSpec(wkv.shape, lambda i, j: (0, 0)),
                  pl.BlockSpec((1, LANES), lambda i, j: (0, 0)), pl.BlockSpec((2, LANES), lambda i, j: (0, 0))],
        out_specs=[pl.BlockSpec((1, tm, mh * 2 * LANES), lambda i, j: (i, j, 0)),
                   pl.BlockSpec((1, tm, mh * MLA_V_DIM), lambda i, j: (i, j, 0))],
        out_shape=[jax.ShapeDtypeStruct((b, lk, mh * 2 * LANES), BF16),
                   jax.ShapeDtypeStruct((b, lk, mh * MLA_V_DIM), BF16)],
        compiler_params=_params(("parallel", "parallel")),
        name="mla_kv",
    )(c_all, kr2_all, wkv, prm["k_gn"], prm["k_gr"])


def _attn_kernel(q_ref, k_ref, v_ref, o_ref, m_ref, l_ref, acc_ref, *, causal, tq, tk):
    qi = pl.program_id(2)
    ki = pl.program_id(3)
    nk = pl.num_programs(3)

    @pl.when(ki == 0)
    def _init():
        m_ref[...] = jnp.full(m_ref.shape, -jnp.inf, F32)
        l_ref[...] = jnp.zeros(l_ref.shape, F32)
        acc_ref[...] = jnp.zeros(acc_ref.shape, F32)

    def body():
        s = lax.dot_general(q_ref[0], k_ref[0], NT_DIMS, preferred_element_type=F32)
        if causal:
            rpos = qi * tq + lax.broadcasted_iota(jnp.int32, (tq, tk), 0)
            cpos = ki * tk + lax.broadcasted_iota(jnp.int32, (tq, tk), 1)
            s = jnp.where((rpos // CHUNK) >= (cpos // CHUNK), s, -jnp.inf)
        m_prev = m_ref[...]
        m_new = jnp.maximum(m_prev, jnp.max(s, axis=-1, keepdims=True))
        alpha = jnp.exp(m_prev - m_new)
        p = jnp.exp(s - m_new)
        l_ref[...] = alpha * l_ref[...] + jnp.sum(p, axis=-1, keepdims=True)
        acc_ref[...] = alpha * acc_ref[...] + jnp.dot(p.astype(BF16), v_ref[0], preferred_element_type=F32)
        m_ref[...] = m_new

    if causal:
        pl.when(ki * tk <= qi * tq + tq - 1)(body)
    else:
        body()

    @pl.when(ki == nk - 1)
    def _fin():
        o_ref[0] = acc_ref[...] / l_ref[...]


def _attention(q, k, v, mh, causal, tq, tk):
    b, lq, _ = q.shape
    lk = k.shape[1]
    nq, nk = lq // tq, lk // tk
    if causal:
        kmap = lambda bi, h, qi, ki: (bi, jnp.minimum(ki, (qi * tq + tq - 1) // tk), h)
    else:
        kmap = lambda bi, h, qi, ki: (bi, ki, h)
    return pl.pallas_call(
        functools.partial(_attn_kernel, causal=causal, tq=tq, tk=tk),
        grid=(b, mh, nq, nk),
        in_specs=[pl.BlockSpec((1, tq, 2 * LANES), lambda bi, h, qi, ki: (bi, qi, h)),
                  pl.BlockSpec((1, tk, 2 * LANES), kmap),
                  pl.BlockSpec((1, tk, MLA_V_DIM), kmap)],
        out_specs=pl.BlockSpec((1, tq, MLA_V_DIM), lambda bi, h, qi, ki: (bi, qi, h)),
        out_shape=jax.ShapeDtypeStruct((b, lq, mh * MLA_V_DIM), F32),
        scratch_shapes=[pltpu.VMEM((tq, 1), F32), pltpu.VMEM((tq, 1), F32), pltpu.VMEM((tq, MLA_V_DIM), F32)],
        compiler_params=_params(("parallel", "parallel", "parallel", "arbitrary")),
        name="mla_attention",
    )(q, k, v)


def _top_values(x, k):
    vals, cnts = [], []
    for _ in range(k):
        mx = jnp.max(x, axis=0, keepdims=True)
        eq = x == mx
        vals.append(mx)
        cnts.append(jnp.sum(jnp.where(eq, 1.0, 0.0), axis=0, keepdims=True))
        x = jnp.where(eq, -jnp.inf, x)
    return vals, cnts


def _peer_select_kernel(pq_ref, keys_ref, s1_ref, s2_ref, e1_ref, e2_ref, tau_ref, *, PH, HALF, TOPK):
    for h in range(PH):
        q1 = pq_ref[:, h * 2 * HALF: h * 2 * HALF + HALF].astype(BF16)
        q2 = pq_ref[:, h * 2 * HALF + HALF:(h + 1) * 2 * HALF].astype(BF16)
        s1 = lax.dot_general(keys_ref[h, 0], q1, NT_DIMS, preferred_element_type=F32)
        s2 = lax.dot_general(keys_ref[h, 1], q2, NT_DIMS, preferred_element_type=F32)
        v1, c1 = _top_values(s1, TOPK)
        v2, c2 = _top_values(s2, TOPK)
        v2m = jnp.concatenate(v2, axis=0)
        c2m = jnp.concatenate(c2, axis=0)
        cand = jnp.concatenate([v1[a] + v2m for a in range(TOPK)], axis=0)
        wgt = jnp.concatenate([c1[a] * c2m for a in range(TOPK)], axis=0)
        m = v1[0] + v2[0]
        cnt = jnp.zeros_like(m)
        tau = m
        zsum = jnp.zeros_like(m)
        for _ in range(TOPK):
            mx = jnp.max(cand, axis=0, keepdims=True)
            eq = cand == mx
            cw = jnp.sum(jnp.where(eq, wgt, 0.0), axis=0, keepdims=True)
            need = cnt < float(TOPK)
            tau = jnp.where(need, mx, tau)
            zsum = zsum + jnp.where(need, cw * jnp.exp(mx - m), 0.0)
            cnt = cnt + cw
            cand = jnp.where(eq, -jnp.inf, cand)
        s1_ref[h] = s1
        s2_ref[h] = s2
        e1_ref[h] = jnp.exp(s1 - v1[0])
        e2_ref[h] = jnp.exp(s2 - v2[0]) / zsum
        tau_ref[h] = tau


def _peer_select(pq, keys_b, tt=256):
    t = pq.shape[0]
    ph, _, nk, half = keys_b.shape
    tt = _pick(t, tt, LANES)
    big = lambda: pl.BlockSpec((ph, nk, tt), lambda i: (0, 0, i))
    shp = jax.ShapeDtypeStruct((ph, nk, t), F32)
    return pl.pallas_call(
        functools.partial(_peer_select_kernel, PH=ph, HALF=half, TOPK=PEER_TOPK),
        grid=(t // tt,),
        in_specs=[pl.BlockSpec((tt, ph * 2 * half), lambda i: (i, 0)),
                  pl.BlockSpec(keys_b.shape, lambda i: (0, 0, 0, 0))],
        out_specs=[big(), big(), big(), big(), pl.BlockSpec((ph, 1, tt), lambda i: (0, 0, i))],
        out_shape=[shp, shp, shp, shp, jax.ShapeDtypeStruct((ph, 1, t), F32)],
        compiler_params=_params(("parallel",)),
        name="peer_select",
    )(pq, keys_b)


def _gelu(x):
    return 0.5 * x * (1.0 + lax.erf(x * (1.0 / math.sqrt(2.0))))


def _peer_main_kernel(hn_ref, u_ref, vt_ref, s1_ref, s2_ref, e1_ref, e2_ref, tau_ref, out_ref, *, PH, NK, NI):
    j = pl.program_id(1)

    @pl.when(j == 0)
    def _init():
        out_ref[...] = jnp.zeros(out_ref.shape, F32)

    at = lax.dot_general(u_ref[...], hn_ref[...], NT_DIMS, preferred_element_type=F32)
    slabs = []
    for ii in range(NI):
        row = j * NI + ii
        g = None
        for h in range(PH):
            s1r = s1_ref[h, pl.ds(row, 1), :]
            e1r = e1_ref[h, pl.ds(row, 1), :]
            sel = (s1r + s2_ref[h]) >= tau_ref[h]
            term = e1r * jnp.where(sel, e2_ref[h], 0.0)
            g = term if g is None else g + term
        a = at[ii * NK:(ii + 1) * NK, :]
        slabs.append((_gelu(a) * g).astype(BF16))
    ct = jnp.concatenate(slabs, axis=0) if NI > 1 else slabs[0]
    out_ref[...] += jnp.dot(vt_ref[...], ct, preferred_element_type=F32)


def _peer_main(hn, u_b, vt_b, sel, tt=512, ne=512):
    t, d = hn.shape
    s1, s2, e1, e2, tau = sel
    ph, nk, _ = s1.shape
    nexp = u_b.shape[0]
    tt = _pick(t, tt, LANES)
    ne = _pick(nexp, ne, nk)
    ni = ne // nk
    once = dict(pipeline_mode=pl.Buffered(1))
    big = lambda: pl.BlockSpec((ph, nk, tt), lambda i, j: (0, 0, i), **once)
    return pl.pallas_call(
        functools.partial(_peer_main_kernel, PH=ph, NK=nk, NI=ni),
        grid=(t // tt, nexp // ne),
        in_specs=[pl.BlockSpec((tt, d), lambda i, j: (i, 0), **once),
                  pl.BlockSpec((ne, d), lambda i, j: (j, 0)),
                  pl.BlockSpec((d, ne), lambda i, j: (0, j)),
                  big(), big(), big(), big(),
                  pl.BlockSpec((ph, 1, tt), lambda i, j: (0, 0, i), **once)],
        out_specs=pl.BlockSpec((d, tt), lambda i, j: (0, i)),
        out_shape=jax.ShapeDtypeStruct((d, t), F32),
        compiler_params=_params(("parallel", "arbitrary")),
        name="peer_main",
    )(hn, u_b, vt_b, s1, s2, e1, e2, tau)


def _tadd_kernel(x_ref, yt_ref, o_ref):
    o_ref[...] = x_ref[...] + yt_ref[...].T


def _transpose_add(x2d, yt, tm=256):
    t, d = x2d.shape
    tm = _pick(t, tm, LANES)
    return pl.pallas_call(
        _tadd_kernel,
        grid=(t // tm,),
        in_specs=[pl.BlockSpec((tm, d), lambda i: (i, 0)), pl.BlockSpec((d, tm), lambda i: (0, i))],
        out_specs=pl.BlockSpec((tm, d), lambda i: (i, 0)),
        out_shape=jax.ShapeDtypeStruct((t, d), F32),
        compiler_params=_params(("parallel",)),
        name="peer_residual",
    )(x2d, yt)


def _prepare(norm_mix, w_in, conv_w, conv_b, dt_bias, a_log, d_skip, ssd_norm,
             q_a_norm, w_q_up, kv_a_norm, w_kv_up, q_norm, k_norm, attn_out_norm,
             w_out, norm_ffn, peer_w_q, peer_keys, peer_u, peer_v, rope, N, Q):
    d = w_in.shape[0]
    W = ssd_norm.shape[0]
    H = dt_bias.shape[0]
    P = W // H
    xbc = conv_w.shape[1]
    GN = (xbc - W) // 2
    G = GN // N
    QL = q_a_norm.shape[0]
    KVL = kv_a_norm.shape[0]
    MW = attn_out_norm.shape[0]
    MH = MW // MLA_V_DIM
    QK = MLA_NOPE + rope
    HL = _round_up(H, LANES)
    half = rope // 2
    assert 2 * rope == LANES and MH % 2 == 0 and P == SSD_HEAD_DIM

    s = np.cumsum([0, W, xbc, H, QL, KVL, rope])
    wz, wxbc, wdt, wcq, wckv, wkr = (w_in[:, s[i]:s[i + 1]] for i in range(6))
    wrot = jnp.concatenate([-wkr[:, half:], wkr[:, :half]], axis=1)
    pieces = [wz, wxbc, wcq, wckv, wkr, wkr, wrot, wrot, wdt]
    offs, o = {}, 0
    for name, width in (("z", W), ("xs", W), ("bm", GN), ("cm", GN), ("cq", QL), ("ckv", KVL),
                        ("krr", LANES), ("krot", LANES), ("dt", HL)):
        assert o % width == 0, (name, o, width)
        offs[name] = o
        o += width
    NP = _round_up(o, 1024)
    w_in_p = jnp.concatenate(pieces + [jnp.zeros((d, NP - (o - HL + H)), w_in.dtype)], axis=1).astype(BF16)

    wq3 = w_q_up.reshape(QL, MH, QK)
    wq_nope = wq3[:, :, :MLA_NOPE].reshape(QL, MH * MLA_NOPE)
    wq_rope = wq3[:, :, MLA_NOPE:]
    wq_rot = jnp.concatenate([-wq_rope[:, :, half:], wq_rope[:, :, :half]], axis=2)
    wq = jnp.concatenate([wq_nope, wq_rope.reshape(QL, MH * rope), wq_rot.reshape(QL, MH * rope)], axis=1).astype(BF16)

    wkv3 = w_kv_up.reshape(KVL, MH, MLA_NOPE + MLA_V_DIM)
    wkv = jnp.concatenate([wkv3[:, :, :MLA_NOPE].reshape(KVL, MH * MLA_NOPE),
                           wkv3[:, :, MLA_NOPE:].reshape(KVL, MH * MLA_V_DIM)], axis=1).astype(BF16)

    scale = QK ** -0.5
    zeros_h = jnp.zeros((rope,), F32)

    def gains(g, sc):
        gn = (g[:MLA_NOPE] * sc).reshape(1, LANES)
        gr = jnp.stack([jnp.concatenate([g[MLA_NOPE:] * sc, zeros_h]), jnp.concatenate([zeros_h, g[MLA_NOPE:] * sc])])
        return gn.astype(F32), gr.astype(F32)

    q_gn, q_gr = gains(q_norm, scale)
    k_gn, k_gr = gains(k_norm, 1.0)

    pad_h = lambda v: jnp.concatenate([v.astype(F32), jnp.zeros((HL - H,), F32)]).reshape(1, HL)
    head_of_col = np.arange(W) // P
    expand = (np.arange(HL)[:, None] == head_of_col[None, :]).astype(np.float32)
    expand_l = (np.arange(HL)[:, None] == (np.arange(H * Q) // Q)[None, :]).astype(np.float32)
    tri = np.tril(np.ones((Q, Q), np.float32))

    return dict(
        H=H, P=P, G=G, N=N, HL=HL, QL=QL, KVL=KVL, MH=MH, QK=QK, ROPE=rope, NP=NP, offs=offs,
        norm_mix=norm_mix, w_in=w_in_p,
        cwx=conv_w[:, :W], cwb=conv_w[:, W:W + GN], cwc=conv_w[:, W + GN:],
        cbx=conv_b[:W].reshape(1, W), cbb=conv_b[W:W + GN].reshape(1, GN), cbc=conv_b[W + GN:].reshape(1, GN),
        dtb=pad_h(dt_bias), alog=pad_h(a_log),
        dskip_x=jnp.repeat(d_skip.astype(F32), P).reshape(1, W), ssd_gn=ssd_norm.reshape(1, W).astype(F32),
        tri=jnp.asarray(tri, BF16), expand=jnp.asarray(expand, BF16), expand_l=jnp.asarray(expand_l, BF16),
        expand_t=jnp.asarray(expand.T, BF16),
        q_a_norm=q_a_norm.reshape(1, QL).astype(F32), kv_a_norm=kv_a_norm.reshape(1, KVL).astype(F32),
        wq=wq, wkv=wkv, q_gn=q_gn, q_gr=q_gr, k_gn=k_gn, k_gr=k_gr,
        attn_out_norm=attn_out_norm, w_out_a=w_out[:W].astype(BF16), w_out_b=w_out[W:].astype(BF16),
        norm_ffn=norm_ffn, peer_w_q=peer_w_q.astype(BF16), peer_keys=peer_keys.astype(BF16),
        peer_u=peer_u.astype(BF16), peer_vt=peer_v.T.astype(BF16),
    )


def _rope_tables(pos, rope):
    inv = 1.0 / (ROPE_BASE ** (jnp.arange(0, rope, 2, dtype=F32) / rope))
    ang = pos.astype(F32)[:, None] * inv[None, :]
    reps = LANES // (rope // 2)
    return jnp.tile(jnp.cos(ang), (1, reps)), jnp.tile(jnp.sin(ang), (1, reps))


def _layer(x, pos, past_ckv, past_krope, conv_hist, ssm_h0, prm):
    b, l, d = x.shape
    t = b * l
    H, P, G, N = prm["H"], prm["P"], prm["G"], prm["N"]
    W, GN = H * P, G * N
    offs = prm["offs"]
    x2d = x.reshape(t, d)

    h = _rms_cast(x2d, prm["norm_mix"])
    proj = _matmul([(h, prm["w_in"])], tm=1024, tn=512, name="in_proj")
    proj3 = proj.reshape(b, l, prm["NP"])

    hist = (conv_hist[:, :, :W], conv_hist[:, :, W:W + GN], conv_hist[:, :, W + GN:])
    y_ssd, h_last, nx, nb, ncm = _ssd(proj3, offs, hist, ssm_h0.reshape(b, W, N), prm, CHUNK)
    conv_new = jnp.concatenate([nx, nb, ncm], axis=-1)
    h_last = h_last.reshape(b, H, P, N)

    cos4, sin4 = _rope_tables(pos, prm["ROPE"])
    c_new, kr2_new, kr_new = _latent(proj3, offs, prm, cos4, sin4)
    q = _q_proj(proj3, offs, prm, cos4, sin4)
    if past_ckv is None:
        c_all, kr2_all = c_new, kr2_new
    else:
        c_all = jnp.concatenate([past_ckv, c_new], axis=1)
        kr2_all = jnp.concatenate([jnp.concatenate([past_krope, past_krope], axis=-1), kr2_new], axis=1)
    lk = c_all.shape[1]
    k, v = _kv_proj(c_all, kr2_all, prm, tm=_pick(lk, 512, 16))
    if past_ckv is None:
        tq = _pick(l, 256, CHUNK)
        o = _attention(q, k, v, prm["MH"], True, tq, tq)
    else:
        assert l <= CHUNK and (lk - l) % CHUNK == 0
        o = _attention(q, k, v, prm["MH"], False, l, _pick(lk, 1024, 16))
    o_n = _rms_cast(o.reshape(t, -1), prm["attn_out_norm"])

    x_mid = _matmul([(y_ssd.reshape(t, W), prm["w_out_a"]), (o_n, prm["w_out_b"])], res=x2d,
                    tm=512, tn=512, name="out_proj")

    hn = _rms_cast(x_mid, prm["norm_ffn"])
    pq = _matmul([(hn, prm["peer_w_q"])], tm=512, tn=512, name="peer_query")
    sel = _peer_select(pq, prm["peer_keys"])
    yt = _peer_main(hn, prm["peer_u"], prm["peer_vt"], sel)
    out = _transpose_add(x_mid, yt)
    return out.reshape(b, l, d), c_new, kr_new, h_last.astype(ssm_h0.dtype), conv_new


def kernel(x_prompt, x_sample, cache_mla_ckv, cache_mla_krope, state_ssm, state_conv, norm_mix, w_in, conv_w, conv_b, dt_bias, a_log, d_skip, ssd_norm, q_a_norm, w_q_up, kv_a_norm, w_kv_up, q_norm, k_norm, attn_out_norm, w_out, norm_ffn, peer_w_q, peer_keys, peer_u, peer_v):
    bp, lp, _ = x_prompt.shape
    ls = x_sample.shape[1]
    depth = norm_mix.shape[0]
    past = cache_mla_ckv.shape[2]
    rope = cache_mla_krope.shape[-1]
    nstate = state_ssm.shape[-1]
    xbc = state_conv.shape[-1]
    pos_p = jnp.arange(lp, dtype=jnp.int32)
    pos_s = past + jnp.arange(ls, dtype=jnp.int32)
    yp, ys = x_prompt, x_sample
    outs_p, outs_s = [], []
    for layer in range(depth):
        prm = _prepare(norm_mix[layer], w_in[layer], conv_w[layer], conv_b[layer], dt_bias[layer], a_log[layer],
                       d_skip[layer], ssd_norm[layer], q_a_norm[layer], w_q_up[layer], kv_a_norm[layer],
                       w_kv_up[layer], q_norm[layer], k_norm[layer], attn_out_norm[layer], w_out[layer],
                       norm_ffn[layer], peer_w_q[layer], peer_keys[layer], peer_u[layer], peer_v[layer],
                       rope, nstate, CHUNK)
        zero_conv = jnp.zeros((bp, state_conv.shape[2], xbc), x_prompt.dtype)
        zero_ssm = jnp.zeros((bp,) + state_ssm.shape[2:], state_ssm.dtype)
        yp, *rest_p = _layer(yp, pos_p, None, None, zero_conv, zero_ssm, prm)
        outs_p.append(rest_p)
        ys, *rest_s = _layer(ys, pos_s, cache_mla_ckv[layer], cache_mla_krope[layer],
                             state_conv[layer], state_ssm[layer], prm)
        outs_s.append(rest_s)
    stack = lambda outs, i: jnp.stack([o[i] for o in outs])
    return (yp, ys,
            stack(outs_p, 0), stack(outs_p, 1), stack(outs_p, 2), stack(outs_p, 3),
            stack(outs_s, 0), stack(outs_s, 1), stack(outs_s, 2), stack(outs_s, 3))
```

```python
import functools
import math

import numpy as np
import jax
import jax.numpy as jnp
from jax import lax
from jax.experimental import pallas as pl
from jax.experimental.pallas import tpu as pltpu

F32 = jnp.float32
BF16 = jnp.bfloat16

EPS = 1e-6
CHUNK = 64
SSD_HEAD_DIM = 64
MLA_NOPE = 128
MLA_V_DIM = 128
ROPE_BASE = 10000.0
PEER_TOPK = 16
LANES = 128
VMEM_LIMIT = 56 * 1024 * 1024

NT_DIMS = (((1,), (1,)), ((), ()))
TN_DIMS = (((0,), (0,)), ((), ()))


def _pick(n, pref, mult):
    if n <= pref:
        return n
    t = (pref // mult) * mult
    while t >= mult:
        if n % t == 0:
            return t
        t -= mult
    return n


def _round_up(n, m):
    return (n + m - 1) // m * m


def _params(sem, vmem=VMEM_LIMIT):
    return pltpu.CompilerParams(dimension_semantics=sem, vmem_limit_bytes=vmem)


def _split3(x):
    hi = x.astype(BF16)
    r1 = x - hi.astype(F32)
    mid = r1.astype(BF16)
    lo = (r1 - mid.astype(F32)).astype(BF16)
    return hi, mid, lo


def _dot3(a01, x, dims=None):
    out = None
    for piece in _split3(x):
        if dims is None:
            d = jnp.dot(a01, piece, preferred_element_type=F32)
        else:
            d = lax.dot_general(a01, piece, dims, preferred_element_type=F32)
        out = d if out is None else out + d
    return out


def _dot3_right(x, b01, dims=None):
    out = None
    for piece in _split3(x):
        if dims is None:
            d = jnp.dot(piece, b01, preferred_element_type=F32)
        else:
            d = lax.dot_general(piece, b01, dims, preferred_element_type=F32)
        out = d if out is None else out + d
    return out


def _rms_cast_kernel(x_ref, g_ref, o_ref):
    x = x_ref[...].astype(F32)
    ms = jnp.mean(x * x, axis=-1, keepdims=True)
    o_ref[...] = (x * lax.rsqrt(ms + EPS) * g_ref[...]).astype(o_ref.dtype)


def _rms_cast_t_kernel(x_ref, g_ref, o_ref):
    x = x_ref[...].astype(F32)
    ms = jnp.mean(x * x, axis=-1, keepdims=True)
    o_ref[...] = (x * lax.rsqrt(ms + EPS) * g_ref[...]).T.astype(o_ref.dtype)


def _rms_cast(x2d, g, out_dtype=BF16, tm=256, transpose=False):
    t, d = x2d.shape
    tm = _pick(t, tm, LANES if transpose else 16)
    if transpose:
        body, out_spec, out_shape = _rms_cast_t_kernel, pl.BlockSpec((d, tm), lambda i: (0, i)), (d, t)
    else:
        body, out_spec, out_shape = _rms_cast_kernel, pl.BlockSpec((tm, d), lambda i: (i, 0)), (t, d)
    return pl.pallas_call(
        body,
        grid=(t // tm,),
        in_specs=[pl.BlockSpec((tm, d), lambda i: (i, 0)), pl.BlockSpec((1, d), lambda i: (0, 0))],
        out_specs=out_spec,
        out_shape=jax.ShapeDtypeStruct(out_shape, out_dtype),
        compiler_params=_params(("parallel",)),
        name="rms_cast_t" if transpose else "rms_cast",
    )(x2d, g.reshape(1, d).astype(F32))


def _mm_kernel(*refs, n_pairs, has_res):
    o_ref = refs[-1]
    acc = None
    for p in range(n_pairs):
        d = jnp.dot(refs[2 * p][...], refs[2 * p + 1][...], preferred_element_type=F32)
        acc = d if acc is None else acc + d
    if has_res:
        acc = refs[2 * n_pairs][...] + acc
    o_ref[...] = acc.astype(o_ref.dtype)


def _matmul(pairs, res=None, out_dtype=F32, tm=512, tn=512, name="matmul"):
    m = pairs[0][0].shape[0]
    n = pairs[0][1].shape[1]
    tm = _pick(m, tm, 16)
    tn = _pick(n, tn, LANES)
    in_specs, args = [], []
    for a, w in pairs:
        k = a.shape[1]
        in_specs += [pl.BlockSpec((tm, k), lambda i, j: (i, 0)), pl.BlockSpec((k, tn), lambda i, j: (0, j))]
        args += [a, w]
    if res is not None:
        in_specs.append(pl.BlockSpec((tm, tn), lambda i, j: (i, j)))
        args.append(res)
    return pl.pallas_call(
        functools.partial(_mm_kernel, n_pairs=len(pairs), has_res=res is not None),
        grid=(m // tm, n // tn),
        in_specs=in_specs,
        out_specs=pl.BlockSpec((tm, tn), lambda i, j: (i, j)),
        out_shape=jax.ShapeDtypeStruct((m, n), out_dtype),
        compiler_params=_params(("parallel", "arbitrary")),
        name=name,
    )(*args)


def _silu(x):
    return x * jax.nn.sigmoid(x)


def _softplus(x):
    return jnp.maximum(x, 0.0) + jnp.log1p(jnp.exp(-jnp.abs(x)))


def _ssd_kernel(z_ref, xs_ref, bm_ref, cm_ref, dt_ref,
                hx_ref, hb_ref, hc_ref, h0_ref,
                cwx_ref, cwb_ref, cwc_ref, cbx_ref, cbb_ref, cbc_ref,
                dtb_ref, alog_ref, dskip_ref, gn_ref,
                tri_ref, exp_ref, expl_ref, expt_ref,
                y_ref, hlast_ref, nx_ref, nb_ref, nc_ref,
                bx_ref, bb_ref, bc_ref, state_ref, ydiag_ref,
                *, Q, H, P, G, N):
    c = pl.program_id(1)
    nc = pl.num_programs(1)
    R = H // G
    W = H * P

    @pl.when(c == 0)
    def _init():
        for buf, hist in ((bx_ref, hx_ref), (bb_ref, hb_ref), (bc_ref, hc_ref)):
            buf[0:8, :] = jnp.zeros((8, buf.shape[1]), F32)
            buf[5:8, :] = hist[0]
        state_ref[...] = h0_ref[0]

    def conv(buf, x_ref, cw_ref, cb_ref):
        x = x_ref[0]
        buf[8:8 + Q, :] = x
        acc = buf[5:5 + Q, :] * cw_ref[0:1, :]
        acc = acc + buf[6:6 + Q, :] * cw_ref[1:2, :]
        acc = acc + buf[7:7 + Q, :] * cw_ref[2:3, :]
        acc = acc + x * cw_ref[3:4, :]
        buf[0:8, :] = buf[Q:Q + 8, :]
        return _silu(cb_ref[...] + acc)

    xs = conv(bx_ref, xs_ref, cwx_ref, cbx_ref)
    bm = conv(bb_ref, bm_ref, cwb_ref, cbb_ref)
    cm = conv(bc_ref, cm_ref, cwc_ref, cbc_ref)

    @pl.when(c == nc - 1)
    def _tail():
        nx_ref[0] = bx_ref[5:8, :]
        nb_ref[0] = bb_ref[5:8, :]
        nc_ref[0] = bc_ref[5:8, :]

    dt = _softplus(dt_ref[0] + dtb_ref[...])
    a_neg = -jnp.exp(alog_ref[...])
    dta = dt * a_neg
    tri = tri_ref[...]
    acum = _dot3(tri, dta)
    eye = (lax.broadcasted_iota(jnp.int32, (Q, Q), 0) == lax.broadcasted_iota(jnp.int32, (Q, Q), 1)).astype(BF16)
    acum_t = _dot3_right(acum, eye, TN_DIMS)
    dt_t = _dot3_right(dt, eye, TN_DIMS)

    acum_x = _dot3_right(acum, exp_ref[...])
    dt_x = _dot3_right(dt, exp_ref[...])
    acum_l = _dot3_right(acum, expl_ref[...])

    ii = lax.broadcasted_iota(jnp.int32, (Q, Q), 0)
    jj = lax.broadcasted_iota(jnp.int32, (Q, Q), 1)
    causal = ii >= jj
    xs_b = xs.astype(BF16)
    bm_b = bm.astype(BF16)
    cm_b = cm.astype(BF16)
    st_b = state_ref[...].astype(BF16)

    exp_acum_x = jnp.exp(acum_x)
    decay_x = jnp.exp(acum_x[Q - 1:Q, :] - acum_x) * dt_x
    xd_b = (xs * decay_x).astype(BF16)

    last_t = jnp.broadcast_to(acum_t[:, Q - 1:Q], (acum_t.shape[0], N))
    cd_rows = jnp.exp(_dot3(expt_ref[...], last_t))

    for g in range(G):
        cg = cm_b[:, g * N:(g + 1) * N]
        bg = bm_b[:, g * N:(g + 1) * N]
        cb = lax.dot_general(cg, bg, NT_DIMS, preferred_element_type=F32)
        for r in range(R):
            h = g * R + r
            dm = acum_l[:, h * Q:(h + 1) * Q] - acum_t[h:h + 1, :]
            lm = jnp.where(causal, jnp.exp(dm), 0.0)
            wm = (cb * lm * dt_t[h:h + 1, :]).astype(BF16)
            ydiag_ref[:, h * P:(h + 1) * P] = jnp.dot(wm, xs_b[:, h * P:(h + 1) * P],
                                                      preferred_element_type=F32)
        rows = slice(g * R * P, (g + 1) * R * P)
        y_off = lax.dot_general(cg, st_b[rows, :], NT_DIMS, preferred_element_type=F32)
        ydiag_ref[:, rows] = ydiag_ref[:, rows] + y_off * exp_acum_x[:, rows]
        s_new = lax.dot_general(xd_b[:, rows], bg, TN_DIMS, preferred_element_type=F32)
        state_ref[rows, :] = cd_rows[rows, :] * state_ref[rows, :] + s_new

    y = ydiag_ref[...] + dskip_ref[...] * xs
    yg = y * _silu(z_ref[0])
    ms = jnp.mean(yg * yg, axis=-1, keepdims=True)
    y_ref[0] = (yg * lax.rsqrt(ms + EPS) * gn_ref[...]).astype(y_ref.dtype)

    @pl.when(c == nc - 1)
    def _final():
        hlast_ref[0] = state_ref[...]


def _ssd(proj3, offs, hist, h0, prm, Q):
    b, l, _ = proj3.shape
    H, P, G, N = prm["H"], prm["P"], prm["G"], prm["N"]
    W, GN, HL = H * P, G * N, prm["HL"]
    nchunk = l // Q
    hx, hb, hc = hist
    full = lambda shape: pl.BlockSpec(shape, lambda i, j: (0,) * len(shape))
    per_b = lambda shape: pl.BlockSpec(shape, lambda i, j: (i,) + (0,) * (len(shape) - 1))
    col = lambda width, off: pl.BlockSpec((1, Q, width), lambda i, j, o=off // width: (i, j, o))
    in_specs = [
        col(W, offs["z"]), col(W, offs["xs"]), col(GN, offs["bm"]), col(GN, offs["cm"]), col(HL, offs["dt"]),
        per_b((1, 3, W)), per_b((1, 3, GN)), per_b((1, 3, GN)), per_b((1, W, N)),
        full((4, W)), full((4, GN)), full((4, GN)), full((1, W)), full((1, GN)), full((1, GN)),
        full((1, HL)), full((1, HL)), full((1, W)), full((1, W)),
        full((Q, Q)), full((HL, W)), full((HL, H * Q)), full((W, HL)),
    ]
    out_specs = [
        pl.BlockSpec((1, Q, W), lambda i, j: (i, j, 0)),
        per_b((1, W, N)), per_b((1, 3, W)), per_b((1, 3, GN)), per_b((1, 3, GN)),
    ]
    out_shape = [
        jax.ShapeDtypeStruct((b, l, W), BF16),
        jax.ShapeDtypeStruct((b, W, N), F32),
        jax.ShapeDtypeStruct((b, 3, W), F32),
        jax.ShapeDtypeStruct((b, 3, GN), F32),
        jax.ShapeDtypeStruct((b, 3, GN), F32),
    ]
    scratch = [
        pltpu.VMEM((Q + 8, W), F32), pltpu.VMEM((Q + 8, GN), F32), pltpu.VMEM((Q + 8, GN), F32),
        pltpu.VMEM((W, N), F32), pltpu.VMEM((Q, W), F32),
    ]
    return pl.pallas_call(
        functools.partial(_ssd_kernel, Q=Q, H=H, P=P, G=G, N=N),
        grid=(b, nchunk),
        in_specs=in_specs, out_specs=out_specs, out_shape=out_shape, scratch_shapes=scratch,
        compiler_params=_params(("parallel", "arbitrary")),
        name="ssd_scan",
    )(proj3, proj3, proj3, proj3, proj3, hx, hb, hc, h0,
      prm["cwx"], prm["cwb"], prm["cwc"], prm["cbx"], prm["cbb"], prm["cbc"],
      prm["dtb"], prm["alog"], prm["dskip_x"], prm["ssd_gn"],
      prm["tri"], prm["expand"], prm["expand_l"], prm["expand_t"])


def _latent_kernel(ckv_ref, krr_ref, krot_ref, g_ref, cos_ref, sin_ref, c_ref, kr2_ref, kr_ref, *, rope):
    x = ckv_ref[0]
    ms = jnp.mean(x * x, axis=-1, keepdims=True)
    c_ref[0] = x * lax.rsqrt(ms + EPS) * g_ref[...]
    kr2 = krr_ref[0] * cos_ref[...] + krot_ref[0] * sin_ref[...]
    kr2_ref[0] = kr2
    kr_ref[0] = kr2[:, :rope]


def _latent(proj3, offs, prm, cos4, sin4, tm=256):
    b, l, _ = proj3.shape
    kvl, rope = prm["KVL"], prm["ROPE"]
    tm = _pick(l, tm, 8)
    col = lambda width, off: pl.BlockSpec((1, tm, width), lambda i, j, o=off // width: (i, j, o))
    return pl.pallas_call(
        functools.partial(_latent_kernel, rope=rope),
        grid=(b, l // tm),
        in_specs=[col(kvl, offs["ckv"]), col(LANES, offs["krr"]), col(LANES, offs["krot"]),
                  pl.BlockSpec((1, kvl), lambda i, j: (0, 0)),
                  pl.BlockSpec((tm, LANES), lambda i, j: (j, 0)), pl.BlockSpec((tm, LANES), lambda i, j: (j, 0))],
        out_specs=[pl.BlockSpec((1, tm, kvl), lambda i, j: (i, j, 0)),
                   pl.BlockSpec((1, tm, LANES), lambda i, j: (i, j, 0)),
                   pl.BlockSpec((1, tm, rope), lambda i, j: (i, j, 0))],
        out_shape=[jax.ShapeDtypeStruct((b, l, kvl), F32), jax.ShapeDtypeStruct((b, l, LANES), F32),
                   jax.ShapeDtypeStruct((b, l, rope), F32)],
        compiler_params=_params(("parallel", "parallel")),
        name="mla_latent",
    )(proj3, proj3, proj3, prm["kv_a_norm"], cos4, sin4)


def _q_kernel(cq_ref, g_ref, w_ref, cos_ref, sin_ref, gn_ref, gr_ref, q_ref, *, MH, qk_dim):
    x = cq_ref[0]
    ms = jnp.mean(x * x, axis=-1, keepdims=True)
    xn = (x * lax.rsqrt(ms + EPS) * g_ref[...]).astype(BF16)
    qf = jnp.dot(xn, w_ref[...], preferred_element_type=F32)
    cos = cos_ref[...]
    sin = sin_ref[...]
    lane = lax.broadcasted_iota(jnp.int32, (1, LANES), 1)
    half_mask = [(lane < LANES // 2).astype(F32), (lane >= LANES // 2).astype(F32)]
    rope0 = MH * MLA_NOPE
    rot0 = rope0 + MH * (LANES // 2)
    for p in range(MH // 2):
        rp = (qf[:, rope0 + p * LANES: rope0 + (p + 1) * LANES] * cos
              + qf[:, rot0 + p * LANES: rot0 + (p + 1) * LANES] * sin)
        for e in (0, 1):
            h = 2 * p + e
            nope = qf[:, h * MLA_NOPE:(h + 1) * MLA_NOPE]
            rh = rp * half_mask[e]
            ssq = jnp.sum(nope * nope, axis=-1, keepdims=True) + jnp.sum(rh * rh, axis=-1, keepdims=True)
            inv = lax.rsqrt(ssq * (1.0 / qk_dim) + EPS)
            q_ref[0, :, h * 2 * LANES: h * 2 * LANES + LANES] = (nope * inv * gn_ref[...]).astype(q_ref.dtype)
            q_ref[0, :, h * 2 * LANES + LANES:(h + 1) * 2 * LANES] = (rh * inv * gr_ref[e:e + 1, :]).astype(q_ref.dtype)


def _q_proj(proj3, offs, prm, cos4, sin4, tm=256):
    b, l, _ = proj3.shape
    ql, mh = prm["QL"], prm["MH"]
    tm = _pick(l, tm, 16)
    wq = prm["wq"]
    return pl.pallas_call(
        functools.partial(_q_kernel, MH=mh, qk_dim=prm["QK"]),
        grid=(b, l // tm),
        in_specs=[pl.BlockSpec((1, tm, ql), lambda i, j, o=offs["cq"] // ql: (i, j, o)),
                  pl.BlockSpec((1, ql), lambda i, j: (0, 0)),
                  pl.BlockSpec(wq.shape, lambda i, j: (0, 0)),
                  pl.BlockSpec((tm, LANES), lambda i, j: (j, 0)), pl.BlockSpec((tm, LANES), lambda i, j: (j, 0)),
                  pl.BlockSpec((1, LANES), lambda i, j: (0, 0)), pl.BlockSpec((2, LANES), lambda i, j: (0, 0))],
        out_specs=pl.BlockSpec((1, tm, mh * 2 * LANES), lambda i, j: (i, j, 0)),
        out_shape=jax.ShapeDtypeStruct((b, l, mh * 2 * LANES), BF16),
        compiler_params=_params(("parallel", "parallel")),
        name="mla_q",
    )(proj3, prm["q_a_norm"], wq, cos4, sin4, prm["q_gn"], prm["q_gr"])


def _kv_kernel(c_ref, kr2_ref, w_ref, gn_ref, gr_ref, k_ref, v_ref, *, MH, qk_dim):
    c = c_ref[0].astype(BF16)
    kv = jnp.dot(c, w_ref[...], preferred_element_type=F32)
    kr2 = kr2_ref[0]
    lane = lax.broadcasted_iota(jnp.int32, (1, LANES), 1)
    kr_lo = kr2 * (lane < LANES // 2).astype(F32)
    kr_ss = jnp.sum(kr_lo * kr_lo, axis=-1, keepdims=True)
    for h in range(MH):
        nope = kv[:, h * MLA_NOPE:(h + 1) * MLA_NOPE]
        ssq = jnp.sum(nope * nope, axis=-1, keepdims=True) + kr_ss
        inv = lax.rsqrt(ssq * (1.0 / qk_dim) + EPS)
        k_ref[0, :, h * 2 * LANES: h * 2 * LANES + LANES] = (nope * inv * gn_ref[...]).astype(k_ref.dtype)
        k_ref[0, :, h * 2 * LANES + LANES:(h + 1) * 2 * LANES] = (kr2 * inv * gr_ref[h % 2:h % 2 + 1, :]).astype(k_ref.dtype)
    v_ref[0] = kv[:, MH * MLA_NOPE:].astype(v_ref.dtype)


def _kv_proj(c_all, kr2_all, prm, tm=256):
    b, lk, kvl = c_all.shape
    mh = prm["MH"]
    tm = _pick(lk, tm, 16)
    wkv = prm["wkv"]
    return pl.pallas_call(
        functools.partial(_kv_kernel, MH=mh, qk_dim=prm["QK"]),
        grid=(b, lk // tm),
        in_specs=[pl.BlockSpec((1, tm, kvl), lambda i, j: (i, j, 0)),
                  pl.BlockSpec((1, tm, LANES), lambda i, j: (i, j, 0)),
                  pl.BlockSpec(wkv.shape, lambda i, j: (0, 0)),
                  pl.BlockSpec((1, LANES), lambda i, j: (0, 0)), pl.BlockSpec((2, LANES), lambda i, j: (0, 0))],
        out_specs=[pl.BlockSpec((1, tm, mh * 2 * LANES), lambda i, j: (i, j, 0)),
                   pl.BlockSpec((1, tm, mh * MLA_V_DIM), lambda i, j: (i, j, 0))],
        out_shape=[jax.ShapeDtypeStruct((b, lk, mh * 2 * LANES), BF16),
                   jax.ShapeDtypeStruct((b, lk, mh * MLA_V_DIM), BF16)],
        compiler_params=_params(("parallel", "parallel")),
        name="mla_kv",
    )(c_all, kr2_all, wkv, prm["k_gn"], prm["k_gr"])


def _attn_kernel(q_ref, k_ref, v_ref, o_ref, *, causal, tq, lq):
    for qi in range(lq // tq):
        rows = slice(qi * tq, (qi + 1) * tq)
        q = q_ref[0, rows, :]
        if causal:
            past = qi * tq
            sd = lax.dot_general(q, k_ref[0, past:past + tq, :], NT_DIMS, preferred_element_type=F32)
            ri = lax.broadcasted_iota(jnp.int32, (tq, tq), 0) // CHUNK
            ci = lax.broadcasted_iota(jnp.int32, (tq, tq), 1) // CHUNK
            sd = jnp.where(ri >= ci, sd, -jnp.inf)
            m = jnp.max(sd, axis=-1, keepdims=True)
            if past > 0:
                sp = lax.dot_general(q, k_ref[0, 0:past, :], NT_DIMS, preferred_element_type=F32)
                m = jnp.maximum(m, jnp.max(sp, axis=-1, keepdims=True))
                pp = jnp.exp(sp - m)
                l = jnp.sum(pp, axis=-1, keepdims=True)
                acc = jnp.dot(pp.astype(BF16), v_ref[0, 0:past, :], preferred_element_type=F32)
            pd = jnp.exp(sd - m)
            ld = jnp.sum(pd, axis=-1, keepdims=True)
            accd = jnp.dot(pd.astype(BF16), v_ref[0, past:past + tq, :], preferred_element_type=F32)
            if past > 0:
                l, acc = l + ld, acc + accd
            else:
                l, acc = ld, accd
        else:
            s = lax.dot_general(q, k_ref[0], NT_DIMS, preferred_element_type=F32)
            m = jnp.max(s, axis=-1, keepdims=True)
            p = jnp.exp(s - m)
            l = jnp.sum(p, axis=-1, keepdims=True)
            acc = jnp.dot(p.astype(BF16), v_ref[0], preferred_element_type=F32)
        o_ref[0, rows, :] = acc / l


def _attention(q, k, v, mh, causal, tq):
    b, lq, _ = q.shape
    lk = k.shape[1]
    return pl.pallas_call(
        functools.partial(_attn_kernel, causal=causal, tq=tq, lq=lq),
        grid=(b, mh),
        in_specs=[pl.BlockSpec((1, lq, 2 * LANES), lambda bi, h: (bi, 0, h)),
                  pl.BlockSpec((1, lk, 2 * LANES), lambda bi, h: (bi, 0, h)),
                  pl.BlockSpec((1, lk, MLA_V_DIM), lambda bi, h: (bi, 0, h))],
        out_specs=pl.BlockSpec((1, lq, MLA_V_DIM), lambda bi, h: (bi, 0, h)),
        out_shape=jax.ShapeDtypeStruct((b, lq, mh * MLA_V_DIM), F32),
        compiler_params=_params(("parallel", "parallel")),
        name="mla_attention",
    )(q, k, v)


def _top_values(x, k):
    vals, cnts = [], []
    for _ in range(k):
        mx = jnp.max(x, axis=0, keepdims=True)
        eq = x == mx
        vals.append(mx)
        cnts.append(jnp.sum(jnp.where(eq, 1.0, 0.0), axis=0, keepdims=True))
        x = jnp.where(eq, -jnp.inf, x)
    return vals, cnts


def _peer_select_kernel(pq_ref, keys_ref, s1_ref, s2_ref, e1_ref, e2_ref, tau_ref, *, PH, HALF, TOPK):
    for h in range(PH):
        q1 = pq_ref[h * 2 * HALF: h * 2 * HALF + HALF, :].astype(BF16)
        q2 = pq_ref[h * 2 * HALF + HALF:(h + 1) * 2 * HALF, :].astype(BF16)
        s1 = jnp.dot(keys_ref[h, 0], q1, preferred_element_type=F32)
        s2 = jnp.dot(keys_ref[h, 1], q2, preferred_element_type=F32)
        v1, c1 = _top_values(s1, TOPK)
        v2, c2 = _top_values(s2, TOPK)
        v2m = jnp.concatenate(v2, axis=0)
        c2m = jnp.concatenate(c2, axis=0)
        cand = jnp.concatenate([v1[a] + v2m for a in range(TOPK)], axis=0)
        wgt = jnp.concatenate([c1[a] * c2m for a in range(TOPK)], axis=0)
        m = v1[0] + v2[0]
        cnt = jnp.zeros_like(m)
        tau = m
        zsum = jnp.zeros_like(m)
        for _ in range(TOPK):
            mx = jnp.max(cand, axis=0, keepdims=True)
            eq = cand == mx
            cw = jnp.sum(jnp.where(eq, wgt, 0.0), axis=0, keepdims=True)
            need = cnt < float(TOPK)
            tau = jnp.where(need, mx, tau)
            zsum = zsum + jnp.where(need, cw * jnp.exp(mx - m), 0.0)
            cnt = cnt + cw
            cand = jnp.where(eq, -jnp.inf, cand)
        s1_ref[h] = s1
        s2_ref[h] = s2
        e1_ref[h] = jnp.exp(s1 - v1[0])
        e2_ref[h] = jnp.exp(s2 - v2[0]) / zsum
        tau_ref[h] = tau


def _peer_select(pq_t, keys_b, tt=256):
    t = pq_t.shape[1]
    ph, _, nk, half = keys_b.shape
    tt = _pick(t, tt, LANES)
    big = lambda: pl.BlockSpec((ph, nk, tt), lambda i: (0, 0, i))
    shp = jax.ShapeDtypeStruct((ph, nk, t), F32)
    return pl.pallas_call(
        functools.partial(_peer_select_kernel, PH=ph, HALF=half, TOPK=PEER_TOPK),
        grid=(t // tt,),
        in_specs=[pl.BlockSpec((ph * 2 * half, tt), lambda i: (0, i)),
                  pl.BlockSpec(keys_b.shape, lambda i: (0, 0, 0, 0))],
        out_specs=[big(), big(), big(), big(), pl.BlockSpec((ph, 1, tt), lambda i: (0, 0, i))],
        out_shape=[shp, shp, shp, shp, jax.ShapeDtypeStruct((ph, 1, t), F32)],
        compiler_params=_params(("parallel",)),
        name="peer_select",
    )(pq_t, keys_b)


def _gelu(x):
    return 0.5 * x * (1.0 + lax.erf(x * (1.0 / math.sqrt(2.0))))


ROWS_BF16 = 16
GATE_ROWS = 128


def _peer_main_kernel(hn_ref, u_ref, vt_ref, s1_ref, s2_ref, e1_ref, e2_ref, tau_ref, out_ref,
                      ata_ref, atb_ref, cta_ref, ctb_ref, *, PH, NK, NI, NJ):
    s = pl.program_id(0)
    tt = ata_ref.shape[1]

    @pl.when(s == 0)
    def _init():
        for ref in (ata_ref, atb_ref, cta_ref, ctb_ref):
            ref[...] = jnp.zeros(ref.shape, ref.dtype)

    @pl.when(lax.rem(jnp.maximum(s - 2, 0), NJ) == 0)
    def _zero_out():
        out_ref[...] = jnp.zeros(out_ref.shape, F32)

    def step(at_new, at_prev, ct_new, ct_prev):
        d = out_ref.shape[0]
        blk = lax.rem(jnp.maximum(s - 1, 0), NJ)

        def mix_rows(m0, m1):
            out_ref[m0:m1, :] += jnp.dot(vt_ref[m0:m1, :], ct_prev[...], preferred_element_type=F32)

        def score_cols(n0, n1):
            at_new[:, n0:n1] = jnp.dot(u_ref[...], hn_ref[:, n0:n1], preferred_element_type=F32)

        def gate_tile(ii, tc, r0):
            row = blk * NI + ii
            cols = slice(tc * LANES, (tc + 1) * LANES)
            groups = [slice(r0 + k * ROWS_BF16, r0 + (k + 1) * ROWS_BF16) for k in range(GATE_ROWS // ROWS_BF16)]
            g = [None] * len(groups)
            for h in range(PH):
                s1r = s1_ref[h, pl.ds(row, 1), :][:, cols]
                e1r = e1_ref[h, pl.ds(row, 1), :][:, cols]
                tau = tau_ref[h, :, cols]
                for k, rws in enumerate(groups):
                    sel = (s1r + s2_ref[h, rws, cols]) >= tau
                    term = e1r * jnp.where(sel, e2_ref[h, rws, cols], 0.0)
                    g[k] = term if g[k] is None else g[k] + term
            for k, rws in enumerate(groups):
                orow = slice(ii * NK + rws.start, ii * NK + rws.stop)
                ct_new[orow, cols] = (_gelu(at_prev[orow, cols]) * g[k]).astype(BF16)

        tiles = [(ii, tc, r0) for ii in range(NI) for tc in range(tt // LANES) for r0 in range(0, NK, GATE_ROWS)]
        n_mix, n_score = 4, max(1, min(2, tt // (2 * LANES)))
        mxu_items = [functools.partial(mix_rows, k * d // n_mix, (k + 1) * d // n_mix) for k in range(n_mix)]
        mxu_items += [functools.partial(score_cols, k * tt // n_score, (k + 1) * tt // n_score)
                      for k in range(n_score)]
        weights = [1] * n_mix + [n_mix // n_score] * n_score
        done, total = 0, sum(weights)
        for item, w in zip(mxu_items, weights):
            upto = len(tiles) * (done + w) // total
            for tile_args in tiles[len(tiles) * done // total: upto]:
                gate_tile(*tile_args)
            item()
            done += w

    @pl.when(s % 2 == 0)
    def _even():
        step(ata_ref, atb_ref, ctb_ref, cta_ref)

    @pl.when(s % 2 == 1)
    def _odd():
        step(atb_ref, ata_ref, cta_ref, ctb_ref)


def _peer_main(hn_t, u_b, vt_b, sel, tt=512, ne=512):
    d, t = hn_t.shape
    s1, s2, e1, e2, tau = sel
    ph, nk, _ = s1.shape
    nexp = u_b.shape[0]
    tt = _pick(t, tt, LANES)
    ne = _pick(nexp, ne, nk)
    ni = ne // nk
    nj = nexp // ne
    n_tiles = t // tt
    n_blocks = n_tiles * nj
    assert nj % 2 == 0
    once = dict(pipeline_mode=pl.Buffered(1))
    tile = lambda s, lag: jnp.minimum(jnp.maximum(s - lag, 0) // nj, n_tiles - 1)
    big = lambda: pl.BlockSpec((ph, nk, tt), lambda s: (0, 0, tile(s, 1)), **once)
    return pl.pallas_call(
        functools.partial(_peer_main_kernel, PH=ph, NK=nk, NI=ni, NJ=nj),
        grid=(n_blocks + 2,),
        in_specs=[pl.BlockSpec((d, tt), lambda s: (0, tile(s, 0)), **once),
                  pl.BlockSpec((ne, d), lambda s: (lax.rem(s, nj), 0)),
                  pl.BlockSpec((d, ne), lambda s: (0, lax.rem(jnp.maximum(s - 2, 0), nj))),
                  big(), big(), big(), big(),
                  pl.BlockSpec((ph, 1, tt), lambda s: (0, 0, tile(s, 1)), **once)],
        out_specs=pl.BlockSpec((d, tt), lambda s: (0, tile(s, 2))),
        out_shape=jax.ShapeDtypeStruct((d, t), F32),
        scratch_shapes=[pltpu.VMEM((ne, tt), F32), pltpu.VMEM((ne, tt), F32),
                        pltpu.VMEM((ne, tt), BF16), pltpu.VMEM((ne, tt), BF16)],
        compiler_params=_params(("arbitrary",)),
        name="peer_main",
    )(hn_t, u_b, vt_b, s1, s2, e1, e2, tau)


def _tadd_kernel(x_ref, yt_ref, o_ref):
    o_ref[...] = x_ref[...] + yt_ref[...].T


def _transpose_add(x2d, yt, tm=256):
    t, d = x2d.shape
    tm = _pick(t, tm, LANES)
    return pl.pallas_call(
        _tadd_kernel,
        grid=(t // tm,),
        in_specs=[pl.BlockSpec((tm, d), lambda i: (i, 0)), pl.BlockSpec((d, tm), lambda i: (0, i))],
        out_specs=pl.BlockSpec((tm, d), lambda i: (i, 0)),
        out_shape=jax.ShapeDtypeStruct((t, d), F32),
        compiler_params=_params(("parallel",)),
        name="peer_residual",
    )(x2d, yt)


def _prepare(norm_mix, w_in, conv_w, conv_b, dt_bias, a_log, d_skip, ssd_norm,
             q_a_norm, w_q_up, kv_a_norm, w_kv_up, q_norm, k_norm, attn_out_norm,
             w_out, norm_ffn, peer_w_q, peer_keys, peer_u, peer_v, rope, N, Q):
    d = w_in.shape[0]
    W = ssd_norm.shape[0]
    H = dt_bias.shape[0]
    P = W // H
    xbc = conv_w.shape[1]
    GN = (xbc - W) // 2
    G = GN // N
    QL = q_a_norm.shape[0]
    KVL = kv_a_norm.shape[0]
    MW = attn_out_norm.shape[0]
    MH = MW // MLA_V_DIM
    QK = MLA_NOPE + rope
    HL = _round_up(H, LANES)
    half = rope // 2
    assert 2 * rope == LANES and MH % 2 == 0 and P == SSD_HEAD_DIM

    s = np.cumsum([0, W, xbc, H, QL, KVL, rope])
    wz, wxbc, wdt, wcq, wckv, wkr = (w_in[:, s[i]:s[i + 1]] for i in range(6))
    wrot = jnp.concatenate([-wkr[:, half:], wkr[:, :half]], axis=1)
    pieces = [wz, wxbc, wcq, wckv, wkr, wkr, wrot, wrot, wdt]
    offs, o = {}, 0
    for name, width in (("z", W), ("xs", W), ("bm", GN), ("cm", GN), ("cq", QL), ("ckv", KVL),
                        ("krr", LANES), ("krot", LANES), ("dt", HL)):
        assert o % width == 0, (name, o, width)
        offs[name] = o
        o += width
    NP = _round_up(o, 1024)
    w_in_p = jnp.concatenate(pieces + [jnp.zeros((d, NP - (o - HL + H)), w_in.dtype)], axis=1).astype(BF16)

    wq3 = w_q_up.reshape(QL, MH, QK)
    wq_nope = wq3[:, :, :MLA_NOPE].reshape(QL, MH * MLA_NOPE)
    wq_rope = wq3[:, :, MLA_NOPE:]
    wq_rot = jnp.concatenate([-wq_rope[:, :, half:], wq_rope[:, :, :half]], axis=2)
    wq = jnp.concatenate([wq_nope, wq_rope.reshape(QL, MH * rope), wq_rot.reshape(QL, MH * rope)], axis=1).astype(BF16)

    wkv3 = w_kv_up.reshape(KVL, MH, MLA_NOPE + MLA_V_DIM)
    wkv = jnp.concatenate([wkv3[:, :, :MLA_NOPE].reshape(KVL, MH * MLA_NOPE),
                           wkv3[:, :, MLA_NOPE:].reshape(KVL, MH * MLA_V_DIM)], axis=1).astype(BF16)

    scale = QK ** -0.5
    zeros_h = jnp.zeros((rope,), F32)

    def gains(g, sc):
        gn = (g[:MLA_NOPE] * sc).reshape(1, LANES)
        gr = jnp.stack([jnp.concatenate([g[MLA_NOPE:] * sc, zeros_h]), jnp.concatenate([zeros_h, g[MLA_NOPE:] * sc])])
        return gn.astype(F32), gr.astype(F32)

    q_gn, q_gr = gains(q_norm, scale)
    k_gn, k_gr = gains(k_norm, 1.0)

    pad_h = lambda v: jnp.concatenate([v.astype(F32), jnp.zeros((HL - H,), F32)]).reshape(1, HL)
    head_of_col = np.arange(W) // P
    expand = (np.arange(HL)[:, None] == head_of_col[None, :]).astype(np.float32)
    expand_l = (np.arange(HL)[:, None] == (np.arange(H * Q) // Q)[None, :]).astype(np.float32)
    tri = np.tril(np.ones((Q, Q), np.float32))

    return dict(
        H=H, P=P, G=G, N=N, HL=HL, QL=QL, KVL=KVL, MH=MH, QK=QK, ROPE=rope, NP=NP, offs=offs,
        norm_mix=norm_mix, w_in=w_in_p,
        cwx=conv_w[:, :W], cwb=conv_w[:, W:W + GN], cwc=conv_w[:, W + GN:],
        cbx=conv_b[:W].reshape(1, W), cbb=conv_b[W:W + GN].reshape(1, GN), cbc=conv_b[W + GN:].reshape(1, GN),
        dtb=pad_h(dt_bias), alog=pad_h(a_log),
        dskip_x=jnp.repeat(d_skip.astype(F32), P).reshape(1, W), ssd_gn=ssd_norm.reshape(1, W).astype(F32),
        tri=jnp.asarray(tri, BF16), expand=jnp.asarray(expand, BF16), expand_l=jnp.asarray(expand_l, BF16),
        expand_t=jnp.asarray(expand.T, BF16),
        q_a_norm=q_a_norm.reshape(1, QL).astype(F32), kv_a_norm=kv_a_norm.reshape(1, KVL).astype(F32),
        wq=wq, wkv=wkv, q_gn=q_gn, q_gr=q_gr, k_gn=k_gn, k_gr=k_gr,
        attn_out_norm=attn_out_norm, w_out_a=w_out[:W].astype(BF16), w_out_b=w_out[W:].astype(BF16),
        norm_ffn=norm_ffn, peer_w_q_t=peer_w_q.T.astype(BF16), peer_keys=peer_keys.astype(BF16),
        peer_u=peer_u.astype(BF16), peer_vt=peer_v.T.astype(BF16),
    )


def _rope_tables(pos, rope):
    inv = 1.0 / (ROPE_BASE ** (jnp.arange(0, rope, 2, dtype=F32) / rope))
    ang = pos.astype(F32)[:, None] * inv[None, :]
    reps = LANES // (rope // 2)
    return jnp.tile(jnp.cos(ang), (1, reps)), jnp.tile(jnp.sin(ang), (1, reps))


def _layer(x, pos, past_ckv, past_krope, conv_hist, ssm_h0, prm):
    b, l, d = x.shape
    t = b * l
    H, P, G, N = prm["H"], prm["P"], prm["G"], prm["N"]
    W, GN = H * P, G * N
    offs = prm["offs"]
    x2d = x.reshape(t, d)

    h = _rms_cast(x2d, prm["norm_mix"])
    proj = _matmul([(h, prm["w_in"])], tm=1024, tn=512, name="in_proj")
    proj3 = proj.reshape(b, l, prm["NP"])

    hist = (conv_hist[:, :, :W], conv_hist[:, :, W:W + GN], conv_hist[:, :, W + GN:])
    y_ssd, h_last, nx, nb, ncm = _ssd(proj3, offs, hist, ssm_h0.reshape(b, W, N), prm, CHUNK)
    conv_new = jnp.concatenate([nx, nb, ncm], axis=-1)
    h_last = h_last.reshape(b, H, P, N)

    cos4, sin4 = _rope_tables(pos, prm["ROPE"])
    c_new, kr2_new, kr_new = _latent(proj3, offs, prm, cos4, sin4)
    q = _q_proj(proj3, offs, prm, cos4, sin4)
    if past_ckv is None:
        c_all, kr2_all = c_new, kr2_new
    else:
        c_all = jnp.concatenate([past_ckv, c_new], axis=1)
        kr2_all = jnp.concatenate([jnp.concatenate([past_krope, past_krope], axis=-1), kr2_new], axis=1)
    lk = c_all.shape[1]
    k, v = _kv_proj(c_all, kr2_all, prm, tm=_pick(lk, 512, 16))
    if past_ckv is None:
        o = _attention(q, k, v, prm["MH"], True, _pick(l, 256, CHUNK))
    else:
        assert l <= CHUNK and (lk - l) % CHUNK == 0
        o = _attention(q, k, v, prm["MH"], False, l)
    o_n = _rms_cast(o.reshape(t, -1), prm["attn_out_norm"])

    x_mid = _matmul([(y_ssd.reshape(t, W), prm["w_out_a"]), (o_n, prm["w_out_b"])], res=x2d,
                    tm=512, tn=512, name="out_proj")

    hn_t = _rms_cast(x_mid, prm["norm_ffn"], transpose=True)
    pq_t = _matmul([(prm["peer_w_q_t"], hn_t)], tm=1024, tn=512, name="peer_query")
    sel = _peer_select(pq_t, prm["peer_keys"])
    yt = _peer_main(hn_t, prm["peer_u"], prm["peer_vt"], sel)
    out = _transpose_add(x_mid, yt)
    return out.reshape(b, l, d), c_new, kr_new, h_last.astype(ssm_h0.dtype), conv_new


def kernel(x_prompt, x_sample, cache_mla_ckv, cache_mla_krope, state_ssm, state_conv, norm_mix, w_in, conv_w, conv_b, dt_bias, a_log, d_skip, ssd_norm, q_a_norm, w_q_up, kv_a_norm, w_kv_up, q_norm, k_norm, attn_out_norm, w_out, norm_ffn, peer_w_q, peer_keys, peer_u, peer_v):
    bp, lp, _ = x_prompt.shape
    ls = x_sample.shape[1]
    depth = norm_mix.shape[0]
    past = cache_mla_ckv.shape[2]
    rope = cache_mla_krope.shape[-1]
    nstate = state_ssm.shape[-1]
    xbc = state_conv.shape[-1]
    pos_p = jnp.arange(lp, dtype=jnp.int32)
    pos_s = past + jnp.arange(ls, dtype=jnp.int32)
    yp, ys = x_prompt, x_sample
    outs_p, outs_s = [], []
    for layer in range(depth):
        prm = _prepare(norm_mix[layer], w_in[layer], conv_w[layer], conv_b[layer], dt_bias[layer], a_log[layer],
                       d_skip[layer], ssd_norm[layer], q_a_norm[layer], w_q_up[layer], kv_a_norm[layer],
                       w_kv_up[layer], q_norm[layer], k_norm[layer], attn_out_norm[layer], w_out[layer],
                       norm_ffn[layer], peer_w_q[layer], peer_keys[layer], peer_u[layer], peer_v[layer],
                       rope, nstate, CHUNK)
        zero_conv = jnp.zeros((bp, state_conv.shape[2], xbc), x_prompt.dtype)
        zero_ssm = jnp.zeros((bp,) + state_ssm.shape[2:], state_ssm.dtype)
        yp, *rest_p = _layer(yp, pos_p, None, None, zero_conv, zero_ssm, prm)
        outs_p.append(rest_p)
        ys, *rest_s = _layer(ys, pos_s, cache_mla_ckv[layer], cache_mla_krope[layer],
                             state_conv[layer], state_ssm[layer], prm)
        outs_s.append(rest_s)
    stack = lambda outs, i: jnp.stack([o[i] for o in outs])
    return (yp, ys,
            stack(outs_p, 0), stack(outs_p, 1), stack(outs_p, 2), stack(outs_p, 3),
            stack(outs_s, 0), stack(outs_s, 1), stack(outs_s, 2), stack(outs_s, 3))
```

```python
import functools
import math

import numpy as np
import jax
import jax.numpy as jnp
from jax import lax
from jax.experimental import pallas as pl
from jax.experimental.pallas import tpu as pltpu

F32 = jnp.float32
BF16 = jnp.bfloat16

EPS = 1e-6
CHUNK = 64
SSD_CHUNK = 128
SSD_HEAD_DIM = 64
MLA_NOPE = 128
MLA_V_DIM = 128
ROPE_BASE = 10000.0
PEER_TOPK = 16
LANES = 128
SUBLANES = 8
VMEM_LIMIT = 56 * 1024 * 1024

NT_DIMS = (((1,), (1,)), ((), ()))
TN_DIMS = (((0,), (0,)), ((), ()))


def _pick(n, pref, mult):
    if n <= pref:
        return n
    t = (pref // mult) * mult
    while t >= mult:
        if n % t == 0:
            return t
        t -= mult
    return n


def _round_up(n, m):
    return (n + m - 1) // m * m


def _params(sem, vmem=VMEM_LIMIT):
    return pltpu.CompilerParams(dimension_semantics=sem, vmem_limit_bytes=vmem)


def _split3(x):
    hi = x.astype(BF16)
    r1 = x - hi.astype(F32)
    mid = r1.astype(BF16)
    lo = (r1 - mid.astype(F32)).astype(BF16)
    return hi, mid, lo


def _dot3(a01, x, dims=None):
    out = None
    for piece in _split3(x):
        if dims is None:
            d = jnp.dot(a01, piece, preferred_element_type=F32)
        else:
            d = lax.dot_general(a01, piece, dims, preferred_element_type=F32)
        out = d if out is None else out + d
    return out


def _dot3_right(x, b01, dims=None):
    out = None
    for piece in _split3(x):
        if dims is None:
            d = jnp.dot(piece, b01, preferred_element_type=F32)
        else:
            d = lax.dot_general(piece, b01, dims, preferred_element_type=F32)
        out = d if out is None else out + d
    return out


def _rms_cast_kernel(x_ref, g_ref, o_ref):
    x = x_ref[...].astype(F32)
    ms = jnp.mean(x * x, axis=-1, keepdims=True)
    o_ref[...] = (x * lax.rsqrt(ms + EPS) * g_ref[...]).astype(o_ref.dtype)


def _rms_cast_t_kernel(x_ref, g_ref, o_ref):
    x = x_ref[...].astype(F32)
    ms = jnp.mean(x * x, axis=-1, keepdims=True)
    o_ref[...] = (x * lax.rsqrt(ms + EPS) * g_ref[...]).T.astype(o_ref.dtype)


def _rms_cast(x2d, g, out_dtype=BF16, tm=256, transpose=False):
    t, d = x2d.shape
    tm = _pick(t, tm, LANES if transpose else 16)
    if transpose:
        body, out_spec, out_shape = _rms_cast_t_kernel, pl.BlockSpec((d, tm), lambda i: (0, i)), (d, t)
    else:
        body, out_spec, out_shape = _rms_cast_kernel, pl.BlockSpec((tm, d), lambda i: (i, 0)), (t, d)
    return pl.pallas_call(
        body,
        grid=(t // tm,),
        in_specs=[pl.BlockSpec((tm, d), lambda i: (i, 0)), pl.BlockSpec((1, d), lambda i: (0, 0))],
        out_specs=out_spec,
        out_shape=jax.ShapeDtypeStruct(out_shape, out_dtype),
        compiler_params=_params(("parallel",)),
        name="rms_cast_t" if transpose else "rms_cast",
    )(x2d, g.reshape(1, d).astype(F32))


def _mm_kernel(*refs, n_pairs, has_res):
    o_ref = refs[-1]
    acc = None
    for p in range(n_pairs):
        d = jnp.dot(refs[2 * p][...], refs[2 * p + 1][...], preferred_element_type=F32)
        acc = d if acc is None else acc + d
    if has_res:
        acc = refs[2 * n_pairs][...] + acc
    o_ref[...] = acc.astype(o_ref.dtype)


def _matmul(pairs, res=None, out_dtype=F32, tm=512, tn=512, name="matmul"):
    m = pairs[0][0].shape[0]
    n = pairs[0][1].shape[1]
    tm = _pick(m, tm, 16)
    tn = _pick(n, tn, LANES)
    in_specs, args = [], []
    for a, w in pairs:
        k = a.shape[1]
        in_specs += [pl.BlockSpec((tm, k), lambda i, j: (i, 0)), pl.BlockSpec((k, tn), lambda i, j: (0, j))]
        args += [a, w]
    if res is not None:
        in_specs.append(pl.BlockSpec((tm, tn), lambda i, j: (i, j)))
        args.append(res)
    return pl.pallas_call(
        functools.partial(_mm_kernel, n_pairs=len(pairs), has_res=res is not None),
        grid=(m // tm, n // tn),
        in_specs=in_specs,
        out_specs=pl.BlockSpec((tm, tn), lambda i, j: (i, j)),
        out_shape=jax.ShapeDtypeStruct((m, n), out_dtype),
        compiler_params=_params(("parallel", "arbitrary")),
        name=name,
    )(*args)


def _silu(x):
    return (0.5 * x) * (1.0 + jnp.tanh(0.5 * x))


def _softplus(x):
    return jnp.maximum(x, 0.0) + jnp.log1p(jnp.exp(-jnp.abs(x)))


def _ssd_kernel(z_ref, xs_ref, bm_ref, cm_ref, dt_ref,
                hx_ref, hb_ref, hc_ref, h0_ref,
                cwx_ref, cwb_ref, cwc_ref, cbx_ref, cbb_ref, cbc_ref,
                dtb_ref, alog_ref, dskip_ref, gn_ref,
                tri_ref, exp_ref, expl_ref, expt_ref,
                y_ref, hlast_ref, nx_ref, nb_ref, nc_ref,
                bx_ref, bb_ref, bc_ref, state_ref, ydiag_ref,
                *, Q, H, P, G, N):
    c = pl.program_id(1)
    nc = pl.num_programs(1)
    R = H // G
    W = H * P

    @pl.when(c == 0)
    def _init():
        for buf, hist in ((bx_ref, hx_ref), (bb_ref, hb_ref), (bc_ref, hc_ref)):
            buf[0:8, :] = jnp.zeros((8, buf.shape[1]), F32)
            buf[5:8, :] = hist[0]
        state_ref[...] = h0_ref[0]

    def conv(buf, x_ref, cw_ref, cb_ref):
        x = x_ref[0]
        buf[8:8 + Q, :] = x
        acc = buf[5:5 + Q, :] * cw_ref[0:1, :]
        acc = acc + buf[6:6 + Q, :] * cw_ref[1:2, :]
        acc = acc + buf[7:7 + Q, :] * cw_ref[2:3, :]
        acc = acc + x * cw_ref[3:4, :]
        buf[0:8, :] = buf[Q:Q + 8, :]
        return _silu(cb_ref[...] + acc)

    xs = conv(bx_ref, xs_ref, cwx_ref, cbx_ref)
    bm = conv(bb_ref, bm_ref, cwb_ref, cbb_ref)
    cm = conv(bc_ref, cm_ref, cwc_ref, cbc_ref)

    @pl.when(c == nc - 1)
    def _tail():
        nx_ref[0] = bx_ref[5:8, :]
        nb_ref[0] = bb_ref[5:8, :]
        nc_ref[0] = bc_ref[5:8, :]

    dt = _softplus(dt_ref[0] + dtb_ref[...])
    a_neg = -jnp.exp(alog_ref[...])
    dta = dt * a_neg
    tri = tri_ref[...]
    acum = _dot3(tri, dta)
    eye = (lax.broadcasted_iota(jnp.int32, (Q, Q), 0) == lax.broadcasted_iota(jnp.int32, (Q, Q), 1)).astype(BF16)
    acum_t = _dot3_right(acum, eye, TN_DIMS)
    dt_t = _dot3_right(dt, eye, TN_DIMS)

    acum_x = _dot3_right(acum, exp_ref[...])
    dt_x = _dot3_right(dt, exp_ref[...])
    acum_l = _dot3_right(acum, expl_ref[...])

    ii = lax.broadcasted_iota(jnp.int32, (Q, Q), 0)
    jj = lax.broadcasted_iota(jnp.int32, (Q, Q), 1)
    causal = ii >= jj
    xs_b = xs.astype(BF16)
    bm_b = bm.astype(BF16)
    cm_b = cm.astype(BF16)
    st_b = state_ref[...].astype(BF16)

    exp_acum_x = jnp.exp(acum_x)
    decay_x = jnp.exp(acum_x[Q - 1:Q, :] - acum_x) * dt_x
    xd_b = (xs * decay_x).astype(BF16)

    last_t = jnp.broadcast_to(acum_t[:, Q - 1:Q], (acum_t.shape[0], N))
    cd_rows = jnp.exp(_dot3(expt_ref[...], last_t))

    for g in range(G):
        cg = cm_b[:, g * N:(g + 1) * N]
        bg = bm_b[:, g * N:(g + 1) * N]
        cb = lax.dot_general(cg, bg, NT_DIMS, preferred_element_type=F32)
        for r in range(R):
            h = g * R + r
            dm = acum_l[:, h * Q:(h + 1) * Q] - acum_t[h:h + 1, :]
            lm = jnp.where(causal, jnp.exp(dm), 0.0)
            wm = (cb * lm * dt_t[h:h + 1, :]).astype(BF16)
            ydiag_ref[:, h * P:(h + 1) * P] = jnp.dot(wm, xs_b[:, h * P:(h + 1) * P],
                                                      preferred_element_type=F32)
        rows = slice(g * R * P, (g + 1) * R * P)
        y_off = lax.dot_general(cg, st_b[rows, :], NT_DIMS, preferred_element_type=F32)
        ydiag_ref[:, rows] = ydiag_ref[:, rows] + y_off * exp_acum_x[:, rows]
        s_new = lax.dot_general(xd_b[:, rows], bg, TN_DIMS, preferred_element_type=F32)
        state_ref[rows, :] = cd_rows[rows, :] * state_ref[rows, :] + s_new

    y = ydiag_ref[...] + dskip_ref[...] * xs
    yg = y * _silu(z_ref[0])
    ms = jnp.mean(yg * yg, axis=-1, keepdims=True)
    y_ref[0] = (yg * lax.rsqrt(ms + EPS) * gn_ref[...]).astype(y_ref.dtype)

    @pl.when(c == nc - 1)
    def _final():
        hlast_ref[0] = state_ref[...]


def _ssd(proj3, offs, hist, h0, prm, Q):
    b, l, _ = proj3.shape
    H, P, G, N = prm["H"], prm["P"], prm["G"], prm["N"]
    W, GN, HL = H * P, G * N, prm["HL"]
    nchunk = l // Q
    hx, hb, hc = hist
    full = lambda shape: pl.BlockSpec(shape, lambda i, j: (0,) * len(shape))
    per_b = lambda shape: pl.BlockSpec(shape, lambda i, j: (i,) + (0,) * (len(shape) - 1))
    col = lambda width, off: pl.BlockSpec((1, Q, width), lambda i, j, o=off // width: (i, j, o))
    in_specs = [
        col(W, offs["z"]), col(W, offs["xs"]), col(GN, offs["bm"]), col(GN, offs["cm"]), col(HL, offs["dt"]),
        per_b((1, 3, W)), per_b((1, 3, GN)), per_b((1, 3, GN)), per_b((1, W, N)),
        full((4, W)), full((4, GN)), full((4, GN)), full((1, W)), full((1, GN)), full((1, GN)),
        full((1, HL)), full((1, HL)), full((1, W)), full((1, W)),
        full((Q, Q)), full((HL, W)), full((HL, H * Q)), full((W, HL)),
    ]
    out_specs = [
        pl.BlockSpec((1, Q, W), lambda i, j: (i, j, 0)),
        per_b((1, W, N)), per_b((1, 3, W)), per_b((1, 3, GN)), per_b((1, 3, GN)),
    ]
    out_shape = [
        jax.ShapeDtypeStruct((b, l, W), BF16),
        jax.ShapeDtypeStruct((b, W, N), F32),
        jax.ShapeDtypeStruct((b, 3, W), F32),
        jax.ShapeDtypeStruct((b, 3, GN), F32),
        jax.ShapeDtypeStruct((b, 3, GN), F32),
    ]
    scratch = [
        pltpu.VMEM((Q + 8, W), F32), pltpu.VMEM((Q + 8, GN), F32), pltpu.VMEM((Q + 8, GN), F32),
        pltpu.VMEM((W, N), F32), pltpu.VMEM((Q, W), F32),
    ]
    return pl.pallas_call(
        functools.partial(_ssd_kernel, Q=Q, H=H, P=P, G=G, N=N),
        grid=(b, nchunk),
        in_specs=in_specs, out_specs=out_specs, out_shape=out_shape, scratch_shapes=scratch,
        compiler_params=_params(("parallel", "arbitrary")),
        name="ssd_scan",
    )(proj3, proj3, proj3, proj3, proj3, hx, hb, hc, h0,
      prm["cwx"], prm["cwb"], prm["cwc"], prm["cbx"], prm["cbb"], prm["cbc"],
      prm["dtb"], prm["alog"], prm["dskip_x"], prm["ssd_gn"],
      jnp.asarray(np.tril(np.ones((Q, Q), np.float32)), BF16), prm["expand"],
      jnp.asarray(np.arange(HL)[:, None] == (np.arange(H * Q) // Q)[None, :], BF16), prm["expand_t"])


def _latent_kernel(ckv_ref, krr_ref, krot_ref, g_ref, cos_ref, sin_ref, c_ref, kr2_ref, kr_ref, *, rope):
    x = ckv_ref[0]
    ms = jnp.mean(x * x, axis=-1, keepdims=True)
    c_ref[0] = x * lax.rsqrt(ms + EPS) * g_ref[...]
    kr2 = krr_ref[0] * cos_ref[...] + krot_ref[0] * sin_ref[...]
    kr2_ref[0] = kr2
    kr_ref[0] = kr2[:, :rope]


def _latent(proj3, offs, prm, cos4, sin4, tm=256):
    b, l, _ = proj3.shape
    kvl, rope = prm["KVL"], prm["ROPE"]
    tm = _pick(l, tm, 8)
    col = lambda width, off: pl.BlockSpec((1, tm, width), lambda i, j, o=off // width: (i, j, o))
    return pl.pallas_call(
        functools.partial(_latent_kernel, rope=rope),
        grid=(b, l // tm),
        in_specs=[col(kvl, offs["ckv"]), col(LANES, offs["krr"]), col(LANES, offs["krot"]),
                  pl.BlockSpec((1, kvl), lambda i, j: (0, 0)),
                  pl.BlockSpec((tm, LANES), lambda i, j: (j, 0)), pl.BlockSpec((tm, LANES), lambda i, j: (j, 0))],
        out_specs=[pl.BlockSpec((1, tm, kvl), lambda i, j: (i, j, 0)),
                   pl.BlockSpec((1, tm, LANES), lambda i, j: (i, j, 0)),
                   pl.BlockSpec((1, tm, rope), lambda i, j: (i, j, 0))],
        out_shape=[jax.ShapeDtypeStruct((b, l, kvl), F32), jax.ShapeDtypeStruct((b, l, LANES), F32),
                   jax.ShapeDtypeStruct((b, l, rope), F32)],
        compiler_params=_params(("parallel", "parallel")),
        name="mla_latent",
    )(proj3, proj3, proj3, prm["kv_a_norm"], cos4, sin4)


def _q_kernel(cq_ref, g_ref, w_ref, cos_ref, sin_ref, gn_ref, gr_ref, q_ref, *, MH, qk_dim):
    x = cq_ref[0]
    ms = jnp.mean(x * x, axis=-1, keepdims=True)
    xn = (x * lax.rsqrt(ms + EPS) * g_ref[...]).astype(BF16)
    qf = jnp.dot(xn, w_ref[...], preferred_element_type=F32)
    cos = cos_ref[...]
    sin = sin_ref[...]
    lane = lax.broadcasted_iota(jnp.int32, (1, LANES), 1)
    half_mask = [(lane < LANES // 2).astype(F32), (lane >= LANES // 2).astype(F32)]
    rope0 = MH * MLA_NOPE
    rot0 = rope0 + MH * (LANES // 2)
    for p in range(MH // 2):
        rp = (qf[:, rope0 + p * LANES: rope0 + (p + 1) * LANES] * cos
              + qf[:, rot0 + p * LANES: rot0 + (p + 1) * LANES] * sin)
        for e in (0, 1):
            h = 2 * p + e
            nope = qf[:, h * MLA_NOPE:(h + 1) * MLA_NOPE]
            rh = rp * half_mask[e]
            ssq = jnp.sum(nope * nope, axis=-1, keepdims=True) + jnp.sum(rh * rh, axis=-1, keepdims=True)
            inv = lax.rsqrt(ssq * (1.0 / qk_dim) + EPS)
            q_ref[0, :, h * 2 * LANES: h * 2 * LANES + LANES] = (nope * inv * gn_ref[...]).astype(q_ref.dtype)
            q_ref[0, :, h * 2 * LANES + LANES:(h + 1) * 2 * LANES] = (rh * inv * gr_ref[e:e + 1, :]).astype(q_ref.dtype)


def _q_proj(proj3, offs, prm, cos4, sin4, tm=256):
    b, l, _ = proj3.shape
    ql, mh = prm["QL"], prm["MH"]
    tm = _pick(l, tm, 16)
    wq = prm["wq"]
    return pl.pallas_call(
        functools.partial(_q_kernel, MH=mh, qk_dim=prm["QK"]),
        grid=(b, l // tm),
        in_specs=[pl.BlockSpec((1, tm, ql), lambda i, j, o=offs["cq"] // ql: (i, j, o)),
                  pl.BlockSpec((1, ql), lambda i, j: (0, 0)),
                  pl.BlockSpec(wq.shape, lambda i, j: (0, 0)),
                  pl.BlockSpec((tm, LANES), lambda i, j: (j, 0)), pl.BlockSpec((tm, LANES), lambda i, j: (j, 0)),
                  pl.BlockSpec((1, LANES), lambda i, j: (0, 0)), pl.BlockSpec((2, LANES), lambda i, j: (0, 0))],
        out_specs=pl.BlockSpec((1, tm, mh * 2 * LANES), lambda i, j: (i, j, 0)),
        out_shape=jax.ShapeDtypeStruct((b, l, mh * 2 * LANES), BF16),
        compiler_params=_params(("parallel", "parallel")),
        name="mla_q",
    )(proj3, prm["q_a_norm"], wq, cos4, sin4, prm["q_gn"], prm["q_gr"])


def _kv_kernel(c_ref, kr2_ref, w_ref, gn_ref, gr_ref, k_ref, v_ref, *, MH, qk_dim):
    c = c_ref[0].astype(BF16)
    kv = jnp.dot(c, w_ref[...], preferred_element_type=F32)
    kr2 = kr2_ref[0]
    lane = lax.broadcasted_iota(jnp.int32, (1, LANES), 1)
    kr_lo = kr2 * (lane < LANES // 2).astype(F32)
    kr_ss = jnp.sum(kr_lo * kr_lo, axis=-1, keepdims=True)
    for h in range(MH):
        nope = kv[:, h * MLA_NOPE:(h + 1) * MLA_NOPE]
        ssq = jnp.sum(nope * nope, axis=-1, keepdims=True) + kr_ss
        inv = lax.rsqrt(ssq * (1.0 / qk_dim) + EPS)
        k_ref[0, :, h * 2 * LANES: h * 2 * LANES + LANES] = (nope * inv * gn_ref[...]).astype(k_ref.dtype)
        k_ref[0, :, h * 2 * LANES + LANES:(h + 1) * 2 * LANES] = (kr2 * inv * gr_ref[h % 2:h % 2 + 1, :]).astype(k_ref.dtype)
    v_ref[0] = kv[:, MH * MLA_NOPE:].astype(v_ref.dtype)


def _kv_proj(c_all, kr2_all, prm, tm=256):
    b, lk, kvl = c_all.shape
    mh = prm["MH"]
    tm = _pick(lk, tm, 16)
    wkv = prm["wkv"]
    return pl.pallas_call(
        functools.partial(_kv_kernel, MH=mh, qk_dim=prm["QK"]),
        grid=(b, lk // tm),
        in_specs=[pl.BlockSpec((1, tm, kvl), lambda i, j: (i, j, 0)),
                  pl.BlockSpec((1, tm, LANES), lambda i, j: (i, j, 0)),
                  pl.BlockSpec(wkv.shape, lambda i, j: (0, 0)),
                  pl.BlockSpec((1, LANES), lambda i, j: (0, 0)), pl.BlockSpec((2, LANES), lambda i, j: (0, 0))],
        out_specs=[pl.BlockSpec((1, tm, mh * 2 * LANES), lambda i, j: (i, j, 0)),
                   pl.BlockSpec((1, tm, mh * MLA_V_DIM), lambda i, j: (i, j, 0))],
        out_shape=[jax.ShapeDtypeStruct((b, lk, mh * 2 * LANES), BF16),
                   jax.ShapeDtypeStruct((b, lk, mh * MLA_V_DIM), BF16)],
        compiler_params=_params(("parallel", "parallel")),
        name="mla_kv",
    )(c_all, kr2_all, wkv, prm["k_gn"], prm["k_gr"])


def _attn_kernel(q_ref, k_ref, v_ref, o_ref, *, causal, tq, lq):
    for qi in range(lq // tq):
        rows = slice(qi * tq, (qi + 1) * tq)
        q = q_ref[0, rows, :]
        if causal:
            past = qi * tq
            sd = lax.dot_general(q, k_ref[0, past:past + tq, :], NT_DIMS, preferred_element_type=F32)
            ri = lax.broadcasted_iota(jnp.int32, (tq, tq), 0) // CHUNK
            ci = lax.broadcasted_iota(jnp.int32, (tq, tq), 1) // CHUNK
            sd = jnp.where(ri >= ci, sd, -jnp.inf)
            m = jnp.max(sd, axis=-1, keepdims=True)
            if past > 0:
                sp = lax.dot_general(q, k_ref[0, 0:past, :], NT_DIMS, preferred_element_type=F32)
                m = jnp.maximum(m, jnp.max(sp, axis=-1, keepdims=True))
                pp = jnp.exp(sp - m)
                l = jnp.sum(pp, axis=-1, keepdims=True)
                acc = jnp.dot(pp.astype(BF16), v_ref[0, 0:past, :], preferred_element_type=F32)
            pd = jnp.exp(sd - m)
            ld = jnp.sum(pd, axis=-1, keepdims=True)
            accd = jnp.dot(pd.astype(BF16), v_ref[0, past:past + tq, :], preferred_element_type=F32)
            if past > 0:
                l, acc = l + ld, acc + accd
            else:
                l, acc = ld, accd
        else:
            s = lax.dot_general(q, k_ref[0], NT_DIMS, preferred_element_type=F32)
            m = jnp.max(s, axis=-1, keepdims=True)
            p = jnp.exp(s - m)
            l = jnp.sum(p, axis=-1, keepdims=True)
            acc = jnp.dot(p.astype(BF16), v_ref[0], preferred_element_type=F32)
        o_ref[0, rows, :] = acc / l


def _attention(q, k, v, mh, causal, tq):
    b, lq, _ = q.shape
    lk = k.shape[1]
    return pl.pallas_call(
        functools.partial(_attn_kernel, causal=causal, tq=tq, lq=lq),
        grid=(b, mh),
        in_specs=[pl.BlockSpec((1, lq, 2 * LANES), lambda bi, h: (bi, 0, h)),
                  pl.BlockSpec((1, lk, 2 * LANES), lambda bi, h: (bi, 0, h)),
                  pl.BlockSpec((1, lk, MLA_V_DIM), lambda bi, h: (bi, 0, h))],
        out_specs=pl.BlockSpec((1, lq, MLA_V_DIM), lambda bi, h: (bi, 0, h)),
        out_shape=jax.ShapeDtypeStruct((b, lq, mh * MLA_V_DIM), F32),
        compiler_params=_params(("parallel", "parallel")),
        name="mla_attention",
    )(q, k, v)


def _oddeven_sort_pairs(lo, n):
    def merge(lo, n, r):
        step = r * 2
        if step < n:
            yield from merge(lo, n, step)
            yield from merge(lo + r, n, step)
            for i in range(lo + r, lo + n - r, step):
                yield (i, i + r)
        else:
            yield (lo, lo + r)
    if n > 1:
        m = n // 2
        yield from _oddeven_sort_pairs(lo, m)
        yield from _oddeven_sort_pairs(lo + m, m)
        yield from merge(lo, n, 1)


def _sort_desc(xs):
    xs = list(xs)
    for i, j in _oddeven_sort_pairs(0, len(xs)):
        xs[i], xs[j] = jnp.maximum(xs[i], xs[j]), jnp.minimum(xs[i], xs[j])
    return xs


def _bitonic_merge_desc(xs):
    xs = list(xs)
    n = len(xs)
    d = n // 2
    while d >= 1:
        for i in range(n):
            if i & d == 0:
                xs[i], xs[i + d] = jnp.maximum(xs[i], xs[i + d]), jnp.minimum(xs[i], xs[i + d])
        d //= 2
    return xs


def _merge_sublanes_top(xs, k):
    shift = SUBLANES // 2
    while shift >= 1:
        other = [pltpu.roll(x, shift, 0) for x in xs]
        if len(xs) < k:
            xs = _bitonic_merge_desc(xs + other[::-1])
        else:
            xs = _bitonic_merge_desc([jnp.maximum(xs[i], other[k - 1 - i]) for i in range(k)])
        shift //= 2
    return xs


def _top_sorted(x, k):
    groups = [x[i * SUBLANES:(i + 1) * SUBLANES, :] for i in range(x.shape[0] // SUBLANES)]
    return _merge_sublanes_top(_sort_desc(groups)[:k], k)


def _peer_select_kernel(pq_ref, keys_ref, s1_ref, s2_ref, e1_ref, e2_ref, tau_ref, *, PH, HALF, TOPK):
    tt = pq_ref.shape[1]
    sub = lax.broadcasted_iota(jnp.int32, (SUBLANES, tt), 0)

    def by_sublane(rows):
        out = rows[-1]
        for s in range(len(rows) - 2, -1, -1):
            out = jnp.where(sub == s, rows[s], out)
        return out

    for h in range(PH):
        q1 = pq_ref[h * 2 * HALF: h * 2 * HALF + HALF, :].astype(BF16)
        q2 = pq_ref[h * 2 * HALF + HALF:(h + 1) * 2 * HALF, :].astype(BF16)
        s1 = jnp.dot(keys_ref[h, 0], q1, preferred_element_type=F32)
        s2 = jnp.dot(keys_ref[h, 1], q2, preferred_element_type=F32)
        v1 = _top_sorted(s1, TOPK)
        v2 = _top_sorted(s2, TOPK)
        v2_lo, v2_hi = by_sublane(v2[:SUBLANES]), by_sublane(v2[SUBLANES:])
        cands = [v1[0] + v2_lo, v1[0] + v2_hi, v1[1] + v2_lo, v1[2] + v2_lo, v1[3] + v2_lo, v1[4] + v2_lo,
                 by_sublane(v1[SUBLANES:]) + v2[0],
                 jnp.where(sub < 6,
                           by_sublane([v1[5], v1[5], v1[6], v1[6], v1[7], v1[7], v1[7], v1[7]])
                           + jnp.where(sub % 2 == 0, v2[0], v2[1]), -jnp.inf)]
        top = _merge_sublanes_top(_sort_desc(cands), TOPK)
        tau = top[TOPK - 1]
        m = top[0]
        zsum = None
        for c in cands:
            z = jnp.where(c >= tau, jnp.exp(c - m), 0.0)
            zsum = z if zsum is None else zsum + z
        zsum = jnp.sum(zsum, axis=0, keepdims=True)
        s1_ref[h] = s1
        s2_ref[h] = s2
        e1_ref[h] = jnp.exp(s1 - v1[0][0:1, :])
        e2_ref[h] = jnp.exp(s2 - v2[0][0:1, :]) / zsum
        tau_ref[h] = tau[0:1, :]


def _peer_select(pq_t, keys_b, tt=256):
    t = pq_t.shape[1]
    ph, _, nk, half = keys_b.shape
    tt = _pick(t, tt, LANES)
    big = lambda: pl.BlockSpec((ph, nk, tt), lambda i: (0, 0, i))
    shp = jax.ShapeDtypeStruct((ph, nk, t), F32)
    return pl.pallas_call(
        functools.partial(_peer_select_kernel, PH=ph, HALF=half, TOPK=PEER_TOPK),
        grid=(t // tt,),
        in_specs=[pl.BlockSpec((ph * 2 * half, tt), lambda i: (0, i)),
                  pl.BlockSpec(keys_b.shape, lambda i: (0, 0, 0, 0))],
        out_specs=[big(), big(), big(), big(), pl.BlockSpec((ph, 1, tt), lambda i: (0, 0, i))],
        out_shape=[shp, shp, shp, shp, jax.ShapeDtypeStruct((ph, 1, t), F32)],
        compiler_params=_params(("parallel",)),
        name="peer_select",
    )(pq_t, keys_b)


def _gelu(x):
    return 0.5 * x * (1.0 + lax.erf(x * (1.0 / math.sqrt(2.0))))


ROWS_BF16 = 16
GATE_ROWS = 128


def _peer_main_kernel(hn_ref, u_ref, vt_ref, s1_ref, s2_ref, e1_ref, e2_ref, tau_ref, out_ref,
                      ata_ref, atb_ref, cta_ref, ctb_ref, *, PH, NK, NI, NJ):
    s = pl.program_id(0)
    tt = ata_ref.shape[1]

    @pl.when(s == 0)
    def _init():
        for ref in (ata_ref, atb_ref, cta_ref, ctb_ref):
            ref[...] = jnp.zeros(ref.shape, ref.dtype)

    @pl.when(lax.rem(jnp.maximum(s - 2, 0), NJ) == 0)
    def _zero_out():
        out_ref[...] = jnp.zeros(out_ref.shape, F32)

    def step(at_new, at_prev, ct_new, ct_prev):
        d = out_ref.shape[0]
        blk = lax.rem(jnp.maximum(s - 1, 0), NJ)

        def mix_rows(m0, m1):
            out_ref[m0:m1, :] += jnp.dot(vt_ref[m0:m1, :], ct_prev[...], preferred_element_type=F32)

        def score_block(m0, m1, n0, n1):
            at_new[m0:m1, n0:n1] = jnp.dot(u_ref[m0:m1, :], hn_ref[:, n0:n1], preferred_element_type=F32)

        def gate_tile(ii, tc, r0):
            row = blk * NI + ii
            cols = slice(tc * LANES, (tc + 1) * LANES)
            groups = [slice(r0 + k * ROWS_BF16, r0 + (k + 1) * ROWS_BF16) for k in range(GATE_ROWS // ROWS_BF16)]
            g = [None] * len(groups)
            for h in range(PH):
                s1r = s1_ref[h, pl.ds(row, 1), :][:, cols]
                e1r = e1_ref[h, pl.ds(row, 1), :][:, cols]
                tau = tau_ref[h, :, cols]
                for k, rws in enumerate(groups):
                    sel = (s1r + s2_ref[h, rws, cols]) >= tau
                    term = e1r * jnp.where(sel, e2_ref[h, rws, cols], 0.0)
                    g[k] = term if g[k] is None else g[k] + term
            for k, rws in enumerate(groups):
                orow = slice(ii * NK + rws.start, ii * NK + rws.stop)
                ct_new[orow, cols] = (_gelu(at_prev[orow, cols]) * g[k]).astype(BF16)

        tiles = [(ii, tc, r0) for ii in range(NI) for tc in range(tt // LANES) for r0 in range(0, NK, GATE_ROWS)]
        n_mix = 4
        n_sn = max(1, min(2, tt // (2 * LANES)))
        n_sm = 2
        ne = at_new.shape[0]
        mxu_items = [functools.partial(mix_rows, k * d // n_mix, (k + 1) * d // n_mix) for k in range(n_mix)]
        mxu_items += [functools.partial(score_block, km * ne // n_sm, (km + 1) * ne // n_sm,
                                        kn * tt // n_sn, (kn + 1) * tt // n_sn)
                      for kn in range(n_sn) for km in range(n_sm)]
        weights = [n_sn * n_sm] * n_mix + [n_mix] * (n_sn * n_sm)
        done, total = 0, sum(weights)
        for item, w in zip(mxu_items, weights):
            upto = len(tiles) * (done + w) // total
            for tile_args in tiles[len(tiles) * done // total: upto]:
                gate_tile(*tile_args)
            item()
            done += w

    @pl.when(s % 2 == 0)
    def _even():
        step(ata_ref, atb_ref, ctb_ref, cta_ref)

    @pl.when(s % 2 == 1)
    def _odd():
        step(atb_ref, ata_ref, cta_ref, ctb_ref)


def _peer_main(hn_t, u_b, vt_b, sel, tt=512, ne=512):
    d, t = hn_t.shape
    s1, s2, e1, e2, tau = sel
    ph, nk, _ = s1.shape
    nexp = u_b.shape[0]
    tt = _pick(t, tt, LANES)
    ne = _pick(nexp, ne, nk)
    ni = ne // nk
    nj = nexp // ne
    n_tiles = t // tt
    n_blocks = n_tiles * nj
    assert nj % 2 == 0
    once = dict(pipeline_mode=pl.Buffered(1))
    tile = lambda s, lag: jnp.minimum(jnp.maximum(s - lag, 0) // nj, n_tiles - 1)
    big = lambda: pl.BlockSpec((ph, nk, tt), lambda s: (0, 0, tile(s, 1)), **once)
    return pl.pallas_call(
        functools.partial(_peer_main_kernel, PH=ph, NK=nk, NI=ni, NJ=nj),
        grid=(n_blocks + 2,),
        in_specs=[pl.BlockSpec((d, tt), lambda s: (0, tile(s, 0)), **once),
                  pl.BlockSpec((ne, d), lambda s: (lax.rem(s, nj), 0)),
                  pl.BlockSpec((d, ne), lambda s: (0, lax.rem(jnp.maximum(s - 2, 0), nj))),
                  big(), big(), big(), big(),
                  pl.BlockSpec((ph, 1, tt), lambda s: (0, 0, tile(s, 1)), **once)],
        out_specs=pl.BlockSpec((d, tt), lambda s: (0, tile(s, 2))),
        out_shape=jax.ShapeDtypeStruct((d, t), F32),
        scratch_shapes=[pltpu.VMEM((ne, tt), F32), pltpu.VMEM((ne, tt), F32),
                        pltpu.VMEM((ne, tt), BF16), pltpu.VMEM((ne, tt), BF16)],
        compiler_params=_params(("arbitrary",)),
        name="peer_main",
    )(hn_t, u_b, vt_b, s1, s2, e1, e2, tau)


def _tadd_kernel(x_ref, yt_ref, o_ref):
    o_ref[...] = x_ref[...] + yt_ref[...].T


def _transpose_add(x2d, yt, tm=256):
    t, d = x2d.shape
    tm = _pick(t, tm, LANES)
    return pl.pallas_call(
        _tadd_kernel,
        grid=(t // tm,),
        in_specs=[pl.BlockSpec((tm, d), lambda i: (i, 0)), pl.BlockSpec((d, tm), lambda i: (0, i))],
        out_specs=pl.BlockSpec((tm, d), lambda i: (i, 0)),
        out_shape=jax.ShapeDtypeStruct((t, d), F32),
        compiler_params=_params(("parallel",)),
        name="peer_residual",
    )(x2d, yt)


def _prepare(norm_mix, w_in, conv_w, conv_b, dt_bias, a_log, d_skip, ssd_norm,
             q_a_norm, w_q_up, kv_a_norm, w_kv_up, q_norm, k_norm, attn_out_norm,
             w_out, norm_ffn, peer_w_q, peer_keys, peer_u, peer_v, rope, N):
    d = w_in.shape[0]
    W = ssd_norm.shape[0]
    H = dt_bias.shape[0]
    P = W // H
    xbc = conv_w.shape[1]
    GN = (xbc - W) // 2
    G = GN // N
    QL = q_a_norm.shape[0]
    KVL = kv_a_norm.shape[0]
    MW = attn_out_norm.shape[0]
    MH = MW // MLA_V_DIM
    QK = MLA_NOPE + rope
    HL = _round_up(H, LANES)
    half = rope // 2
    assert 2 * rope == LANES and MH % 2 == 0 and P == SSD_HEAD_DIM

    s = np.cumsum([0, W, xbc, H, QL, KVL, rope])
    wz, wxbc, wdt, wcq, wckv, wkr = (w_in[:, s[i]:s[i + 1]] for i in range(6))
    wrot = jnp.concatenate([-wkr[:, half:], wkr[:, :half]], axis=1)
    pieces = [wz, wxbc, wcq, wckv, wkr, wkr, wrot, wrot, wdt]
    offs, o = {}, 0
    for name, width in (("z", W), ("xs", W), ("bm", GN), ("cm", GN), ("cq", QL), ("ckv", KVL),
                        ("krr", LANES), ("krot", LANES), ("dt", HL)):
        assert o % width == 0, (name, o, width)
        offs[name] = o
        o += width
    NP = _round_up(o, 1024)
    w_in_p = jnp.concatenate(pieces + [jnp.zeros((d, NP - (o - HL + H)), w_in.dtype)], axis=1).astype(BF16)

    wq3 = w_q_up.reshape(QL, MH, QK)
    wq_nope = wq3[:, :, :MLA_NOPE].reshape(QL, MH * MLA_NOPE)
    wq_rope = wq3[:, :, MLA_NOPE:]
    wq_rot = jnp.concatenate([-wq_rope[:, :, half:], wq_rope[:, :, :half]], axis=2)
    wq = jnp.concatenate([wq_nope, wq_rope.reshape(QL, MH * rope), wq_rot.reshape(QL, MH * rope)], axis=1).astype(BF16)

    wkv3 = w_kv_up.reshape(KVL, MH, MLA_NOPE + MLA_V_DIM)
    wkv = jnp.concatenate([wkv3[:, :, :MLA_NOPE].reshape(KVL, MH * MLA_NOPE),
                           wkv3[:, :, MLA_NOPE:].reshape(KVL, MH * MLA_V_DIM)], axis=1).astype(BF16)

    scale = QK ** -0.5
    zeros_h = jnp.zeros((rope,), F32)

    def gains(g, sc):
        gn = (g[:MLA_NOPE] * sc).reshape(1, LANES)
        gr = jnp.stack([jnp.concatenate([g[MLA_NOPE:] * sc, zeros_h]), jnp.concatenate([zeros_h, g[MLA_NOPE:] * sc])])
        return gn.astype(F32), gr.astype(F32)

    q_gn, q_gr = gains(q_norm, scale)
    k_gn, k_gr = gains(k_norm, 1.0)

    pad_h = lambda v: jnp.concatenate([v.astype(F32), jnp.zeros((HL - H,), F32)]).reshape(1, HL)
    head_of_col = np.arange(W) // P
    expand = (np.arange(HL)[:, None] == head_of_col[None, :]).astype(np.float32)

    return dict(
        H=H, P=P, G=G, N=N, HL=HL, QL=QL, KVL=KVL, MH=MH, QK=QK, ROPE=rope, NP=NP, offs=offs,
        norm_mix=norm_mix, w_in=w_in_p,
        cwx=conv_w[:, :W], cwb=conv_w[:, W:W + GN], cwc=conv_w[:, W + GN:],
        cbx=conv_b[:W].reshape(1, W), cbb=conv_b[W:W + GN].reshape(1, GN), cbc=conv_b[W + GN:].reshape(1, GN),
        dtb=pad_h(dt_bias), alog=pad_h(a_log),
        dskip_x=jnp.repeat(d_skip.astype(F32), P).reshape(1, W), ssd_gn=ssd_norm.reshape(1, W).astype(F32),
        expand=jnp.asarray(expand, BF16), expand_t=jnp.asarray(expand.T, BF16),
        q_a_norm=q_a_norm.reshape(1, QL).astype(F32), kv_a_norm=kv_a_norm.reshape(1, KVL).astype(F32),
        wq=wq, wkv=wkv, q_gn=q_gn, q_gr=q_gr, k_gn=k_gn, k_gr=k_gr,
        attn_out_norm=attn_out_norm, w_out_a=w_out[:W].astype(BF16), w_out_b=w_out[W:].astype(BF16),
        norm_ffn=norm_ffn, peer_w_q_t=peer_w_q.T.astype(BF16), peer_keys=peer_keys.astype(BF16),
        peer_u=peer_u.astype(BF16), peer_vt=peer_v.T.astype(BF16),
    )


def _rope_tables(pos, rope):
    inv = 1.0 / (ROPE_BASE ** (jnp.arange(0, rope, 2, dtype=F32) / rope))
    ang = pos.astype(F32)[:, None] * inv[None, :]
    reps = LANES // (rope // 2)
    return jnp.tile(jnp.cos(ang), (1, reps)), jnp.tile(jnp.sin(ang), (1, reps))


def _layer(x, pos, past_ckv, past_krope, conv_hist, ssm_h0, prm):
    b, l, d = x.shape
    t = b * l
    H, P, G, N = prm["H"], prm["P"], prm["G"], prm["N"]
    W, GN = H * P, G * N
    offs = prm["offs"]
    x2d = x.reshape(t, d)

    h = _rms_cast(x2d, prm["norm_mix"])
    proj = _matmul([(h, prm["w_in"])], tm=1024, tn=512, name="in_proj")
    proj3 = proj.reshape(b, l, prm["NP"])

    hist = (conv_hist[:, :, :W], conv_hist[:, :, W:W + GN], conv_hist[:, :, W + GN:])
    y_ssd, h_last, nx, nb, ncm = _ssd(proj3, offs, hist, ssm_h0.reshape(b, W, N), prm, _pick(l, SSD_CHUNK, 8))
    conv_new = jnp.concatenate([nx, nb, ncm], axis=-1)
    h_last = h_last.reshape(b, H, P, N)

    cos4, sin4 = _rope_tables(pos, prm["ROPE"])
    c_new, kr2_new, kr_new = _latent(proj3, offs, prm, cos4, sin4)
    q = _q_proj(proj3, offs, prm, cos4, sin4)
    if past_ckv is None:
        c_all, kr2_all = c_new, kr2_new
    else:
        c_all = jnp.concatenate([past_ckv, c_new], axis=1)
        kr2_all = jnp.concatenate([jnp.concatenate([past_krope, past_krope], axis=-1), kr2_new], axis=1)
    lk = c_all.shape[1]
    k, v = _kv_proj(c_all, kr2_all, prm, tm=_pick(lk, 512, 16))
    if past_ckv is None:
        o = _attention(q, k, v, prm["MH"], True, _pick(l, 256, CHUNK))
    else:
        assert l <= CHUNK and (lk - l) % CHUNK == 0
        o = _attention(q, k, v, prm["MH"], False, l)
    o_n = _rms_cast(o.reshape(t, -1), prm["attn_out_norm"])

    x_mid = _matmul([(y_ssd.reshape(t, W), prm["w_out_a"]), (o_n, prm["w_out_b"])], res=x2d,
                    tm=512, tn=512, name="out_proj")

    hn_t = _rms_cast(x_mid, prm["norm_ffn"], transpose=True)
    pq_t = _matmul([(prm["peer_w_q_t"], hn_t)], tm=1024, tn=512, name="peer_query")
    sel = _peer_select(pq_t, prm["peer_keys"])
    yt = _peer_main(hn_t, prm["peer_u"], prm["peer_vt"], sel)
    out = _transpose_add(x_mid, yt)
    return out.reshape(b, l, d), c_new, kr_new, h_last.astype(ssm_h0.dtype), conv_new


def kernel(x_prompt, x_sample, cache_mla_ckv, cache_mla_krope, state_ssm, state_conv, norm_mix, w_in, conv_w, conv_b, dt_bias, a_log, d_skip, ssd_norm, q_a_norm, w_q_up, kv_a_norm, w_kv_up, q_norm, k_norm, attn_out_norm, w_out, norm_ffn, peer_w_q, peer_keys, peer_u, peer_v):
    bp, lp, _ = x_prompt.shape
    ls = x_sample.shape[1]
    depth = norm_mix.shape[0]
    past = cache_mla_ckv.shape[2]
    rope = cache_mla_krope.shape[-1]
    nstate = state_ssm.shape[-1]
    xbc = state_conv.shape[-1]
    pos_p = jnp.arange(lp, dtype=jnp.int32)
    pos_s = past + jnp.arange(ls, dtype=jnp.int32)
    yp, ys = x_prompt, x_sample
    outs_p, outs_s = [], []
    for layer in range(depth):
        prm = _prepare(norm_mix[layer], w_in[layer], conv_w[layer], conv_b[layer], dt_bias[layer], a_log[layer],
                       d_skip[layer], ssd_norm[layer], q_a_norm[layer], w_q_up[layer], kv_a_norm[layer],
                       w_kv_up[layer], q_norm[layer], k_norm[layer], attn_out_norm[layer], w_out[layer],
                       norm_ffn[layer], peer_w_q[layer], peer_keys[layer], peer_u[layer], peer_v[layer],
                       rope, nstate)
        zero_conv = jnp.zeros((bp, state_conv.shape[2], xbc), x_prompt.dtype)
        zero_ssm = jnp.zeros((bp,) + state_ssm.shape[2:], state_ssm.dtype)
        yp, *rest_p = _layer(yp, pos_p, None, None, zero_conv, zero_ssm, prm)
        outs_p.append(rest_p)
        ys, *rest_s = _layer(ys, pos_s, cache_mla_ckv[layer], cache_mla_krope[layer],
                             state_conv[layer], state_ssm[layer], prm)
        outs_s.append(rest_s)
    stack = lambda outs, i: jnp.stack([o[i] for o in outs])
    return (yp, ys,
            stack(outs_p, 0), stack(outs_p, 1), stack(outs_p, 2), stack(outs_p, 3),
            stack(outs_s, 0), stack(outs_s, 1), stack(outs_s, 2), stack(outs_s, 3))
```

```python
import functools
import math

import numpy as np
import jax
import jax.numpy as jnp
from jax import lax
from jax.experimental import pallas as pl
from jax.experimental.pallas import tpu as pltpu

F32 = jnp.float32
BF16 = jnp.bfloat16

EPS = 1e-6
CHUNK = 64
SSD_CHUNK = 128
SSD_HEAD_DIM = 64
MLA_NOPE = 128
MLA_V_DIM = 128
ROPE_BASE = 10000.0
PEER_TOPK = 16
LANES = 128
SUBLANES = 8
VMEM_LIMIT = 56 * 1024 * 1024

NT_DIMS = (((1,), (1,)), ((), ()))
TN_DIMS = (((0,), (0,)), ((), ()))


def _pick(n, pref, mult):
    if n <= pref:
        return n
    t = (pref // mult) * mult
    while t >= mult:
        if n % t == 0:
            return t
        t -= mult
    return n


def _round_up(n, m):
    return (n + m - 1) // m * m


def _params(sem, vmem=VMEM_LIMIT):
    return pltpu.CompilerParams(dimension_semantics=sem, vmem_limit_bytes=vmem)


def _split3(x):
    hi = x.astype(BF16)
    r1 = x - hi.astype(F32)
    mid = r1.astype(BF16)
    lo = (r1 - mid.astype(F32)).astype(BF16)
    return hi, mid, lo


def _dot3(a01, x, dims=None):
    out = None
    for piece in _split3(x):
        if dims is None:
            d = jnp.dot(a01, piece, preferred_element_type=F32)
        else:
            d = lax.dot_general(a01, piece, dims, preferred_element_type=F32)
        out = d if out is None else out + d
    return out


def _dot3_right(x, b01, dims=None):
    out = None
    for piece in _split3(x):
        if dims is None:
            d = jnp.dot(piece, b01, preferred_element_type=F32)
        else:
            d = lax.dot_general(piece, b01, dims, preferred_element_type=F32)
        out = d if out is None else out + d
    return out


def _rms_cast_kernel(x_ref, g_ref, o_ref):
    x = x_ref[...].astype(F32)
    ms = jnp.mean(x * x, axis=-1, keepdims=True)
    o_ref[...] = (x * lax.rsqrt(ms + EPS) * g_ref[...]).astype(o_ref.dtype)


def _rms_cast_t_kernel(x_ref, g_ref, o_ref):
    x = x_ref[...].astype(F32)
    ms = jnp.mean(x * x, axis=-1, keepdims=True)
    o_ref[...] = (x * lax.rsqrt(ms + EPS) * g_ref[...]).T.astype(o_ref.dtype)


def _rms_cast(x2d, g, out_dtype=BF16, tm=256, transpose=False):
    t, d = x2d.shape
    tm = _pick(t, tm, LANES if transpose else 16)
    if transpose:
        body, out_spec, out_shape = _rms_cast_t_kernel, pl.BlockSpec((d, tm), lambda i: (0, i)), (d, t)
    else:
        body, out_spec, out_shape = _rms_cast_kernel, pl.BlockSpec((tm, d), lambda i: (i, 0)), (t, d)
    return pl.pallas_call(
        body,
        grid=(t // tm,),
        in_specs=[pl.BlockSpec((tm, d), lambda i: (i, 0)), pl.BlockSpec((1, d), lambda i: (0, 0))],
        out_specs=out_spec,
        out_shape=jax.ShapeDtypeStruct(out_shape, out_dtype),
        compiler_params=_params(("parallel",)),
        name="rms_cast_t" if transpose else "rms_cast",
    )(x2d, g.reshape(1, d).astype(F32))


def _mm_kernel(*refs, n_pairs, has_res):
    o_ref = refs[-1]
    acc = None
    for p in range(n_pairs):
        d = jnp.dot(refs[2 * p][...], refs[2 * p + 1][...], preferred_element_type=F32)
        acc = d if acc is None else acc + d
    if has_res:
        acc = refs[2 * n_pairs][...] + acc
    o_ref[...] = acc.astype(o_ref.dtype)


def _matmul(pairs, res=None, out_dtype=F32, tm=512, tn=512, name="matmul"):
    m = pairs[0][0].shape[0]
    n = pairs[0][1].shape[1]
    tm = _pick(m, tm, 16)
    tn = _pick(n, tn, LANES)
    in_specs, args = [], []
    for a, w in pairs:
        k = a.shape[1]
        in_specs += [pl.BlockSpec((tm, k), lambda i, j: (i, 0)), pl.BlockSpec((k, tn), lambda i, j: (0, j))]
        args += [a, w]
    if res is not None:
        in_specs.append(pl.BlockSpec((tm, tn), lambda i, j: (i, j)))
        args.append(res)
    return pl.pallas_call(
        functools.partial(_mm_kernel, n_pairs=len(pairs), has_res=res is not None),
        grid=(m // tm, n // tn),
        in_specs=in_specs,
        out_specs=pl.BlockSpec((tm, tn), lambda i, j: (i, j)),
        out_shape=jax.ShapeDtypeStruct((m, n), out_dtype),
        compiler_params=_params(("parallel", "arbitrary")),
        name=name,
    )(*args)


def _silu(x):
    return (0.5 * x) * (1.0 + jnp.tanh(0.5 * x))


def _softplus(x):
    return jnp.maximum(x, 0.0) + jnp.log1p(jnp.exp(-jnp.abs(x)))


def _ssd_kernel(z_ref, xs_ref, bm_ref, cm_ref, dt_ref,
                hx_ref, hb_ref, hc_ref, h0_ref,
                cwx_ref, cwb_ref, cwc_ref, cbx_ref, cbb_ref, cbc_ref,
                dtb_ref, alog_ref, dskip_ref, gn_ref,
                tri_ref, exp_ref, expl_ref, expt_ref,
                y_ref, hlast_ref, nx_ref, nb_ref, nc_ref,
                bx_ref, bb_ref, bc_ref, state_ref, ydiag_ref,
                *, Q, H, P, G, N):
    c = pl.program_id(1)
    nc = pl.num_programs(1)
    R = H // G
    W = H * P

    @pl.when(c == 0)
    def _init():
        for buf, hist in ((bx_ref, hx_ref), (bb_ref, hb_ref), (bc_ref, hc_ref)):
            buf[0:8, :] = jnp.zeros((8, buf.shape[1]), F32)
            buf[5:8, :] = hist[0]
        state_ref[...] = h0_ref[0]

    def conv(buf, x_ref, cw_ref, cb_ref):
        x = x_ref[0]
        buf[8:8 + Q, :] = x
        acc = buf[5:5 + Q, :] * cw_ref[0:1, :]
        acc = acc + buf[6:6 + Q, :] * cw_ref[1:2, :]
        acc = acc + buf[7:7 + Q, :] * cw_ref[2:3, :]
        acc = acc + x * cw_ref[3:4, :]
        buf[0:8, :] = buf[Q:Q + 8, :]
        return _silu(cb_ref[...] + acc)

    xs = conv(bx_ref, xs_ref, cwx_ref, cbx_ref)
    bm = conv(bb_ref, bm_ref, cwb_ref, cbb_ref)
    cm = conv(bc_ref, cm_ref, cwc_ref, cbc_ref)

    @pl.when(c == nc - 1)
    def _tail():
        nx_ref[0] = bx_ref[5:8, :]
        nb_ref[0] = bb_ref[5:8, :]
        nc_ref[0] = bc_ref[5:8, :]

    dt = _softplus(dt_ref[0] + dtb_ref[...])
    a_neg = -jnp.exp(alog_ref[...])
    dta = dt * a_neg
    tri = tri_ref[...]
    acum = _dot3(tri, dta)
    eye = (lax.broadcasted_iota(jnp.int32, (Q, Q), 0) == lax.broadcasted_iota(jnp.int32, (Q, Q), 1)).astype(BF16)
    acum_t = _dot3_right(acum, eye, TN_DIMS)
    dt_t = _dot3_right(dt, eye, TN_DIMS)

    acum_x = _dot3_right(acum, exp_ref[...])
    dt_x = _dot3_right(dt, exp_ref[...])
    acum_l = _dot3_right(acum, expl_ref[...])

    ii = lax.broadcasted_iota(jnp.int32, (Q, Q), 0)
    jj = lax.broadcasted_iota(jnp.int32, (Q, Q), 1)
    causal = ii >= jj
    xs_b = xs.astype(BF16)
    bm_b = bm.astype(BF16)
    cm_b = cm.astype(BF16)
    st_b = state_ref[...].astype(BF16)

    exp_acum_x = jnp.exp(acum_x)
    decay_x = jnp.exp(acum_x[Q - 1:Q, :] - acum_x) * dt_x
    xd_b = (xs * decay_x).astype(BF16)

    last_t = jnp.broadcast_to(acum_t[:, Q - 1:Q], (acum_t.shape[0], N))
    cd_rows = jnp.exp(_dot3(expt_ref[...], last_t))

    for g in range(G):
        cg = cm_b[:, g * N:(g + 1) * N]
        bg = bm_b[:, g * N:(g + 1) * N]
        cb = lax.dot_general(cg, bg, NT_DIMS, preferred_element_type=F32)
        for r in range(R):
            h = g * R + r
            dm = acum_l[:, h * Q:(h + 1) * Q] - acum_t[h:h + 1, :]
            lm = jnp.where(causal, jnp.exp(dm), 0.0)
            wm = (cb * lm * dt_t[h:h + 1, :]).astype(BF16)
            ydiag_ref[:, h * P:(h + 1) * P] = jnp.dot(wm, xs_b[:, h * P:(h + 1) * P],
                                                      preferred_element_type=F32)
        rows = slice(g * R * P, (g + 1) * R * P)
        y_off = lax.dot_general(cg, st_b[rows, :], NT_DIMS, preferred_element_type=F32)
        ydiag_ref[:, rows] = ydiag_ref[:, rows] + y_off * exp_acum_x[:, rows]
        s_new = lax.dot_general(xd_b[:, rows], bg, TN_DIMS, preferred_element_type=F32)
        state_ref[rows, :] = cd_rows[rows, :] * state_ref[rows, :] + s_new

    y = ydiag_ref[...] + dskip_ref[...] * xs
    yg = y * _silu(z_ref[0])
    ms = jnp.mean(yg * yg, axis=-1, keepdims=True)
    y_ref[0] = (yg * lax.rsqrt(ms + EPS) * gn_ref[...]).astype(y_ref.dtype)

    @pl.when(c == nc - 1)
    def _final():
        hlast_ref[0] = state_ref[...]


def _ssd(proj3, offs, hist, h0, prm, Q):
    b, l, _ = proj3.shape
    H, P, G, N = prm["H"], prm["P"], prm["G"], prm["N"]
    W, GN, HL = H * P, G * N, prm["HL"]
    nchunk = l // Q
    hx, hb, hc = hist
    full = lambda shape: pl.BlockSpec(shape, lambda i, j: (0,) * len(shape))
    per_b = lambda shape: pl.BlockSpec(shape, lambda i, j: (i,) + (0,) * (len(shape) - 1))
    col = lambda width, off: pl.BlockSpec((1, Q, width), lambda i, j, o=off // width: (i, j, o))
    in_specs = [
        col(W, offs["z"]), col(W, offs["xs"]), col(GN, offs["bm"]), col(GN, offs["cm"]), col(HL, offs["dt"]),
        per_b((1, 3, W)), per_b((1, 3, GN)), per_b((1, 3, GN)), per_b((1, W, N)),
        full((4, W)), full((4, GN)), full((4, GN)), full((1, W)), full((1, GN)), full((1, GN)),
        full((1, HL)), full((1, HL)), full((1, W)), full((1, W)),
        full((Q, Q)), full((HL, W)), full((HL, H * Q)), full((W, HL)),
    ]
    out_specs = [
        pl.BlockSpec((1, Q, W), lambda i, j: (i, j, 0)),
        per_b((1, W, N)), per_b((1, 3, W)), per_b((1, 3, GN)), per_b((1, 3, GN)),
    ]
    out_shape = [
        jax.ShapeDtypeStruct((b, l, W), BF16),
        jax.ShapeDtypeStruct((b, W, N), F32),
        jax.ShapeDtypeStruct((b, 3, W), F32),
        jax.ShapeDtypeStruct((b, 3, GN), F32),
        jax.ShapeDtypeStruct((b, 3, GN), F32),
    ]
    scratch = [
        pltpu.VMEM((Q + 8, W), F32), pltpu.VMEM((Q + 8, GN), F32), pltpu.VMEM((Q + 8, GN), F32),
        pltpu.VMEM((W, N), F32), pltpu.VMEM((Q, W), F32),
    ]
    return pl.pallas_call(
        functools.partial(_ssd_kernel, Q=Q, H=H, P=P, G=G, N=N),
        grid=(b, nchunk),
        in_specs=in_specs, out_specs=out_specs, out_shape=out_shape, scratch_shapes=scratch,
        compiler_params=_params(("parallel", "arbitrary")),
        name="ssd_scan",
    )(proj3, proj3, proj3, proj3, proj3, hx, hb, hc, h0,
      prm["cwx"], prm["cwb"], prm["cwc"], prm["cbx"], prm["cbb"], prm["cbc"],
      prm["dtb"], prm["alog"], prm["dskip_x"], prm["ssd_gn"],
      jnp.asarray(np.tril(np.ones((Q, Q), np.float32)), BF16), prm["expand"],
      jnp.asarray(np.arange(HL)[:, None] == (np.arange(H * Q) // Q)[None, :], BF16), prm["expand_t"])


def _latent_kernel(ckv_ref, krr_ref, krot_ref, g_ref, cos_ref, sin_ref, c_ref, kr2_ref, kr_ref, *, rope):
    x = ckv_ref[0]
    ms = jnp.mean(x * x, axis=-1, keepdims=True)
    c_ref[0] = x * lax.rsqrt(ms + EPS) * g_ref[...]
    kr2 = krr_ref[0] * cos_ref[...] + krot_ref[0] * sin_ref[...]
    kr2_ref[0] = kr2
    kr_ref[0] = kr2[:, :rope]


def _latent(proj3, offs, prm, cos4, sin4, tm=256):
    b, l, _ = proj3.shape
    kvl, rope = prm["KVL"], prm["ROPE"]
    tm = _pick(l, tm, 8)
    col = lambda width, off: pl.BlockSpec((1, tm, width), lambda i, j, o=off // width: (i, j, o))
    return pl.pallas_call(
        functools.partial(_latent_kernel, rope=rope),
        grid=(b, l // tm),
        in_specs=[col(kvl, offs["ckv"]), col(LANES, offs["krr"]), col(LANES, offs["krot"]),
                  pl.BlockSpec((1, kvl), lambda i, j: (0, 0)),
                  pl.BlockSpec((tm, LANES), lambda i, j: (j, 0)), pl.BlockSpec((tm, LANES), lambda i, j: (j, 0))],
        out_specs=[pl.BlockSpec((1, tm, kvl), lambda i, j: (i, j, 0)),
                   pl.BlockSpec((1, tm, LANES), lambda i, j: (i, j, 0)),
                   pl.BlockSpec((1, tm, rope), lambda i, j: (i, j, 0))],
        out_shape=[jax.ShapeDtypeStruct((b, l, kvl), F32), jax.ShapeDtypeStruct((b, l, LANES), F32),
                   jax.ShapeDtypeStruct((b, l, rope), F32)],
        compiler_params=_params(("parallel", "parallel")),
        name="mla_latent",
    )(proj3, proj3, proj3, prm["kv_a_norm"], cos4, sin4)


def _q_kernel(cq_ref, g_ref, w_ref, cos_ref, sin_ref, gn_ref, gr_ref, q_ref, *, MH, qk_dim):
    x = cq_ref[0]
    ms = jnp.mean(x * x, axis=-1, keepdims=True)
    xn = (x * lax.rsqrt(ms + EPS) * g_ref[...]).astype(BF16)
    qf = jnp.dot(xn, w_ref[...], preferred_element_type=F32)
    cos = cos_ref[...]
    sin = sin_ref[...]
    lane = lax.broadcasted_iota(jnp.int32, (1, LANES), 1)
    half_mask = [(lane < LANES // 2).astype(F32), (lane >= LANES // 2).astype(F32)]
    rope0 = MH * MLA_NOPE
    rot0 = rope0 + MH * (LANES // 2)
    for p in range(MH // 2):
        rp = (qf[:, rope0 + p * LANES: rope0 + (p + 1) * LANES] * cos
              + qf[:, rot0 + p * LANES: rot0 + (p + 1) * LANES] * sin)
        for e in (0, 1):
            h = 2 * p + e
            nope = qf[:, h * MLA_NOPE:(h + 1) * MLA_NOPE]
            rh = rp * half_mask[e]
            ssq = jnp.sum(nope * nope, axis=-1, keepdims=True) + jnp.sum(rh * rh, axis=-1, keepdims=True)
            inv = lax.rsqrt(ssq * (1.0 / qk_dim) + EPS)
            q_ref[0, :, h * 2 * LANES: h * 2 * LANES + LANES] = (nope * inv * gn_ref[...]).astype(q_ref.dtype)
            q_ref[0, :, h * 2 * LANES + LANES:(h + 1) * 2 * LANES] = (rh * inv * gr_ref[e:e + 1, :]).astype(q_ref.dtype)


def _q_proj(proj3, offs, prm, cos4, sin4, tm=256):
    b, l, _ = proj3.shape
    ql, mh = prm["QL"], prm["MH"]
    tm = _pick(l, tm, 16)
    wq = prm["wq"]
    return pl.pallas_call(
        functools.partial(_q_kernel, MH=mh, qk_dim=prm["QK"]),
        grid=(b, l // tm),
        in_specs=[pl.BlockSpec((1, tm, ql), lambda i, j, o=offs["cq"] // ql: (i, j, o)),
                  pl.BlockSpec((1, ql), lambda i, j: (0, 0)),
                  pl.BlockSpec(wq.shape, lambda i, j: (0, 0)),
                  pl.BlockSpec((tm, LANES), lambda i, j: (j, 0)), pl.BlockSpec((tm, LANES), lambda i, j: (j, 0)),
                  pl.BlockSpec((1, LANES), lambda i, j: (0, 0)), pl.BlockSpec((2, LANES), lambda i, j: (0, 0))],
        out_specs=pl.BlockSpec((1, tm, mh * 2 * LANES), lambda i, j: (i, j, 0)),
        out_shape=jax.ShapeDtypeStruct((b, l, mh * 2 * LANES), BF16),
        compiler_params=_params(("parallel", "parallel")),
        name="mla_q",
    )(proj3, prm["q_a_norm"], wq, cos4, sin4, prm["q_gn"], prm["q_gr"])


def _kv_kernel(c_ref, kr2_ref, w_ref, gn_ref, gr_ref, k_ref, v_ref, *, MH, qk_dim):
    c = c_ref[0].astype(BF16)
    kv = jnp.dot(c, w_ref[...], preferred_element_type=F32)
    kr2 = kr2_ref[0]
    lane = lax.broadcasted_iota(jnp.int32, (1, LANES), 1)
    kr_lo = kr2 * (lane < LANES // 2).astype(F32)
    kr_ss = jnp.sum(kr_lo * kr_lo, axis=-1, keepdims=True)
    for h in range(MH):
        nope = kv[:, h * MLA_NOPE:(h + 1) * MLA_NOPE]
        ssq = jnp.sum(nope * nope, axis=-1, keepdims=True) + kr_ss
        inv = lax.rsqrt(ssq * (1.0 / qk_dim) + EPS)
        k_ref[0, :, h * 2 * LANES: h * 2 * LANES + LANES] = (nope * inv * gn_ref[...]).astype(k_ref.dtype)
        k_ref[0, :, h * 2 * LANES + LANES:(h + 1) * 2 * LANES] = (kr2 * inv * gr_ref[h % 2:h % 2 + 1, :]).astype(k_ref.dtype)
    v_ref[0] = kv[:, MH * MLA_NOPE:].astype(v_ref.dtype)


def _kv_proj(c_all, kr2_all, prm, tm=256):
    b, lk, kvl = c_all.shape
    mh = prm["MH"]
    tm = _pick(lk, tm, 16)
    wkv = prm["wkv"]
    return pl.pallas_call(
        functools.partial(_kv_kernel, MH=mh, qk_dim=prm["QK"]),
        grid=(b, lk // tm),
        in_specs=[pl.BlockSpec((1, tm, kvl), lambda i, j: (i, j, 0)),
                  pl.BlockSpec((1, tm, LANES), lambda i, j: (i, j, 0)),
                  pl.BlockSpec(wkv.shape, lambda i, j: (0, 0)),
                  pl.BlockSpec((1, LANES), lambda i, j: (0, 0)), pl.BlockSpec((2, LANES), lambda i, j: (0, 0))],
        out_specs=[pl.BlockSpec((1, tm, mh * 2 * LANES), lambda i, j: (i, j, 0)),
                   pl.BlockSpec((1, tm, mh * MLA_V_DIM), lambda i, j: (i, j, 0))],
        out_shape=[jax.ShapeDtypeStruct((b, lk, mh * 2 * LANES), BF16),
                   jax.ShapeDtypeStruct((b, lk, mh * MLA_V_DIM), BF16)],
        compiler_params=_params(("parallel", "parallel")),
        name="mla_kv",
    )(c_all, kr2_all, wkv, prm["k_gn"], prm["k_gr"])


def _attn_kernel(q_ref, k_ref, v_ref, o_ref, *, causal, tq, lq):
    for qi in range(lq // tq):
        rows = slice(qi * tq, (qi + 1) * tq)
        q = q_ref[0, rows, :]
        if causal:
            past = qi * tq
            sd = lax.dot_general(q, k_ref[0, past:past + tq, :], NT_DIMS, preferred_element_type=F32)
            ri = lax.broadcasted_iota(jnp.int32, (tq, tq), 0) // CHUNK
            ci = lax.broadcasted_iota(jnp.int32, (tq, tq), 1) // CHUNK
            sd = jnp.where(ri >= ci, sd, -jnp.inf)
            m = jnp.max(sd, axis=-1, keepdims=True)
            if past > 0:
                sp = lax.dot_general(q, k_ref[0, 0:past, :], NT_DIMS, preferred_element_type=F32)
                m = jnp.maximum(m, jnp.max(sp, axis=-1, keepdims=True))
                pp = jnp.exp(sp - m)
                l = jnp.sum(pp, axis=-1, keepdims=True)
                acc = jnp.dot(pp.astype(BF16), v_ref[0, 0:past, :], preferred_element_type=F32)
            pd = jnp.exp(sd - m)
            ld = jnp.sum(pd, axis=-1, keepdims=True)
            accd = jnp.dot(pd.astype(BF16), v_ref[0, past:past + tq, :], preferred_element_type=F32)
            if past > 0:
                l, acc = l + ld, acc + accd
            else:
                l, acc = ld, accd
        else:
            s = lax.dot_general(q, k_ref[0], NT_DIMS, preferred_element_type=F32)
            m = jnp.max(s, axis=-1, keepdims=True)
            p = jnp.exp(s - m)
            l = jnp.sum(p, axis=-1, keepdims=True)
            acc = jnp.dot(p.astype(BF16), v_ref[0], preferred_element_type=F32)
        o_ref[0, rows, :] = acc / l


def _attention(q, k, v, mh, causal, tq):
    b, lq, _ = q.shape
    lk = k.shape[1]
    return pl.pallas_call(
        functools.partial(_attn_kernel, causal=causal, tq=tq, lq=lq),
        grid=(b, mh),
        in_specs=[pl.BlockSpec((1, lq, 2 * LANES), lambda bi, h: (bi, 0, h)),
                  pl.BlockSpec((1, lk, 2 * LANES), lambda bi, h: (bi, 0, h)),
                  pl.BlockSpec((1, lk, MLA_V_DIM), lambda bi, h: (bi, 0, h))],
        out_specs=pl.BlockSpec((1, lq, MLA_V_DIM), lambda bi, h: (bi, 0, h)),
        out_shape=jax.ShapeDtypeStruct((b, lq, mh * MLA_V_DIM), F32),
        compiler_params=_params(("parallel", "parallel")),
        name="mla_attention",
    )(q, k, v)


def _oddeven_sort_pairs(lo, n):
    def merge(lo, n, r):
        step = r * 2
        if step < n:
            yield from merge(lo, n, step)
            yield from merge(lo + r, n, step)
            for i in range(lo + r, lo + n - r, step):
                yield (i, i + r)
        else:
            yield (lo, lo + r)
    if n > 1:
        m = n // 2
        yield from _oddeven_sort_pairs(lo, m)
        yield from _oddeven_sort_pairs(lo + m, m)
        yield from merge(lo, n, 1)


def _sort_desc(xs):
    xs = list(xs)
    for i, j in _oddeven_sort_pairs(0, len(xs)):
        xs[i], xs[j] = jnp.maximum(xs[i], xs[j]), jnp.minimum(xs[i], xs[j])
    return xs


def _bitonic_merge_desc(xs):
    xs = list(xs)
    n = len(xs)
    d = n // 2
    while d >= 1:
        for i in range(n):
            if i & d == 0:
                xs[i], xs[i + d] = jnp.maximum(xs[i], xs[i + d]), jnp.minimum(xs[i], xs[i + d])
        d //= 2
    return xs


def _merge_sublanes_top(xs, k):
    shift = SUBLANES // 2
    while shift >= 1:
        other = [pltpu.roll(x, shift, 0) for x in xs]
        if len(xs) < k:
            xs = _bitonic_merge_desc(xs + other[::-1])
        else:
            xs = _bitonic_merge_desc([jnp.maximum(xs[i], other[k - 1 - i]) for i in range(k)])
        shift //= 2
    return xs


def _top_sorted(x, k):
    groups = [x[i * SUBLANES:(i + 1) * SUBLANES, :] for i in range(x.shape[0] // SUBLANES)]
    return _merge_sublanes_top(_sort_desc(groups)[:k], k)


def _peer_select_kernel(pq_ref, keys_ref, c1_ref, e1_ref, r2_ref, e2_ref, *, PH, HALF, TOPK):
    assert TOPK == 2 * SUBLANES
    tt = pq_ref.shape[1]
    sub = lax.broadcasted_iota(jnp.int32, (SUBLANES, tt), 0)

    def by_sublane(rows):
        out = rows[-1]
        for s in range(len(rows) - 2, -1, -1):
            out = jnp.where(sub == s, rows[s], out)
        return out

    for h in range(PH):
        q1 = pq_ref[h * 2 * HALF: h * 2 * HALF + HALF, :].astype(BF16)
        q2 = pq_ref[h * 2 * HALF + HALF:(h + 1) * 2 * HALF, :].astype(BF16)
        s1 = jnp.dot(keys_ref[h, 0], q1, preferred_element_type=F32)
        s2 = jnp.dot(keys_ref[h, 1], q2, preferred_element_type=F32)
        v1 = _top_sorted(s1, TOPK)
        v2 = _top_sorted(s2, TOPK)
        v2_lo, v2_hi = by_sublane(v2[:SUBLANES]), by_sublane(v2[SUBLANES:])
        cands = [v1[0] + v2_lo, v1[0] + v2_hi, v1[1] + v2_lo, v1[2] + v2_lo, v1[3] + v2_lo, v1[4] + v2_lo,
                 by_sublane(v1[SUBLANES:]) + v2[0],
                 jnp.where(sub < 6,
                           by_sublane([v1[5], v1[5], v1[6], v1[6], v1[7], v1[7], v1[7], v1[7]])
                           + jnp.where(sub % 2 == 0, v2[0], v2[1]), -jnp.inf)]
        top = _merge_sublanes_top(_sort_desc(cands), TOPK)
        tau = top[TOPK - 1]
        m = top[0]
        zsum = None
        for c in cands:
            z = jnp.where(c >= tau, jnp.exp(c - m), 0.0)
            zsum = z if zsum is None else zsum + z
        zsum = jnp.sum(zsum, axis=0, keepdims=True)
        inv_z = 1.0 / zsum
        for g16 in range(s1.shape[0] // ROWS_BF16):
            ranks, gates = [], []
            for g8 in (2 * g16, 2 * g16 + 1):
                rows = slice(g8 * SUBLANES, (g8 + 1) * SUBLANES)
                a, b2 = s1[rows, :], s2[rows, :]
                count1 = jnp.zeros_like(a)
                rank2 = jnp.zeros_like(b2)
                for vb in v2:
                    count1 = count1 + jnp.where(a + vb >= tau, 1.0, 0.0)
                    rank2 = rank2 + jnp.where(vb > b2, 1.0, 0.0)
                c1_ref[h, rows, :] = count1
                e1_ref[h, rows, :] = jnp.exp(a - v1[0])
                ranks.append(rank2)
                gates.append(jnp.exp(b2 - v2[0]) * inv_z)
            rows16 = slice(g16 * ROWS_BF16, (g16 + 1) * ROWS_BF16)
            r2_ref[h, rows16, :] = jnp.concatenate(ranks, axis=0).astype(BF16)
            e2_ref[h, rows16, :] = jnp.concatenate(gates, axis=0).astype(BF16)


def _peer_select(pq_t, keys_b, tt=256):
    t = pq_t.shape[1]
    ph, _, nk, half = keys_b.shape
    tt = _pick(t, tt, LANES)
    big = lambda: pl.BlockSpec((ph, nk, tt), lambda i: (0, 0, i))
    shp = lambda dt: jax.ShapeDtypeStruct((ph, nk, t), dt)
    return pl.pallas_call(
        functools.partial(_peer_select_kernel, PH=ph, HALF=half, TOPK=PEER_TOPK),
        grid=(t // tt,),
        in_specs=[pl.BlockSpec((ph * 2 * half, tt), lambda i: (0, i)),
                  pl.BlockSpec(keys_b.shape, lambda i: (0, 0, 0, 0))],
        out_specs=[big(), big(), big(), big()],
        out_shape=[shp(F32), shp(F32), shp(BF16), shp(BF16)],
        compiler_params=_params(("parallel",)),
        name="peer_select",
    )(pq_t, keys_b)


def _gelu(x):
    return 0.5 * x * (1.0 + lax.erf(x * (1.0 / math.sqrt(2.0))))


ROWS_BF16 = 16
GATE_ROWS = 128


def _peer_main_kernel(hn_ref, u_ref, vt_ref, c1_ref, e1_ref, r2_ref, e2_ref, out_ref,
                      ata_ref, atb_ref, cta_ref, ctb_ref, *, PH, NK, NI, NJ):
    s = pl.program_id(0)
    tt = ata_ref.shape[1]

    @pl.when(s == 0)
    def _init():
        for ref in (ata_ref, atb_ref, cta_ref, ctb_ref):
            ref[...] = jnp.zeros(ref.shape, ref.dtype)

    @pl.when(lax.rem(jnp.maximum(s - 2, 0), NJ) == 0)
    def _zero_out():
        out_ref[...] = jnp.zeros(out_ref.shape, F32)

    def step(at_new, at_prev, ct_new, ct_prev):
        d = out_ref.shape[0]
        blk = lax.rem(jnp.maximum(s - 1, 0), NJ)

        def mix_rows(m0, m1):
            out_ref[m0:m1, :] += jnp.dot(vt_ref[m0:m1, :], ct_prev[...], preferred_element_type=F32)

        def score_block(m0, m1, n0, n1):
            at_new[m0:m1, n0:n1] = jnp.dot(u_ref[m0:m1, :], hn_ref[:, n0:n1], preferred_element_type=F32)

        def gate_tile(ii, tc, r0):
            row = blk * NI + ii
            cols = slice(tc * LANES, (tc + 1) * LANES)
            groups = [slice(r0 + k * ROWS_BF16, r0 + (k + 1) * ROWS_BF16) for k in range(GATE_ROWS // ROWS_BF16)]
            g = [None] * len(groups)
            zero = jnp.zeros((ROWS_BF16, LANES), BF16)
            for h in range(PH):
                bcast = lambda r: jnp.broadcast_to(r[:, cols], (ROWS_BF16, LANES)).astype(BF16)
                c1b = bcast(c1_ref[h, pl.ds(row, 1), :])
                e1b = bcast(e1_ref[h, pl.ds(row, 1), :])
                for k, rws in enumerate(groups):
                    term = e1b * jnp.where(r2_ref[h, rws, cols] < c1b, e2_ref[h, rws, cols], zero)
                    g[k] = term if g[k] is None else g[k] + term
            for k, rws in enumerate(groups):
                orow = slice(ii * NK + rws.start, ii * NK + rws.stop)
                ct_new[orow, cols] = _gelu(at_prev[orow, cols]).astype(BF16) * g[k]

        tiles = [(ii, tc, r0) for ii in range(NI) for tc in range(tt // LANES) for r0 in range(0, NK, GATE_ROWS)]
        n_mix = 4
        n_sn = max(1, min(2, tt // (2 * LANES)))
        n_sm = 1
        ne = at_new.shape[0]
        mxu_items = [functools.partial(mix_rows, k * d // n_mix, (k + 1) * d // n_mix) for k in range(n_mix)]
        mxu_items += [functools.partial(score_block, km * ne // n_sm, (km + 1) * ne // n_sm,
                                        kn * tt // n_sn, (kn + 1) * tt // n_sn)
                      for kn in range(n_sn) for km in range(n_sm)]
        weights = [n_sn * n_sm] * n_mix + [n_mix] * (n_sn * n_sm)
        done, total = 0, sum(weights)
        for item, w in zip(mxu_items, weights):
            upto = len(tiles) * (done + w) // total
            for tile_args in tiles[len(tiles) * done // total: upto]:
                gate_tile(*tile_args)
            item()
            done += w

    @pl.when(s % 2 == 0)
    def _even():
        step(ata_ref, atb_ref, ctb_ref, cta_ref)

    @pl.when(s % 2 == 1)
    def _odd():
        step(atb_ref, ata_ref, cta_ref, ctb_ref)


def _peer_main(hn_t, u_b, vt_b, sel, tt=512, ne=512):
    d, t = hn_t.shape
    c1, e1, r2, e2 = sel
    ph, nk, _ = c1.shape
    nexp = u_b.shape[0]
    tt = _pick(t, tt, LANES)
    ne = _pick(nexp, ne, nk)
    ni = ne // nk
    nj = nexp // ne
    n_tiles = t // tt
    n_blocks = n_tiles * nj
    assert nj % 2 == 0
    once = dict(pipeline_mode=pl.Buffered(1))
    tile = lambda s, lag: jnp.minimum(jnp.maximum(s - lag, 0) // nj, n_tiles - 1)
    big = lambda: pl.BlockSpec((ph, nk, tt), lambda s: (0, 0, tile(s, 1)), **once)
    return pl.pallas_call(
        functools.partial(_peer_main_kernel, PH=ph, NK=nk, NI=ni, NJ=nj),
        grid=(n_blocks + 2,),
        in_specs=[pl.BlockSpec((d, tt), lambda s: (0, tile(s, 0)), **once),
                  pl.BlockSpec((ne, d), lambda s: (lax.rem(s, nj), 0)),
                  pl.BlockSpec((d, ne), lambda s: (0, lax.rem(jnp.maximum(s - 2, 0), nj))),
                  big(), big(), big(), big()],
        out_specs=pl.BlockSpec((d, tt), lambda s: (0, tile(s, 2))),
        out_shape=jax.ShapeDtypeStruct((d, t), F32),
        scratch_shapes=[pltpu.VMEM((ne, tt), F32), pltpu.VMEM((ne, tt), F32),
                        pltpu.VMEM((ne, tt), BF16), pltpu.VMEM((ne, tt), BF16)],
        compiler_params=_params(("arbitrary",)),
        name="peer_main",
    )(hn_t, u_b, vt_b, c1, e1, r2, e2)


def _tadd_kernel(x_ref, yt_ref, o_ref):
    o_ref[...] = x_ref[...] + yt_ref[...].T


def _transpose_add(x2d, yt, tm=256):
    t, d = x2d.shape
    tm = _pick(t, tm, LANES)
    return pl.pallas_call(
        _tadd_kernel,
        grid=(t // tm,),
        in_specs=[pl.BlockSpec((tm, d), lambda i: (i, 0)), pl.BlockSpec((d, tm), lambda i: (0, i))],
        out_specs=pl.BlockSpec((tm, d), lambda i: (i, 0)),
        out_shape=jax.ShapeDtypeStruct((t, d), F32),
        compiler_params=_params(("parallel",)),
        name="peer_residual",
    )(x2d, yt)


def _prepare(norm_mix, w_in, conv_w, conv_b, dt_bias, a_log, d_skip, ssd_norm,
             q_a_norm, w_q_up, kv_a_norm, w_kv_up, q_norm, k_norm, attn_out_norm,
             w_out, norm_ffn, peer_w_q, peer_keys, peer_u, peer_v, rope, N):
    d = w_in.shape[0]
    W = ssd_norm.shape[0]
    H = dt_bias.shape[0]
    P = W // H
    xbc = conv_w.shape[1]
    GN = (xbc - W) // 2
    G = GN // N
    QL = q_a_norm.shape[0]
    KVL = kv_a_norm.shape[0]
    MW = attn_out_norm.shape[0]
    MH = MW // MLA_V_DIM
    QK = MLA_NOPE + rope
    HL = _round_up(H, LANES)
    half = rope // 2
    assert 2 * rope == LANES and MH % 2 == 0 and P == SSD_HEAD_DIM

    s = np.cumsum([0, W, xbc, H, QL, KVL, rope])
    wz, wxbc, wdt, wcq, wckv, wkr = (w_in[:, s[i]:s[i + 1]] for i in range(6))
    wrot = jnp.concatenate([-wkr[:, half:], wkr[:, :half]], axis=1)
    pieces = [wz, wxbc, wcq, wckv, wkr, wkr, wrot, wrot, wdt]
    offs, o = {}, 0
    for name, width in (("z", W), ("xs", W), ("bm", GN), ("cm", GN), ("cq", QL), ("ckv", KVL),
                        ("krr", LANES), ("krot", LANES), ("dt", HL)):
        assert o % width == 0, (name, o, width)
        offs[name] = o
        o += width
    NP = _round_up(o, 1024)
    w_in_p = jnp.concatenate(pieces + [jnp.zeros((d, NP - (o - HL + H)), w_in.dtype)], axis=1).astype(BF16)

    wq3 = w_q_up.reshape(QL, MH, QK)
    wq_nope = wq3[:, :, :MLA_NOPE].reshape(QL, MH * MLA_NOPE)
    wq_rope = wq3[:, :, MLA_NOPE:]
    wq_rot = jnp.concatenate([-wq_rope[:, :, half:], wq_rope[:, :, :half]], axis=2)
    wq = jnp.concatenate([wq_nope, wq_rope.reshape(QL, MH * rope), wq_rot.reshape(QL, MH * rope)], axis=1).astype(BF16)

    wkv3 = w_kv_up.reshape(KVL, MH, MLA_NOPE + MLA_V_DIM)
    wkv = jnp.concatenate([wkv3[:, :, :MLA_NOPE].reshape(KVL, MH * MLA_NOPE),
                           wkv3[:, :, MLA_NOPE:].reshape(KVL, MH * MLA_V_DIM)], axis=1).astype(BF16)

    scale = QK ** -0.5
    zeros_h = jnp.zeros((rope,), F32)

    def gains(g, sc):
        gn = (g[:MLA_NOPE] * sc).reshape(1, LANES)
        gr = jnp.stack([jnp.concatenate([g[MLA_NOPE:] * sc, zeros_h]), jnp.concatenate([zeros_h, g[MLA_NOPE:] * sc])])
        return gn.astype(F32), gr.astype(F32)

    q_gn, q_gr = gains(q_norm, scale)
    k_gn, k_gr = gains(k_norm, 1.0)

    pad_h = lambda v: jnp.concatenate([v.astype(F32), jnp.zeros((HL - H,), F32)]).reshape(1, HL)
    head_of_col = np.arange(W) // P
    expand = (np.arange(HL)[:, None] == head_of_col[None, :]).astype(np.float32)

    return dict(
        H=H, P=P, G=G, N=N, HL=HL, QL=QL, KVL=KVL, MH=MH, QK=QK, ROPE=rope, NP=NP, offs=offs,
        norm_mix=norm_mix, w_in=w_in_p,
        cwx=conv_w[:, :W], cwb=conv_w[:, W:W + GN], cwc=conv_w[:, W + GN:],
        cbx=conv_b[:W].reshape(1, W), cbb=conv_b[W:W + GN].reshape(1, GN), cbc=conv_b[W + GN:].reshape(1, GN),
        dtb=pad_h(dt_bias), alog=pad_h(a_log),
        dskip_x=jnp.repeat(d_skip.astype(F32), P).reshape(1, W), ssd_gn=ssd_norm.reshape(1, W).astype(F32),
        expand=jnp.asarray(expand, BF16), expand_t=jnp.asarray(expand.T, BF16),
        q_a_norm=q_a_norm.reshape(1, QL).astype(F32), kv_a_norm=kv_a_norm.reshape(1, KVL).astype(F32),
        wq=wq, wkv=wkv, q_gn=q_gn, q_gr=q_gr, k_gn=k_gn, k_gr=k_gr,
        attn_out_norm=attn_out_norm, w_out_a=w_out[:W].astype(BF16), w_out_b=w_out[W:].astype(BF16),
        norm_ffn=norm_ffn, peer_w_q_t=peer_w_q.T.astype(BF16), peer_keys=peer_keys.astype(BF16),
        peer_u=peer_u.astype(BF16), peer_vt=peer_v.T.astype(BF16),
    )


def _rope_tables(pos, rope):
    inv = 1.0 / (ROPE_BASE ** (jnp.arange(0, rope, 2, dtype=F32) / rope))
    ang = pos.astype(F32)[:, None] * inv[None, :]
    reps = LANES // (rope // 2)
    return jnp.tile(jnp.cos(ang), (1, reps)), jnp.tile(jnp.sin(ang), (1, reps))


def _layer(x, pos, past_ckv, past_krope, conv_hist, ssm_h0, prm):
    b, l, d = x.shape
    t = b * l
    H, P, G, N = prm["H"], prm["P"], prm["G"], prm["N"]
    W, GN = H * P, G * N
    offs = prm["offs"]
    x2d = x.reshape(t, d)

    h = _rms_cast(x2d, prm["norm_mix"])
    proj = _matmul([(h, prm["w_in"])], tm=1024, tn=512, name="in_proj")
    proj3 = proj.reshape(b, l, prm["NP"])

    hist = (conv_hist[:, :, :W], conv_hist[:, :, W:W + GN], conv_hist[:, :, W + GN:])
    y_ssd, h_last, nx, nb, ncm = _ssd(proj3, offs, hist, ssm_h0.reshape(b, W, N), prm, _pick(l, SSD_CHUNK, 8))
    conv_new = jnp.concatenate([nx, nb, ncm], axis=-1)
    h_last = h_last.reshape(b, H, P, N)

    cos4, sin4 = _rope_tables(pos, prm["ROPE"])
    c_new, kr2_new, kr_new = _latent(proj3, offs, prm, cos4, sin4)
    q = _q_proj(proj3, offs, prm, cos4, sin4)
    if past_ckv is None:
        c_all, kr2_all = c_new, kr2_new
    else:
        c_all = jnp.concatenate([past_ckv, c_new], axis=1)
        kr2_all = jnp.concatenate([jnp.concatenate([past_krope, past_krope], axis=-1), kr2_new], axis=1)
    lk = c_all.shape[1]
    k, v = _kv_proj(c_all, kr2_all, prm, tm=_pick(lk, 512, 16))
    if past_ckv is None:
        o = _attention(q, k, v, prm["MH"], True, _pick(l, 256, CHUNK))
    else:
        assert l <= CHUNK and (lk - l) % CHUNK == 0
        o = _attention(q, k, v, prm["MH"], False, l)
    o_n = _rms_cast(o.reshape(t, -1), prm["attn_out_norm"])

    x_mid = _matmul([(y_ssd.reshape(t, W), prm["w_out_a"]), (o_n, prm["w_out_b"])], res=x2d,
                    tm=512, tn=512, name="out_proj")

    hn_t = _rms_cast(x_mid, prm["norm_ffn"], transpose=True)
    pq_t = _matmul([(prm["peer_w_q_t"], hn_t)], tm=1024, tn=512, name="peer_query")
    sel = _peer_select(pq_t, prm["peer_keys"])
    yt = _peer_main(hn_t, prm["peer_u"], prm["peer_vt"], sel)
    out = _transpose_add(x_mid, yt)
    return out.reshape(b, l, d), c_new, kr_new, h_last.astype(ssm_h0.dtype), conv_new


def kernel(x_prompt, x_sample, cache_mla_ckv, cache_mla_krope, state_ssm, state_conv, norm_mix, w_in, conv_w, conv_b, dt_bias, a_log, d_skip, ssd_norm, q_a_norm, w_q_up, kv_a_norm, w_kv_up, q_norm, k_norm, attn_out_norm, w_out, norm_ffn, peer_w_q, peer_keys, peer_u, peer_v):
    bp, lp, _ = x_prompt.shape
    ls = x_sample.shape[1]
    depth = norm_mix.shape[0]
    past = cache_mla_ckv.shape[2]
    rope = cache_mla_krope.shape[-1]
    nstate = state_ssm.shape[-1]
    xbc = state_conv.shape[-1]
    pos_p = jnp.arange(lp, dtype=jnp.int32)
    pos_s = past + jnp.arange(ls, dtype=jnp.int32)
    yp, ys = x_prompt, x_sample
    outs_p, outs_s = [], []
    for layer in range(depth):
        prm = _prepare(norm_mix[layer], w_in[layer], conv_w[layer], conv_b[layer], dt_bias[layer], a_log[layer],
                       d_skip[layer], ssd_norm[layer], q_a_norm[layer], w_q_up[layer], kv_a_norm[layer],
                       w_kv_up[layer], q_norm[layer], k_norm[layer], attn_out_norm[layer], w_out[layer],
                       norm_ffn[layer], peer_w_q[layer], peer_keys[layer], peer_u[layer], peer_v[layer],
                       rope, nstate)
        zero_conv = jnp.zeros((bp, state_conv.shape[2], xbc), x_prompt.dtype)
        zero_ssm = jnp.zeros((bp,) + state_ssm.shape[2:], state_ssm.dtype)
        yp, *rest_p = _layer(yp, pos_p, None, None, zero_conv, zero_ssm, prm)
        outs_p.append(rest_p)
        ys, *rest_s = _layer(ys, pos_s, cache_mla_ckv[layer], cache_mla_krope[layer],
                             state_conv[layer], state_ssm[layer], prm)
        outs_s.append(rest_s)
    stack = lambda outs, i: jnp.stack([o[i] for o in outs])
    return (yp, ys,
            stack(outs_p, 0), stack(outs_p, 1), stack(outs_p, 2), stack(outs_p, 3),
            stack(outs_s, 0), stack(outs_s, 1), stack(outs_s, 2), stack(outs_s, 3))
```

```python
import functools
import math

import numpy as np
import jax
import jax.numpy as jnp
from jax import lax
from jax.experimental import pallas as pl
from jax.experimental.pallas import tpu as pltpu

F32 = jnp.float32
BF16 = jnp.bfloat16

EPS = 1e-6
CHUNK = 64
SSD_CHUNK = 128
SSD_HEAD_DIM = 64
MLA_NOPE = 128
MLA_V_DIM = 128
ROPE_BASE = 10000.0
PEER_TOPK = 16
LANES = 128
SUBLANES = 8
VMEM_LIMIT = 56 * 1024 * 1024

NT_DIMS = (((1,), (1,)), ((), ()))
TN_DIMS = (((0,), (0,)), ((), ()))


def _pick(n, pref, mult):
    if n <= pref:
        return n
    t = (pref // mult) * mult
    while t >= mult:
        if n % t == 0:
            return t
        t -= mult
    return n


def _round_up(n, m):
    return (n + m - 1) // m * m


def _params(sem, vmem=VMEM_LIMIT):
    return pltpu.CompilerParams(dimension_semantics=sem, vmem_limit_bytes=vmem)


def _split3(x):
    hi = x.astype(BF16)
    r1 = x - hi.astype(F32)
    mid = r1.astype(BF16)
    lo = (r1 - mid.astype(F32)).astype(BF16)
    return hi, mid, lo


def _dot3(a01, x, dims=None):
    out = None
    for piece in _split3(x):
        if dims is None:
            d = jnp.dot(a01, piece, preferred_element_type=F32)
        else:
            d = lax.dot_general(a01, piece, dims, preferred_element_type=F32)
        out = d if out is None else out + d
    return out


def _dot3_right(x, b01, dims=None):
    out = None
    for piece in _split3(x):
        if dims is None:
            d = jnp.dot(piece, b01, preferred_element_type=F32)
        else:
            d = lax.dot_general(piece, b01, dims, preferred_element_type=F32)
        out = d if out is None else out + d
    return out


def _rms_cast_kernel(x_ref, g_ref, o_ref):
    x = x_ref[...].astype(F32)
    ms = jnp.mean(x * x, axis=-1, keepdims=True)
    o_ref[...] = (x * lax.rsqrt(ms + EPS) * g_ref[...]).astype(o_ref.dtype)


def _rms_cast_t_kernel(x_ref, g_ref, o_ref):
    x = x_ref[...].astype(F32)
    ms = jnp.mean(x * x, axis=-1, keepdims=True)
    o_ref[...] = (x * lax.rsqrt(ms + EPS) * g_ref[...]).T.astype(o_ref.dtype)


def _rms_cast(x2d, g, out_dtype=BF16, tm=256, transpose=False):
    t, d = x2d.shape
    tm = _pick(t, tm, LANES if transpose else 16)
    if transpose:
        body, out_spec, out_shape = _rms_cast_t_kernel, pl.BlockSpec((d, tm), lambda i: (0, i)), (d, t)
    else:
        body, out_spec, out_shape = _rms_cast_kernel, pl.BlockSpec((tm, d), lambda i: (i, 0)), (t, d)
    return pl.pallas_call(
        body,
        grid=(t // tm,),
        in_specs=[pl.BlockSpec((tm, d), lambda i: (i, 0)), pl.BlockSpec((1, d), lambda i: (0, 0))],
        out_specs=out_spec,
        out_shape=jax.ShapeDtypeStruct(out_shape, out_dtype),
        compiler_params=_params(("parallel",)),
        name="rms_cast_t" if transpose else "rms_cast",
    )(x2d, g.reshape(1, d).astype(F32))


def _mm_kernel(*refs, n_pairs, has_res):
    o_ref = refs[-1]
    acc = None
    for p in range(n_pairs):
        d = jnp.dot(refs[2 * p][...], refs[2 * p + 1][...], preferred_element_type=F32)
        acc = d if acc is None else acc + d
    if has_res:
        acc = refs[2 * n_pairs][...] + acc
    o_ref[...] = acc.astype(o_ref.dtype)


def _matmul(pairs, res=None, out_dtype=F32, tm=512, tn=512, name="matmul"):
    m = pairs[0][0].shape[0]
    n = pairs[0][1].shape[1]
    tm = _pick(m, tm, 16)
    tn = _pick(n, tn, LANES)
    in_specs, args = [], []
    for a, w in pairs:
        k = a.shape[1]
        in_specs += [pl.BlockSpec((tm, k), lambda i, j: (i, 0)), pl.BlockSpec((k, tn), lambda i, j: (0, j))]
        args += [a, w]
    if res is not None:
        in_specs.append(pl.BlockSpec((tm, tn), lambda i, j: (i, j)))
        args.append(res)
    return pl.pallas_call(
        functools.partial(_mm_kernel, n_pairs=len(pairs), has_res=res is not None),
        grid=(m // tm, n // tn),
        in_specs=in_specs,
        out_specs=pl.BlockSpec((tm, tn), lambda i, j: (i, j)),
        out_shape=jax.ShapeDtypeStruct((m, n), out_dtype),
        compiler_params=_params(("parallel", "arbitrary")),
        name=name,
    )(*args)


def _silu(x):
    return (0.5 * x) * (1.0 + jnp.tanh(0.5 * x))


def _softplus(x):
    return jnp.maximum(x, 0.0) + jnp.log1p(jnp.exp(-jnp.abs(x)))


def _ssd_kernel(z_ref, xs_ref, bm_ref, cm_ref, dt_ref,
                hx_ref, hb_ref, hc_ref, h0_ref,
                cwx_ref, cwb_ref, cwc_ref, cbx_ref, cbb_ref, cbc_ref,
                dtb_ref, alog_ref, dskip_ref, gn_ref,
                tri_ref, exp_ref, expl_ref, expt_ref,
                y_ref, hlast_ref, nx_ref, nb_ref, nc_ref,
                bx_ref, bb_ref, bc_ref, state_ref, ydiag_ref,
                *, Q, H, P, G, N):
    c = pl.program_id(1)
    nc = pl.num_programs(1)
    R = H // G
    W = H * P

    @pl.when(c == 0)
    def _init():
        for buf, hist in ((bx_ref, hx_ref), (bb_ref, hb_ref), (bc_ref, hc_ref)):
            buf[0:8, :] = jnp.zeros((8, buf.shape[1]), F32)
            buf[5:8, :] = hist[0]
        state_ref[...] = h0_ref[0]

    def conv(buf, x_ref, cw_ref, cb_ref):
        x = x_ref[0]
        buf[8:8 + Q, :] = x
        acc = buf[5:5 + Q, :] * cw_ref[0:1, :]
        acc = acc + buf[6:6 + Q, :] * cw_ref[1:2, :]
        acc = acc + buf[7:7 + Q, :] * cw_ref[2:3, :]
        acc = acc + x * cw_ref[3:4, :]
        buf[0:8, :] = buf[Q:Q + 8, :]
        return _silu(cb_ref[...] + acc)

    xs = conv(bx_ref, xs_ref, cwx_ref, cbx_ref)
    bm = conv(bb_ref, bm_ref, cwb_ref, cbb_ref)
    cm = conv(bc_ref, cm_ref, cwc_ref, cbc_ref)

    @pl.when(c == nc - 1)
    def _tail():
        nx_ref[0] = bx_ref[5:8, :]
        nb_ref[0] = bb_ref[5:8, :]
        nc_ref[0] = bc_ref[5:8, :]

    dt = _softplus(dt_ref[0] + dtb_ref[...])
    a_neg = -jnp.exp(alog_ref[...])
    dta = dt * a_neg
    tri = tri_ref[...]
    acum = _dot3(tri, dta)
    eye = (lax.broadcasted_iota(jnp.int32, (Q, Q), 0) == lax.broadcasted_iota(jnp.int32, (Q, Q), 1)).astype(BF16)
    acum_t = _dot3_right(acum, eye, TN_DIMS)
    dt_t = _dot3_right(dt, eye, TN_DIMS)

    acum_x = _dot3_right(acum, exp_ref[...])
    dt_x = _dot3_right(dt, exp_ref[...])
    acum_l = _dot3_right(acum, expl_ref[...])

    ii = lax.broadcasted_iota(jnp.int32, (Q, Q), 0)
    jj = lax.broadcasted_iota(jnp.int32, (Q, Q), 1)
    causal = ii >= jj
    xs_b = xs.astype(BF16)
    bm_b = bm.astype(BF16)
    cm_b = cm.astype(BF16)
    st_b = state_ref[...].astype(BF16)

    exp_acum_x = jnp.exp(acum_x)
    decay_x = jnp.exp(acum_x[Q - 1:Q, :] - acum_x) * dt_x
    xd_b = (xs * decay_x).astype(BF16)

    last_t = jnp.broadcast_to(acum_t[:, Q - 1:Q], (acum_t.shape[0], N))
    cd_rows = jnp.exp(_dot3(expt_ref[...], last_t))

    for g in range(G):
        cg = cm_b[:, g * N:(g + 1) * N]
        bg = bm_b[:, g * N:(g + 1) * N]
        cb = lax.dot_general(cg, bg, NT_DIMS, preferred_element_type=F32)
        for r in range(R):
            h = g * R + r
            dm = acum_l[:, h * Q:(h + 1) * Q] - acum_t[h:h + 1, :]
            lm = jnp.where(causal, jnp.exp(dm), 0.0)
            wm = (cb * lm * dt_t[h:h + 1, :]).astype(BF16)
            ydiag_ref[:, h * P:(h + 1) * P] = jnp.dot(wm, xs_b[:, h * P:(h + 1) * P],
                                                      preferred_element_type=F32)
        rows = slice(g * R * P, (g + 1) * R * P)
        y_off = lax.dot_general(cg, st_b[rows, :], NT_DIMS, preferred_element_type=F32)
        ydiag_ref[:, rows] = ydiag_ref[:, rows] + y_off * exp_acum_x[:, rows]
        s_new = lax.dot_general(xd_b[:, rows], bg, TN_DIMS, preferred_element_type=F32)
        state_ref[rows, :] = cd_rows[rows, :] * state_ref[rows, :] + s_new

    y = ydiag_ref[...] + dskip_ref[...] * xs
    yg = y * _silu(z_ref[0])
    ms = jnp.mean(yg * yg, axis=-1, keepdims=True)
    y_ref[0] = (yg * lax.rsqrt(ms + EPS) * gn_ref[...]).astype(y_ref.dtype)

    @pl.when(c == nc - 1)
    def _final():
        hlast_ref[0] = state_ref[...]


def _ssd(proj3, offs, hist, h0, prm, Q):
    b, l, _ = proj3.shape
    H, P, G, N = prm["H"], prm["P"], prm["G"], prm["N"]
    W, GN, HL = H * P, G * N, prm["HL"]
    nchunk = l // Q
    hx, hb, hc = hist
    full = lambda shape: pl.BlockSpec(shape, lambda i, j: (0,) * len(shape))
    per_b = lambda shape: pl.BlockSpec(shape, lambda i, j: (i,) + (0,) * (len(shape) - 1))
    col = lambda width, off: pl.BlockSpec((1, Q, width), lambda i, j, o=off // width: (i, j, o))
    in_specs = [
        col(W, offs["z"]), col(W, offs["xs"]), col(GN, offs["bm"]), col(GN, offs["cm"]), col(HL, offs["dt"]),
        per_b((1, 3, W)), per_b((1, 3, GN)), per_b((1, 3, GN)), per_b((1, W, N)),
        full((4, W)), full((4, GN)), full((4, GN)), full((1, W)), full((1, GN)), full((1, GN)),
        full((1, HL)), full((1, HL)), full((1, W)), full((1, W)),
        full((Q, Q)), full((HL, W)), full((HL, H * Q)), full((W, HL)),
    ]
    out_specs = [
        pl.BlockSpec((1, Q, W), lambda i, j: (i, j, 0)),
        per_b((1, W, N)), per_b((1, 3, W)), per_b((1, 3, GN)), per_b((1, 3, GN)),
    ]
    out_shape = [
        jax.ShapeDtypeStruct((b, l, W), BF16),
        jax.ShapeDtypeStruct((b, W, N), F32),
        jax.ShapeDtypeStruct((b, 3, W), F32),
        jax.ShapeDtypeStruct((b, 3, GN), F32),
        jax.ShapeDtypeStruct((b, 3, GN), F32),
    ]
    scratch = [
        pltpu.VMEM((Q + 8, W), F32), pltpu.VMEM((Q + 8, GN), F32), pltpu.VMEM((Q + 8, GN), F32),
        pltpu.VMEM((W, N), F32), pltpu.VMEM((Q, W), F32),
    ]
    return pl.pallas_call(
        functools.partial(_ssd_kernel, Q=Q, H=H, P=P, G=G, N=N),
        grid=(b, nchunk),
        in_specs=in_specs, out_specs=out_specs, out_shape=out_shape, scratch_shapes=scratch,
        compiler_params=_params(("parallel", "arbitrary")),
        name="ssd_scan",
    )(proj3, proj3, proj3, proj3, proj3, hx, hb, hc, h0,
      prm["cwx"], prm["cwb"], prm["cwc"], prm["cbx"], prm["cbb"], prm["cbc"],
      prm["dtb"], prm["alog"], prm["dskip_x"], prm["ssd_gn"],
      jnp.asarray(np.tril(np.ones((Q, Q), np.float32)), BF16), prm["expand"],
      jnp.asarray(np.arange(HL)[:, None] == (np.arange(H * Q) // Q)[None, :], BF16), prm["expand_t"])


def _latent_kernel(ckv_ref, krr_ref, krot_ref, g_ref, cos_ref, sin_ref, c_ref, kr2_ref, kr_ref, *, rope):
    x = ckv_ref[0]
    ms = jnp.mean(x * x, axis=-1, keepdims=True)
    c_ref[0] = x * lax.rsqrt(ms + EPS) * g_ref[...]
    kr2 = krr_ref[0] * cos_ref[...] + krot_ref[0] * sin_ref[...]
    kr2_ref[0] = kr2
    kr_ref[0] = kr2[:, :rope]


def _latent(proj3, offs, prm, cos4, sin4, tm=256):
    b, l, _ = proj3.shape
    kvl, rope = prm["KVL"], prm["ROPE"]
    tm = _pick(l, tm, 8)
    col = lambda width, off: pl.BlockSpec((1, tm, width), lambda i, j, o=off // width: (i, j, o))
    return pl.pallas_call(
        functools.partial(_latent_kernel, rope=rope),
        grid=(b, l // tm),
        in_specs=[col(kvl, offs["ckv"]), col(LANES, offs["krr"]), col(LANES, offs["krot"]),
                  pl.BlockSpec((1, kvl), lambda i, j: (0, 0)),
                  pl.BlockSpec((tm, LANES), lambda i, j: (j, 0)), pl.BlockSpec((tm, LANES), lambda i, j: (j, 0))],
        out_specs=[pl.BlockSpec((1, tm, kvl), lambda i, j: (i, j, 0)),
                   pl.BlockSpec((1, tm, LANES), lambda i, j: (i, j, 0)),
                   pl.BlockSpec((1, tm, rope), lambda i, j: (i, j, 0))],
        out_shape=[jax.ShapeDtypeStruct((b, l, kvl), F32), jax.ShapeDtypeStruct((b, l, LANES), F32),
                   jax.ShapeDtypeStruct((b, l, rope), F32)],
        compiler_params=_params(("parallel", "parallel")),
        name="mla_latent",
    )(proj3, proj3, proj3, prm["kv_a_norm"], cos4, sin4)


def _q_kernel(cq_ref, g_ref, w_ref, cos_ref, sin_ref, gn_ref, gr_ref, q_ref, *, MH, qk_dim):
    x = cq_ref[0]
    ms = jnp.mean(x * x, axis=-1, keepdims=True)
    xn = (x * lax.rsqrt(ms + EPS) * g_ref[...]).astype(BF16)
    qf = jnp.dot(xn, w_ref[...], preferred_element_type=F32)
    cos = cos_ref[...]
    sin = sin_ref[...]
    lane = lax.broadcasted_iota(jnp.int32, (1, LANES), 1)
    half_mask = [(lane < LANES // 2).astype(F32), (lane >= LANES // 2).astype(F32)]
    rope0 = MH * MLA_NOPE
    rot0 = rope0 + MH * (LANES // 2)
    for p in range(MH // 2):
        rp = (qf[:, rope0 + p * LANES: rope0 + (p + 1) * LANES] * cos
              + qf[:, rot0 + p * LANES: rot0 + (p + 1) * LANES] * sin)
        for e in (0, 1):
            h = 2 * p + e
            nope = qf[:, h * MLA_NOPE:(h + 1) * MLA_NOPE]
            rh = rp * half_mask[e]
            ssq = jnp.sum(nope * nope, axis=-1, keepdims=True) + jnp.sum(rh * rh, axis=-1, keepdims=True)
            inv = lax.rsqrt(ssq * (1.0 / qk_dim) + EPS)
            q_ref[0, :, h * 2 * LANES: h * 2 * LANES + LANES] = (nope * inv * gn_ref[...]).astype(q_ref.dtype)
            q_ref[0, :, h * 2 * LANES + LANES:(h + 1) * 2 * LANES] = (rh * inv * gr_ref[e:e + 1, :]).astype(q_ref.dtype)


def _q_proj(proj3, offs, prm, cos4, sin4, tm=256):
    b, l, _ = proj3.shape
    ql, mh = prm["QL"], prm["MH"]
    tm = _pick(l, tm, 16)
    wq = prm["wq"]
    return pl.pallas_call(
        functools.partial(_q_kernel, MH=mh, qk_dim=prm["QK"]),
        grid=(b, l // tm),
        in_specs=[pl.BlockSpec((1, tm, ql), lambda i, j, o=offs["cq"] // ql: (i, j, o)),
                  pl.BlockSpec((1, ql), lambda i, j: (0, 0)),
                  pl.BlockSpec(wq.shape, lambda i, j: (0, 0)),
                  pl.BlockSpec((tm, LANES), lambda i, j: (j, 0)), pl.BlockSpec((tm, LANES), lambda i, j: (j, 0)),
                  pl.BlockSpec((1, LANES), lambda i, j: (0, 0)), pl.BlockSpec((2, LANES), lambda i, j: (0, 0))],
        out_specs=pl.BlockSpec((1, tm, mh * 2 * LANES), lambda i, j: (i, j, 0)),
        out_shape=jax.ShapeDtypeStruct((b, l, mh * 2 * LANES), BF16),
        compiler_params=_params(("parallel", "parallel")),
        name="mla_q",
    )(proj3, prm["q_a_norm"], wq, cos4, sin4, prm["q_gn"], prm["q_gr"])


def _kv_kernel(c_ref, kr2_ref, w_ref, gn_ref, gr_ref, k_ref, v_ref, *, MH, qk_dim):
    c = c_ref[0].astype(BF16)
    kv = jnp.dot(c, w_ref[...], preferred_element_type=F32)
    kr2 = kr2_ref[0]
    lane = lax.broadcasted_iota(jnp.int32, (1, LANES), 1)
    kr_lo = kr2 * (lane < LANES // 2).astype(F32)
    kr_ss = jnp.sum(kr_lo * kr_lo, axis=-1, keepdims=True)
    for h in range(MH):
        nope = kv[:, h * MLA_NOPE:(h + 1) * MLA_NOPE]
        ssq = jnp.sum(nope * nope, axis=-1, keepdims=True) + kr_ss
        inv = lax.rsqrt(ssq * (1.0 / qk_dim) + EPS)
        k_ref[0, :, h * 2 * LANES: h * 2 * LANES + LANES] = (nope * inv * gn_ref[...]).astype(k_ref.dtype)
        k_ref[0, :, h * 2 * LANES + LANES:(h + 1) * 2 * LANES] = (kr2 * inv * gr_ref[h % 2:h % 2 + 1, :]).astype(k_ref.dtype)
    v_ref[0] = kv[:, MH * MLA_NOPE:].astype(v_ref.dtype)


def _kv_proj(c_all, kr2_all, prm, tm=256):
    b, lk, kvl = c_all.shape
    mh = prm["MH"]
    tm = _pick(lk, tm, 16)
    wkv = prm["wkv"]
    return pl.pallas_call(
        functools.partial(_kv_kernel, MH=mh, qk_dim=prm["QK"]),
        grid=(b, lk // tm),
        in_specs=[pl.BlockSpec((1, tm, kvl), lambda i, j: (i, j, 0)),
                  pl.BlockSpec((1, tm, LANES), lambda i, j: (i, j, 0)),
                  pl.BlockSpec(wkv.shape, lambda i, j: (0, 0)),
                  pl.BlockSpec((1, LANES), lambda i, j: (0, 0)), pl.BlockSpec((2, LANES), lambda i, j: (0, 0))],
        out_specs=[pl.BlockSpec((1, tm, mh * 2 * LANES), lambda i, j: (i, j, 0)),
                   pl.BlockSpec((1, tm, mh * MLA_V_DIM), lambda i, j: (i, j, 0))],
        out_shape=[jax.ShapeDtypeStruct((b, lk, mh * 2 * LANES), BF16),
                   jax.ShapeDtypeStruct((b, lk, mh * MLA_V_DIM), BF16)],
        compiler_params=_params(("parallel", "parallel")),
        name="mla_kv",
    )(c_all, kr2_all, wkv, prm["k_gn"], prm["k_gr"])


def _attn_kernel(q_ref, k_ref, v_ref, o_ref, *, causal, tq, lq):
    for qi in range(lq // tq):
        rows = slice(qi * tq, (qi + 1) * tq)
        q = q_ref[0, rows, :]
        if causal:
            past = qi * tq
            sd = lax.dot_general(q, k_ref[0, past:past + tq, :], NT_DIMS, preferred_element_type=F32)
            ri = lax.broadcasted_iota(jnp.int32, (tq, tq), 0) // CHUNK
            ci = lax.broadcasted_iota(jnp.int32, (tq, tq), 1) // CHUNK
            sd = jnp.where(ri >= ci, sd, -jnp.inf)
            m = jnp.max(sd, axis=-1, keepdims=True)
            if past > 0:
                sp = lax.dot_general(q, k_ref[0, 0:past, :], NT_DIMS, preferred_element_type=F32)
                m = jnp.maximum(m, jnp.max(sp, axis=-1, keepdims=True))
                pp = jnp.exp(sp - m)
                l = jnp.sum(pp, axis=-1, keepdims=True)
                acc = jnp.dot(pp.astype(BF16), v_ref[0, 0:past, :], preferred_element_type=F32)
            pd = jnp.exp(sd - m)
            ld = jnp.sum(pd, axis=-1, keepdims=True)
            accd = jnp.dot(pd.astype(BF16), v_ref[0, past:past + tq, :], preferred_element_type=F32)
            if past > 0:
                l, acc = l + ld, acc + accd
            else:
                l, acc = ld, accd
        else:
            s = lax.dot_general(q, k_ref[0], NT_DIMS, preferred_element_type=F32)
            m = jnp.max(s, axis=-1, keepdims=True)
            p = jnp.exp(s - m)
            l = jnp.sum(p, axis=-1, keepdims=True)
            acc = jnp.dot(p.astype(BF16), v_ref[0], preferred_element_type=F32)
        o_ref[0, rows, :] = acc / l


def _attention(q, k, v, mh, causal, tq):
    b, lq, _ = q.shape
    lk = k.shape[1]
    return pl.pallas_call(
        functools.partial(_attn_kernel, causal=causal, tq=tq, lq=lq),
        grid=(b, mh),
        in_specs=[pl.BlockSpec((1, lq, 2 * LANES), lambda bi, h: (bi, 0, h)),
                  pl.BlockSpec((1, lk, 2 * LANES), lambda bi, h: (bi, 0, h)),
                  pl.BlockSpec((1, lk, MLA_V_DIM), lambda bi, h: (bi, 0, h))],
        out_specs=pl.BlockSpec((1, lq, MLA_V_DIM), lambda bi, h: (bi, 0, h)),
        out_shape=jax.ShapeDtypeStruct((b, lq, mh * MLA_V_DIM), F32),
        compiler_params=_params(("parallel", "parallel")),
        name="mla_attention",
    )(q, k, v)


def _oddeven_sort_pairs(lo, n):
    def merge(lo, n, r):
        step = r * 2
        if step < n:
            yield from merge(lo, n, step)
            yield from merge(lo + r, n, step)
            for i in range(lo + r, lo + n - r, step):
                yield (i, i + r)
        else:
            yield (lo, lo + r)
    if n > 1:
        m = n // 2
        yield from _oddeven_sort_pairs(lo, m)
        yield from _oddeven_sort_pairs(lo + m, m)
        yield from merge(lo, n, 1)


def _sort_desc(xs):
    xs = list(xs)
    for i, j in _oddeven_sort_pairs(0, len(xs)):
        xs[i], xs[j] = jnp.maximum(xs[i], xs[j]), jnp.minimum(xs[i], xs[j])
    return xs


def _bitonic_merge_desc(xs):
    xs = list(xs)
    n = len(xs)
    d = n // 2
    while d >= 1:
        for i in range(n):
            if i & d == 0:
                xs[i], xs[i + d] = jnp.maximum(xs[i], xs[i + d]), jnp.minimum(xs[i], xs[i + d])
        d //= 2
    return xs


def _merge_sublanes_top(xs, k):
    shift = SUBLANES // 2
    while shift >= 1:
        other = [pltpu.roll(x, shift, 0) for x in xs]
        if len(xs) < k:
            xs = _bitonic_merge_desc(xs + other[::-1])
        else:
            xs = _bitonic_merge_desc([jnp.maximum(xs[i], other[k - 1 - i]) for i in range(k)])
        shift //= 2
    return xs


def _top_sorted(x, k):
    groups = [x[i * SUBLANES:(i + 1) * SUBLANES, :] for i in range(x.shape[0] // SUBLANES)]
    return _merge_sublanes_top(_sort_desc(groups)[:k], k)


def _prefix_count(rows, test):
    w = jnp.where
    g8 = test(rows[7])
    g4 = test(w(g8, rows[11], rows[3]))
    g2 = test(w(g8, w(g4, rows[13], rows[9]), w(g4, rows[5], rows[1])))
    g1 = test(w(g8, w(g4, w(g2, rows[14], rows[12]), w(g2, rows[10], rows[8])),
                w(g4, w(g2, rows[6], rows[4]), w(g2, rows[2], rows[0]))))
    g16 = test(rows[15])
    return w(g8, 8.0, 0.0) + w(g4, 4.0, 0.0) + w(g2, 2.0, 0.0) + w(g1, 1.0, 0.0) + w(g16, 1.0, 0.0)


def _peer_select_kernel(pq_ref, keys_ref, c1_ref, e1_ref, r2_ref, e2_ref, *, PH, HALF, TOPK):
    assert TOPK == 2 * SUBLANES
    tt = pq_ref.shape[1]
    sub = lax.broadcasted_iota(jnp.int32, (SUBLANES, tt), 0)

    def by_sublane(rows):
        out = rows[-1]
        for s in range(len(rows) - 2, -1, -1):
            out = jnp.where(sub == s, rows[s], out)
        return out

    for h in range(PH):
        q1 = pq_ref[h * 2 * HALF: h * 2 * HALF + HALF, :].astype(BF16)
        q2 = pq_ref[h * 2 * HALF + HALF:(h + 1) * 2 * HALF, :].astype(BF16)
        s1 = jnp.dot(keys_ref[h, 0], q1, preferred_element_type=F32)
        s2 = jnp.dot(keys_ref[h, 1], q2, preferred_element_type=F32)
        v1 = _top_sorted(s1, TOPK)
        v2 = _top_sorted(s2, TOPK)
        v2_lo, v2_hi = by_sublane(v2[:SUBLANES]), by_sublane(v2[SUBLANES:])
        cands = [v1[0] + v2_lo, v1[0] + v2_hi, v1[1] + v2_lo, v1[2] + v2_lo, v1[3] + v2_lo, v1[4] + v2_lo,
                 by_sublane(v1[SUBLANES:]) + v2[0],
                 jnp.where(sub < 6,
                           by_sublane([v1[5], v1[5], v1[6], v1[6], v1[7], v1[7], v1[7], v1[7]])
                           + jnp.where(sub % 2 == 0, v2[0], v2[1]), -jnp.inf)]
        top = _merge_sublanes_top(_sort_desc(cands), TOPK)
        tau = top[TOPK - 1]
        m = top[0]
        zsum = None
        for c in cands:
            z = jnp.where(c >= tau, jnp.exp(c - m), 0.0)
            zsum = z if zsum is None else zsum + z
        zsum = jnp.sum(zsum, axis=0, keepdims=True)
        inv_z = 1.0 / zsum
        for g16 in range(s1.shape[0] // ROWS_BF16):
            ranks, gates = [], []
            for g8 in (2 * g16, 2 * g16 + 1):
                rows = slice(g8 * SUBLANES, (g8 + 1) * SUBLANES)
                a, b2 = s1[rows, :], s2[rows, :]
                count1 = _prefix_count(v2, lambda probe: a + probe >= tau)
                rank2 = _prefix_count(v2, lambda probe: probe > b2)
                c1_ref[h, rows, :] = count1
                e1_ref[h, rows, :] = jnp.exp(a - v1[0])
                ranks.append(rank2)
                gates.append(jnp.exp(b2 - v2[0]) * inv_z)
            rows16 = slice(g16 * ROWS_BF16, (g16 + 1) * ROWS_BF16)
            r2_ref[h, rows16, :] = jnp.concatenate(ranks, axis=0).astype(BF16)
            e2_ref[h, rows16, :] = jnp.concatenate(gates, axis=0).astype(BF16)


def _peer_select(pq_t, keys_b, tt=256):
    t = pq_t.shape[1]
    ph, _, nk, half = keys_b.shape
    tt = _pick(t, tt, LANES)
    big = lambda: pl.BlockSpec((ph, nk, tt), lambda i: (0, 0, i))
    shp = lambda dt: jax.ShapeDtypeStruct((ph, nk, t), dt)
    return pl.pallas_call(
        functools.partial(_peer_select_kernel, PH=ph, HALF=half, TOPK=PEER_TOPK),
        grid=(t // tt,),
        in_specs=[pl.BlockSpec((ph * 2 * half, tt), lambda i: (0, i)),
                  pl.BlockSpec(keys_b.shape, lambda i: (0, 0, 0, 0))],
        out_specs=[big(), big(), big(), big()],
        out_shape=[shp(F32), shp(F32), shp(BF16), shp(BF16)],
        compiler_params=_params(("parallel",)),
        name="peer_select",
    )(pq_t, keys_b)


def _gelu(x):
    return 0.5 * x * (1.0 + lax.erf(x * (1.0 / math.sqrt(2.0))))


ROWS_BF16 = 16
GATE_ROWS = 128


def _peer_main_kernel(hn_ref, u_ref, vt_ref, c1_ref, e1_ref, r2_ref, e2_ref, out_ref,
                      ata_ref, atb_ref, cta_ref, ctb_ref, *, PH, NK, NI, NJ):
    s = pl.program_id(0)
    tt = ata_ref.shape[1]

    @pl.when(s == 0)
    def _init():
        for ref in (ata_ref, atb_ref, cta_ref, ctb_ref):
            ref[...] = jnp.zeros(ref.shape, ref.dtype)

    @pl.when(lax.rem(jnp.maximum(s - 2, 0), NJ) == 0)
    def _zero_out():
        out_ref[...] = jnp.zeros(out_ref.shape, F32)

    def step(at_new, at_prev, ct_new, ct_prev):
        d = out_ref.shape[0]
        blk = lax.rem(jnp.maximum(s - 1, 0), NJ)

        def mix_rows(m0, m1):
            out_ref[m0:m1, :] += jnp.dot(vt_ref[m0:m1, :], ct_prev[...], preferred_element_type=F32)

        def score_block(m0, m1, n0, n1):
            at_new[m0:m1, n0:n1] = jnp.dot(u_ref[m0:m1, :], hn_ref[:, n0:n1], preferred_element_type=F32)

        def gate_tile(ii, tc, r0):
            row = blk * NI + ii
            cols = slice(tc * LANES, (tc + 1) * LANES)
            groups = [slice(r0 + k * ROWS_BF16, r0 + (k + 1) * ROWS_BF16) for k in range(GATE_ROWS // ROWS_BF16)]
            g = [None] * len(groups)
            zero = jnp.zeros((ROWS_BF16, LANES), BF16)
            for h in range(PH):
                bcast = lambda r: jnp.broadcast_to(r[:, cols], (ROWS_BF16, LANES)).astype(BF16)
                c1b = bcast(c1_ref[h, pl.ds(row, 1), :])
                e1b = bcast(e1_ref[h, pl.ds(row, 1), :])
                for k, rws in enumerate(groups):
                    term = e1b * jnp.where(r2_ref[h, rws, cols] < c1b, e2_ref[h, rws, cols], zero)
                    g[k] = term if g[k] is None else g[k] + term
            for k, rws in enumerate(groups):
                orow = slice(ii * NK + rws.start, ii * NK + rws.stop)
                ct_new[orow, cols] = _gelu(at_prev[orow, cols]).astype(BF16) * g[k]

        tiles = [(ii, tc, r0) for ii in range(NI) for tc in range(tt // LANES) for r0 in range(0, NK, GATE_ROWS)]
        n_mix = 4
        n_sn = 1
        n_sm = 1
        ne = at_new.shape[0]
        mxu_items = [functools.partial(mix_rows, k * d // n_mix, (k + 1) * d // n_mix) for k in range(n_mix)]
        mxu_items += [functools.partial(score_block, km * ne // n_sm, (km + 1) * ne // n_sm,
                                        kn * tt // n_sn, (kn + 1) * tt // n_sn)
                      for kn in range(n_sn) for km in range(n_sm)]
        weights = [n_sn * n_sm] * n_mix + [n_mix] * (n_sn * n_sm)
        done, total = 0, sum(weights)
        for item, w in zip(mxu_items, weights):
            upto = len(tiles) * (done + w) // total
            for tile_args in tiles[len(tiles) * done // total: upto]:
                gate_tile(*tile_args)
            item()
            done += w

    @pl.when(s % 2 == 0)
    def _even():
        step(ata_ref, atb_ref, ctb_ref, cta_ref)

    @pl.when(s % 2 == 1)
    def _odd():
        step(atb_ref, ata_ref, cta_ref, ctb_ref)


def _peer_main(hn_t, u_b, vt_b, sel, tt=512, ne=512):
    d, t = hn_t.shape
    c1, e1, r2, e2 = sel
    ph, nk, _ = c1.shape
    nexp = u_b.shape[0]
    tt = _pick(t, tt, LANES)
    ne = _pick(nexp, ne, nk)
    ni = ne // nk
    nj = nexp // ne
    n_tiles = t // tt
    n_blocks = n_tiles * nj
    assert nj % 2 == 0
    once = dict(pipeline_mode=pl.Buffered(1))
    tile = lambda s, lag: jnp.minimum(jnp.maximum(s - lag, 0) // nj, n_tiles - 1)
    big = lambda: pl.BlockSpec((ph, nk, tt), lambda s: (0, 0, tile(s, 1)), **once)
    return pl.pallas_call(
        functools.partial(_peer_main_kernel, PH=ph, NK=nk, NI=ni, NJ=nj),
        grid=(n_blocks + 2,),
        in_specs=[pl.BlockSpec((d, tt), lambda s: (0, tile(s, 0)), **once),
                  pl.BlockSpec((ne, d), lambda s: (lax.rem(s, nj), 0)),
                  pl.BlockSpec((d, ne), lambda s: (0, lax.rem(jnp.maximum(s - 2, 0), nj))),
                  big(), big(), big(), big()],
        out_specs=pl.BlockSpec((d, tt), lambda s: (0, tile(s, 2))),
        out_shape=jax.ShapeDtypeStruct((d, t), F32),
        scratch_shapes=[pltpu.VMEM((ne, tt), F32), pltpu.VMEM((ne, tt), F32),
                        pltpu.VMEM((ne, tt), BF16), pltpu.VMEM((ne, tt), BF16)],
        compiler_params=_params(("arbitrary",)),
        name="peer_main",
    )(hn_t, u_b, vt_b, c1, e1, r2, e2)


def _tadd_kernel(x_ref, yt_ref, o_ref):
    o_ref[...] = x_ref[...] + yt_ref[...].T


def _transpose_add(x2d, yt, tm=256):
    t, d = x2d.shape
    tm = _pick(t, tm, LANES)
    return pl.pallas_call(
        _tadd_kernel,
        grid=(t // tm,),
        in_specs=[pl.BlockSpec((tm, d), lambda i: (i, 0)), pl.BlockSpec((d, tm), lambda i: (0, i))],
        out_specs=pl.BlockSpec((tm, d), lambda i: (i, 0)),
        out_shape=jax.ShapeDtypeStruct((t, d), F32),
        compiler_params=_params(("parallel",)),
        name="peer_residual",
    )(x2d, yt)


def _prepare(norm_mix, w_in, conv_w, conv_b, dt_bias, a_log, d_skip, ssd_norm,
             q_a_norm, w_q_up, kv_a_norm, w_kv_up, q_norm, k_norm, attn_out_norm,
             w_out, norm_ffn, peer_w_q, peer_keys, peer_u, peer_v, rope, N):
    d = w_in.shape[0]
    W = ssd_norm.shape[0]
    H = dt_bias.shape[0]
    P = W // H
    xbc = conv_w.shape[1]
    GN = (xbc - W) // 2
    G = GN // N
    QL = q_a_norm.shape[0]
    KVL = kv_a_norm.shape[0]
    MW = attn_out_norm.shape[0]
    MH = MW // MLA_V_DIM
    QK = MLA_NOPE + rope
    HL = _round_up(H, LANES)
    half = rope // 2
    assert 2 * rope == LANES and MH % 2 == 0 and P == SSD_HEAD_DIM

    s = np.cumsum([0, W, xbc, H, QL, KVL, rope])
    wz, wxbc, wdt, wcq, wckv, wkr = (w_in[:, s[i]:s[i + 1]] for i in range(6))
    wrot = jnp.concatenate([-wkr[:, half:], wkr[:, :half]], axis=1)
    pieces = [wz, wxbc, wcq, wckv, wkr, wkr, wrot, wrot, wdt]
    offs, o = {}, 0
    for name, width in (("z", W), ("xs", W), ("bm", GN), ("cm", GN), ("cq", QL), ("ckv", KVL),
                        ("krr", LANES), ("krot", LANES), ("dt", HL)):
        assert o % width == 0, (name, o, width)
        offs[name] = o
        o += width
    NP = _round_up(o, 1024)
    w_in_p = jnp.concatenate(pieces + [jnp.zeros((d, NP - (o - HL + H)), w_in.dtype)], axis=1).astype(BF16)

    wq3 = w_q_up.reshape(QL, MH, QK)
    wq_nope = wq3[:, :, :MLA_NOPE].reshape(QL, MH * MLA_NOPE)
    wq_rope = wq3[:, :, MLA_NOPE:]
    wq_rot = jnp.concatenate([-wq_rope[:, :, half:], wq_rope[:, :, :half]], axis=2)
    wq = jnp.concatenate([wq_nope, wq_rope.reshape(QL, MH * rope), wq_rot.reshape(QL, MH * rope)], axis=1).astype(BF16)

    wkv3 = w_kv_up.reshape(KVL, MH, MLA_NOPE + MLA_V_DIM)
    wkv = jnp.concatenate([wkv3[:, :, :MLA_NOPE].reshape(KVL, MH * MLA_NOPE),
                           wkv3[:, :, MLA_NOPE:].reshape(KVL, MH * MLA_V_DIM)], axis=1).astype(BF16)

    scale = QK ** -0.5
    zeros_h = jnp.zeros((rope,), F32)

    def gains(g, sc):
        gn = (g[:MLA_NOPE] * sc).reshape(1, LANES)
        gr = jnp.stack([jnp.concatenate([g[MLA_NOPE:] * sc, zeros_h]), jnp.concatenate([zeros_h, g[MLA_NOPE:] * sc])])
        return gn.astype(F32), gr.astype(F32)

    q_gn, q_gr = gains(q_norm, scale)
    k_gn, k_gr = gains(k_norm, 1.0)

    pad_h = lambda v: jnp.concatenate([v.astype(F32), jnp.zeros((HL - H,), F32)]).reshape(1, HL)
    head_of_col = np.arange(W) // P
    expand = (np.arange(HL)[:, None] == head_of_col[None, :]).astype(np.float32)

    return dict(
        H=H, P=P, G=G, N=N, HL=HL, QL=QL, KVL=KVL, MH=MH, QK=QK, ROPE=rope, NP=NP, offs=offs,
        norm_mix=norm_mix, w_in=w_in_p,
        cwx=conv_w[:, :W], cwb=conv_w[:, W:W + GN], cwc=conv_w[:, W + GN:],
        cbx=conv_b[:W].reshape(1, W), cbb=conv_b[W:W + GN].reshape(1, GN), cbc=conv_b[W + GN:].reshape(1, GN),
        dtb=pad_h(dt_bias), alog=pad_h(a_log),
        dskip_x=jnp.repeat(d_skip.astype(F32), P).reshape(1, W), ssd_gn=ssd_norm.reshape(1, W).astype(F32),
        expand=jnp.asarray(expand, BF16), expand_t=jnp.asarray(expand.T, BF16),
        q_a_norm=q_a_norm.reshape(1, QL).astype(F32), kv_a_norm=kv_a_norm.reshape(1, KVL).astype(F32),
        wq=wq, wkv=wkv, q_gn=q_gn, q_gr=q_gr, k_gn=k_gn, k_gr=k_gr,
        attn_out_norm=attn_out_norm, w_out_a=w_out[:W].astype(BF16), w_out_b=w_out[W:].astype(BF16),
        norm_ffn=norm_ffn, peer_w_q_t=peer_w_q.T.astype(BF16), peer_keys=peer_keys.astype(BF16),
        peer_u=peer_u.astype(BF16), peer_vt=peer_v.T.astype(BF16),
    )


def _rope_tables(pos, rope):
    inv = 1.0 / (ROPE_BASE ** (jnp.arange(0, rope, 2, dtype=F32) / rope))
    ang = pos.astype(F32)[:, None] * inv[None, :]
    reps = LANES // (rope // 2)
    return jnp.tile(jnp.cos(ang), (1, reps)), jnp.tile(jnp.sin(ang), (1, reps))


def _layer(x, pos, past_ckv, past_krope, conv_hist, ssm_h0, prm):
    b, l, d = x.shape
    t = b * l
    H, P, G, N = prm["H"], prm["P"], prm["G"], prm["N"]
    W, GN = H * P, G * N
    offs = prm["offs"]
    x2d = x.reshape(t, d)

    h = _rms_cast(x2d, prm["norm_mix"])
    proj = _matmul([(h, prm["w_in"])], tm=1024, tn=1024, name="in_proj")
    proj3 = proj.reshape(b, l, prm["NP"])

    hist = (conv_hist[:, :, :W], conv_hist[:, :, W:W + GN], conv_hist[:, :, W + GN:])
    y_ssd, h_last, nx, nb, ncm = _ssd(proj3, offs, hist, ssm_h0.reshape(b, W, N), prm, _pick(l, SSD_CHUNK, 8))
    conv_new = jnp.concatenate([nx, nb, ncm], axis=-1)
    h_last = h_last.reshape(b, H, P, N)

    cos4, sin4 = _rope_tables(pos, prm["ROPE"])
    c_new, kr2_new, kr_new = _latent(proj3, offs, prm, cos4, sin4)
    q = _q_proj(proj3, offs, prm, cos4, sin4)
    if past_ckv is None:
        c_all, kr2_all = c_new, kr2_new
    else:
        c_all = jnp.concatenate([past_ckv, c_new], axis=1)
        kr2_all = jnp.concatenate([jnp.concatenate([past_krope, past_krope], axis=-1), kr2_new], axis=1)
    lk = c_all.shape[1]
    k, v = _kv_proj(c_all, kr2_all, prm, tm=_pick(lk, 512, 16))
    if past_ckv is None:
        o = _attention(q, k, v, prm["MH"], True, _pick(l, 256, CHUNK))
    else:
        assert l <= CHUNK and (lk - l) % CHUNK == 0
        o = _attention(q, k, v, prm["MH"], False, l)
    o_n = _rms_cast(o.reshape(t, -1), prm["attn_out_norm"])

    x_mid = _matmul([(y_ssd.reshape(t, W), prm["w_out_a"]), (o_n, prm["w_out_b"])], res=x2d,
                    tm=1024, tn=1024, name="out_proj")

    hn_t = _rms_cast(x_mid, prm["norm_ffn"], transpose=True)
    pq_t = _matmul([(prm["peer_w_q_t"], hn_t)], tm=1024, tn=512, name="peer_query")
    sel = _peer_select(pq_t, prm["peer_keys"])
    yt = _peer_main(hn_t, prm["peer_u"], prm["peer_vt"], sel)
    out = _transpose_add(x_mid, yt)
    return out.reshape(b, l, d), c_new, kr_new, h_last.astype(ssm_h0.dtype), conv_new


def kernel(x_prompt, x_sample, cache_mla_ckv, cache_mla_krope, state_ssm, state_conv, norm_mix, w_in, conv_w, conv_b, dt_bias, a_log, d_skip, ssd_norm, q_a_norm, w_q_up, kv_a_norm, w_kv_up, q_norm, k_norm, attn_out_norm, w_out, norm_ffn, peer_w_q, peer_keys, peer_u, peer_v):
    bp, lp, _ = x_prompt.shape
    ls = x_sample.shape[1]
    depth = norm_mix.shape[0]
    past = cache_mla_ckv.shape[2]
    rope = cache_mla_krope.shape[-1]
    nstate = state_ssm.shape[-1]
    xbc = state_conv.shape[-1]
    pos_p = jnp.arange(lp, dtype=jnp.int32)
    pos_s = past + jnp.arange(ls, dtype=jnp.int32)
    yp, ys = x_prompt, x_sample
    outs_p, outs_s = [], []
    for layer in range(depth):
        prm = _prepare(norm_mix[layer], w_in[layer], conv_w[layer], conv_b[layer], dt_bias[layer], a_log[layer],
                       d_skip[layer], ssd_norm[layer], q_a_norm[layer], w_q_up[layer], kv_a_norm[layer],
                       w_kv_up[layer], q_norm[layer], k_norm[layer], attn_out_norm[layer], w_out[layer],
                       norm_ffn[layer], peer_w_q[layer], peer_keys[layer], peer_u[layer], peer_v[layer],
                       rope, nstate)
        zero_conv = jnp.zeros((bp, state_conv.shape[2], xbc), x_prompt.dtype)
        zero_ssm = jnp.zeros((bp,) + state_ssm.shape[2:], state_ssm.dtype)
        yp, *rest_p = _layer(yp, pos_p, None, None, zero_conv, zero_ssm, prm)
        outs_p.append(rest_p)
        ys, *rest_s = _layer(ys, pos_s, cache_mla_ckv[layer], cache_mla_krope[layer],
                             state_conv[layer], state_ssm[layer], prm)
        outs_s.append(rest_s)
    stack = lambda outs, i: jnp.stack([o[i] for o in outs])
    return (yp, ys,
            stack(outs_p, 0), stack(outs_p, 1), stack(outs_p, 2), stack(outs_p, 3),
            stack(outs_s, 0), stack(outs_s, 1), stack(outs_s, 2), stack(outs_s, 3))
```

```python
import functools
import math

import numpy as np
import jax
import jax.numpy as jnp
from jax import lax
from jax.experimental import pallas as pl
from jax.experimental.pallas import tpu as pltpu

F32 = jnp.float32
BF16 = jnp.bfloat16

EPS = 1e-6
CHUNK = 64
SSD_CHUNK = 128
SSD_HEAD_DIM = 64
MLA_NOPE = 128
MLA_V_DIM = 128
ROPE_BASE = 10000.0
PEER_TOPK = 16
LANES = 128
SUBLANES = 8
VMEM_LIMIT = 56 * 1024 * 1024

NT_DIMS = (((1,), (1,)), ((), ()))
TN_DIMS = (((0,), (0,)), ((), ()))


def _pick(n, pref, mult):
    if n <= pref:
        return n
    t = (pref // mult) * mult
    while t >= mult:
        if n % t == 0:
            return t
        t -= mult
    return n


def _round_up(n, m):
    return (n + m - 1) // m * m


def _params(sem, vmem=VMEM_LIMIT):
    return pltpu.CompilerParams(dimension_semantics=sem, vmem_limit_bytes=vmem)


def _split3(x):
    hi = x.astype(BF16)
    r1 = x - hi.astype(F32)
    mid = r1.astype(BF16)
    lo = (r1 - mid.astype(F32)).astype(BF16)
    return hi, mid, lo


def _dot3(a01, x, dims=None):
    out = None
    for piece in _split3(x):
        if dims is None:
            d = jnp.dot(a01, piece, preferred_element_type=F32)
        else:
            d = lax.dot_general(a01, piece, dims, preferred_element_type=F32)
        out = d if out is None else out + d
    return out


def _dot3_right(x, b01, dims=None):
    out = None
    for piece in _split3(x):
        if dims is None:
            d = jnp.dot(piece, b01, preferred_element_type=F32)
        else:
            d = lax.dot_general(piece, b01, dims, preferred_element_type=F32)
        out = d if out is None else out + d
    return out


def _rms_cast_kernel(x_ref, g_ref, o_ref):
    x = x_ref[...].astype(F32)
    ms = jnp.mean(x * x, axis=-1, keepdims=True)
    o_ref[...] = (x * lax.rsqrt(ms + EPS) * g_ref[...]).astype(o_ref.dtype)


def _rms_cast_t_kernel(x_ref, g_ref, o_ref):
    x = x_ref[...].astype(F32)
    ms = jnp.mean(x * x, axis=-1, keepdims=True)
    o_ref[...] = (x * lax.rsqrt(ms + EPS) * g_ref[...]).T.astype(o_ref.dtype)


def _rms_cast(x2d, g, out_dtype=BF16, tm=256, transpose=False):
    t, d = x2d.shape
    tm = _pick(t, tm, LANES if transpose else 16)
    if transpose:
        body, out_spec, out_shape = _rms_cast_t_kernel, pl.BlockSpec((d, tm), lambda i: (0, i)), (d, t)
    else:
        body, out_spec, out_shape = _rms_cast_kernel, pl.BlockSpec((tm, d), lambda i: (i, 0)), (t, d)
    return pl.pallas_call(
        body,
        grid=(t // tm,),
        in_specs=[pl.BlockSpec((tm, d), lambda i: (i, 0)), pl.BlockSpec((1, d), lambda i: (0, 0))],
        out_specs=out_spec,
        out_shape=jax.ShapeDtypeStruct(out_shape, out_dtype),
        compiler_params=_params(("parallel",)),
        name="rms_cast_t" if transpose else "rms_cast",
    )(x2d, g.reshape(1, d).astype(F32))


def _mm_kernel(*refs, n_pairs, has_res):
    o_ref = refs[-1]
    acc = None
    for p in range(n_pairs):
        d = jnp.dot(refs[2 * p][...], refs[2 * p + 1][...], preferred_element_type=F32)
        acc = d if acc is None else acc + d
    if has_res:
        acc = refs[2 * n_pairs][...] + acc
    o_ref[...] = acc.astype(o_ref.dtype)


def _matmul(pairs, res=None, out_dtype=F32, tm=512, tn=512, name="matmul"):
    m = pairs[0][0].shape[0]
    n = pairs[0][1].shape[1]
    tm = _pick(m, tm, 16)
    tn = _pick(n, tn, LANES)
    in_specs, args = [], []
    for a, w in pairs:
        k = a.shape[1]
        in_specs += [pl.BlockSpec((tm, k), lambda i, j: (i, 0)), pl.BlockSpec((k, tn), lambda i, j: (0, j))]
        args += [a, w]
    if res is not None:
        in_specs.append(pl.BlockSpec((tm, tn), lambda i, j: (i, j)))
        args.append(res)
    return pl.pallas_call(
        functools.partial(_mm_kernel, n_pairs=len(pairs), has_res=res is not None),
        grid=(m // tm, n // tn),
        in_specs=in_specs,
        out_specs=pl.BlockSpec((tm, tn), lambda i, j: (i, j)),
        out_shape=jax.ShapeDtypeStruct((m, n), out_dtype),
        compiler_params=_params(("parallel", "arbitrary")),
        name=name,
    )(*args)


def _silu(x):
    hx = 0.5 * x
    return hx + hx * jnp.tanh(hx)


def _softplus(x):
    return jnp.maximum(x, 0.0) + jnp.log1p(jnp.exp(-jnp.abs(x)))


def _ssd_kernel(z_ref, xs_ref, bm_ref, cm_ref, dt_ref,
                hx_ref, hb_ref, hc_ref, h0_ref,
                cwx_ref, cwb_ref, cwc_ref, cbx_ref, cbb_ref, cbc_ref,
                dtb_ref, alog_ref, dskip_ref, gn_ref,
                tri_ref, exp_ref,
                y_ref, hlast_ref, nx_ref, nb_ref, nc_ref,
                bx_ref, bb_ref, bc_ref, state_ref, ydiag_ref,
                *, Q, H, P, G, N):
    c = pl.program_id(1)
    nc = pl.num_programs(1)
    R = H // G
    W = H * P

    @pl.when(c == 0)
    def _init():
        for buf, hist in ((bx_ref, hx_ref), (bb_ref, hb_ref), (bc_ref, hc_ref)):
            buf[0:8, :] = jnp.zeros((8, buf.shape[1]), F32)
            buf[5:8, :] = hist[0]
        state_ref[...] = h0_ref[0]

    def conv(buf, x_ref, cw_ref, cb_ref):
        x = x_ref[0]
        buf[8:8 + Q, :] = x
        win = buf[...]
        acc = None
        for k in range(3):
            tap = pltpu.roll(win, Q + 3 - k, 0)[0:Q, :] * cw_ref[k:k + 1, :]
            acc = tap if acc is None else acc + tap
        acc = acc + x * cw_ref[3:4, :]
        buf[0:8, :] = buf[Q:Q + 8, :]
        return _silu(cb_ref[...] + acc)

    xs = conv(bx_ref, xs_ref, cwx_ref, cbx_ref)
    bm = conv(bb_ref, bm_ref, cwb_ref, cbb_ref)
    cm = conv(bc_ref, cm_ref, cwc_ref, cbc_ref)

    @pl.when(c == nc - 1)
    def _tail():
        nx_ref[0] = bx_ref[5:8, :]
        nb_ref[0] = bb_ref[5:8, :]
        nc_ref[0] = bc_ref[5:8, :]

    dt = _softplus(dt_ref[0] + dtb_ref[...])
    a_neg = -jnp.exp(alog_ref[...])
    dta = dt * a_neg
    tri = tri_ref[...]
    acum = _dot3(tri, dta)
    eye = (lax.broadcasted_iota(jnp.int32, (Q, Q), 0) == lax.broadcasted_iota(jnp.int32, (Q, Q), 1)).astype(BF16)
    acum_t = _dot3_right(acum, eye, TN_DIMS)
    dt_t = _dot3_right(dt, eye, TN_DIMS)

    acum_x = _dot3_right(acum, exp_ref[...])
    dt_x = _dot3_right(dt, exp_ref[...])

    ii = lax.broadcasted_iota(jnp.int32, (Q, Q), 0)
    jj = lax.broadcasted_iota(jnp.int32, (Q, Q), 1)
    causal = ii >= jj
    xs_b = xs.astype(BF16)
    bm_b = bm.astype(BF16)
    cm_b = cm.astype(BF16)
    st_b = state_ref[...].astype(BF16)

    exp_acum_x = jnp.exp(acum_x)
    decay_x = jnp.exp(acum_x[Q - 1:Q, :] - acum_x) * dt_x
    xd_b = (xs * decay_x).astype(BF16)

    chunk_decay = jnp.exp(acum_t[:, Q - 1:Q])

    for g in range(G):
        cg = cm_b[:, g * N:(g + 1) * N]
        bg = bm_b[:, g * N:(g + 1) * N]
        cb = lax.dot_general(cg, bg, NT_DIMS, preferred_element_type=F32)
        for r in range(R):
            h = g * R + r
            dm = acum[:, h:h + 1] - acum_t[h:h + 1, :]
            lm = jnp.where(causal, jnp.exp(dm), 0.0)
            wm = (cb * lm * dt_t[h:h + 1, :]).astype(BF16)
            ydiag_ref[:, h * P:(h + 1) * P] = jnp.dot(wm, xs_b[:, h * P:(h + 1) * P],
                                                      preferred_element_type=F32)
        rows = slice(g * R * P, (g + 1) * R * P)
        y_off = lax.dot_general(cg, st_b[rows, :], NT_DIMS, preferred_element_type=F32)
        ydiag_ref[:, rows] = ydiag_ref[:, rows] + y_off * exp_acum_x[:, rows]
        s_new = lax.dot_general(xd_b[:, rows], bg, TN_DIMS, preferred_element_type=F32)
        for r in range(R):
            h = g * R + r
            hr = slice(h * P, (h + 1) * P)
            state_ref[hr, :] = chunk_decay[h:h + 1, :] * state_ref[hr, :] + s_new[r * P:(r + 1) * P, :]

    y = ydiag_ref[...] + dskip_ref[...] * xs
    yg = y * _silu(z_ref[0])
    ms = jnp.mean(yg * yg, axis=-1, keepdims=True)
    y_ref[0] = (yg * lax.rsqrt(ms + EPS) * gn_ref[...]).astype(y_ref.dtype)

    @pl.when(c == nc - 1)
    def _final():
        hlast_ref[0] = state_ref[...]


def _ssd(proj3, offs, hist, h0, prm, Q):
    b, l, _ = proj3.shape
    H, P, G, N = prm["H"], prm["P"], prm["G"], prm["N"]
    W, GN, HL = H * P, G * N, prm["HL"]
    nchunk = l // Q
    hx, hb, hc = hist
    full = lambda shape: pl.BlockSpec(shape, lambda i, j: (0,) * len(shape))
    per_b = lambda shape: pl.BlockSpec(shape, lambda i, j: (i,) + (0,) * (len(shape) - 1))
    col = lambda width, off: pl.BlockSpec((1, Q, width), lambda i, j, o=off // width: (i, j, o))
    in_specs = [
        col(W, offs["z"]), col(W, offs["xs"]), col(GN, offs["bm"]), col(GN, offs["cm"]), col(HL, offs["dt"]),
        per_b((1, 3, W)), per_b((1, 3, GN)), per_b((1, 3, GN)), per_b((1, W, N)),
        full((4, W)), full((4, GN)), full((4, GN)), full((1, W)), full((1, GN)), full((1, GN)),
        full((1, HL)), full((1, HL)), full((1, W)), full((1, W)),
        full((Q, Q)), full((HL, W)),
    ]
    out_specs = [
        pl.BlockSpec((1, Q, W), lambda i, j: (i, j, 0)),
        per_b((1, W, N)), per_b((1, 3, W)), per_b((1, 3, GN)), per_b((1, 3, GN)),
    ]
    out_shape = [
        jax.ShapeDtypeStruct((b, l, W), BF16),
        jax.ShapeDtypeStruct((b, W, N), F32),
        jax.ShapeDtypeStruct((b, 3, W), F32),
        jax.ShapeDtypeStruct((b, 3, GN), F32),
        jax.ShapeDtypeStruct((b, 3, GN), F32),
    ]
    scratch = [
        pltpu.VMEM((Q + 8, W), F32), pltpu.VMEM((Q + 8, GN), F32), pltpu.VMEM((Q + 8, GN), F32),
        pltpu.VMEM((W, N), F32), pltpu.VMEM((Q, W), F32),
    ]
    return pl.pallas_call(
        functools.partial(_ssd_kernel, Q=Q, H=H, P=P, G=G, N=N),
        grid=(b, nchunk),
        in_specs=in_specs, out_specs=out_specs, out_shape=out_shape, scratch_shapes=scratch,
        compiler_params=_params(("parallel", "arbitrary")),
        name="ssd_scan",
    )(proj3, proj3, proj3, proj3, proj3, hx, hb, hc, h0,
      prm["cwx"], prm["cwb"], prm["cwc"], prm["cbx"], prm["cbb"], prm["cbc"],
      prm["dtb"], prm["alog"], prm["dskip_x"], prm["ssd_gn"],
      jnp.asarray(np.tril(np.ones((Q, Q), np.float32)), BF16), prm["expand"])


def _latent_kernel(ckv_ref, krr_ref, krot_ref, g_ref, cos_ref, sin_ref, c_ref, kr2_ref, kr_ref, *, rope):
    x = ckv_ref[0]
    ms = jnp.mean(x * x, axis=-1, keepdims=True)
    c_ref[0] = x * lax.rsqrt(ms + EPS) * g_ref[...]
    kr2 = krr_ref[0] * cos_ref[...] + krot_ref[0] * sin_ref[...]
    kr2_ref[0] = kr2
    kr_ref[0] = kr2[:, :rope]


def _latent(proj3, offs, prm, cos4, sin4, tm=256):
    b, l, _ = proj3.shape
    kvl, rope = prm["KVL"], prm["ROPE"]
    tm = _pick(l, tm, 8)
    col = lambda width, off: pl.BlockSpec((1, tm, width), lambda i, j, o=off // width: (i, j, o))
    return pl.pallas_call(
        functools.partial(_latent_kernel, rope=rope),
        grid=(b, l // tm),
        in_specs=[col(kvl, offs["ckv"]), col(LANES, offs["krr"]), col(LANES, offs["krot"]),
                  pl.BlockSpec((1, kvl), lambda i, j: (0, 0)),
                  pl.BlockSpec((tm, LANES), lambda i, j: (j, 0)), pl.BlockSpec((tm, LANES), lambda i, j: (j, 0))],
        out_specs=[pl.BlockSpec((1, tm, kvl), lambda i, j: (i, j, 0)),
                   pl.BlockSpec((1, tm, LANES), lambda i, j: (i, j, 0)),
                   pl.BlockSpec((1, tm, rope), lambda i, j: (i, j, 0))],
        out_shape=[jax.ShapeDtypeStruct((b, l, kvl), F32), jax.ShapeDtypeStruct((b, l, LANES), F32),
                   jax.ShapeDtypeStruct((b, l, rope), F32)],
        compiler_params=_params(("parallel", "parallel")),
        name="mla_latent",
    )(proj3, proj3, proj3, prm["kv_a_norm"], cos4, sin4)


def _q_kernel(cq_ref, g_ref, w_ref, cos_ref, sin_ref, gn_ref, gr_ref, q_ref, *, MH, qk_dim):
    x = cq_ref[0]
    ms = jnp.mean(x * x, axis=-1, keepdims=True)
    xn = (x * lax.rsqrt(ms + EPS) * g_ref[...]).astype(BF16)
    qf = jnp.dot(xn, w_ref[...], preferred_element_type=F32)
    cos = cos_ref[...]
    sin = sin_ref[...]
    lane = lax.broadcasted_iota(jnp.int32, (1, LANES), 1)
    half_mask = [(lane < LANES // 2).astype(F32), (lane >= LANES // 2).astype(F32)]
    rope0 = MH * MLA_NOPE
    rot0 = rope0 + MH * (LANES // 2)
    for p in range(MH // 2):
        rp = (qf[:, rope0 + p * LANES: rope0 + (p + 1) * LANES] * cos
              + qf[:, rot0 + p * LANES: rot0 + (p + 1) * LANES] * sin)
        for e in (0, 1):
            h = 2 * p + e
            nope = qf[:, h * MLA_NOPE:(h + 1) * MLA_NOPE]
            rh = rp * half_mask[e]
            ssq = jnp.sum(nope * nope, axis=-1, keepdims=True) + jnp.sum(rh * rh, axis=-1, keepdims=True)
            inv = lax.rsqrt(ssq * (1.0 / qk_dim) + EPS)
            q_ref[0, :, h * 2 * LANES: h * 2 * LANES + LANES] = (nope * inv * gn_ref[...]).astype(q_ref.dtype)
            q_ref[0, :, h * 2 * LANES + LANES:(h + 1) * 2 * LANES] = (rh * inv * gr_ref[e:e + 1, :]).astype(q_ref.dtype)


def _q_proj(proj3, offs, prm, cos4, sin4, tm=256):
    b, l, _ = proj3.shape
    ql, mh = prm["QL"], prm["MH"]
    tm = _pick(l, tm, 16)
    wq = prm["wq"]
    return pl.pallas_call(
        functools.partial(_q_kernel, MH=mh, qk_dim=prm["QK"]),
        grid=(b, l // tm),
        in_specs=[pl.BlockSpec((1, tm, ql), lambda i, j, o=offs["cq"] // ql: (i, j, o)),
                  pl.BlockSpec((1, ql), lambda i, j: (0, 0)),
                  pl.BlockSpec(wq.shape, lambda i, j: (0, 0)),
                  pl.BlockSpec((tm, LANES), lambda i, j: (j, 0)), pl.BlockSpec((tm, LANES), lambda i, j: (j, 0)),
                  pl.BlockSpec((1, LANES), lambda i, j: (0, 0)), pl.BlockSpec((2, LANES), lambda i, j: (0, 0))],
        out_specs=pl.BlockSpec((1, tm, mh * 2 * LANES), lambda i, j: (i, j, 0)),
        out_shape=jax.ShapeDtypeStruct((b, l, mh * 2 * LANES), BF16),
        compiler_params=_params(("parallel", "parallel")),
        name="mla_q",
    )(proj3, prm["q_a_norm"], wq, cos4, sin4, prm["q_gn"], prm["q_gr"])


def _kv_kernel(c_ref, kr2_ref, w_ref, gn_ref, gr_ref, k_ref, v_ref, *, MH, qk_dim):
    c = c_ref[0].astype(BF16)
    kv = jnp.dot(c, w_ref[...], preferred_element_type=F32)
    kr2 = kr2_ref[0]
    lane = lax.broadcasted_iota(jnp.int32, (1, LANES), 1)
    kr_lo = kr2 * (lane < LANES // 2).astype(F32)
    kr_ss = jnp.sum(kr_lo * kr_lo, axis=-1, keepdims=True)
    for h in range(MH):
        nope = kv[:, h * MLA_NOPE:(h + 1) * MLA_NOPE]
        ssq = jnp.sum(nope * nope, axis=-1, keepdims=True) + kr_ss
        inv = lax.rsqrt(ssq * (1.0 / qk_dim) + EPS)
        k_ref[0, :, h * 2 * LANES: h * 2 * LANES + LANES] = (nope * inv * gn_ref[...]).astype(k_ref.dtype)
        k_ref[0, :, h * 2 * LANES + LANES:(h + 1) * 2 * LANES] = (kr2 * inv * gr_ref[h % 2:h % 2 + 1, :]).astype(k_ref.dtype)
    v_ref[0] = kv[:, MH * MLA_NOPE:].astype(v_ref.dtype)


def _kv_proj(c_all, kr2_all, prm, tm=256):
    b, lk, kvl = c_all.shape
    mh = prm["MH"]
    tm = _pick(lk, tm, 16)
    wkv = prm["wkv"]
    return pl.pallas_call(
        functools.partial(_kv_kernel, MH=mh, qk_dim=prm["QK"]),
        grid=(b, lk // tm),
        in_specs=[pl.BlockSpec((1, tm, kvl), lambda i, j: (i, j, 0)),
                  pl.BlockSpec((1, tm, LANES), lambda i, j: (i, j, 0)),
                  pl.BlockSpec(wkv.shape, lambda i, j: (0, 0)),
                  pl.BlockSpec((1, LANES), lambda i, j: (0, 0)), pl.BlockSpec((2, LANES), lambda i, j: (0, 0))],
        out_specs=[pl.BlockSpec((1, tm, mh * 2 * LANES), lambda i, j: (i, j, 0)),
                   pl.BlockSpec((1, tm, mh * MLA_V_DIM), lambda i, j: (i, j, 0))],
        out_shape=[jax.ShapeDtypeStruct((b, lk, mh * 2 * LANES), BF16),
                   jax.ShapeDtypeStruct((b, lk, mh * MLA_V_DIM), BF16)],
        compiler_params=_params(("parallel", "parallel")),
        name="mla_kv",
    )(c_all, kr2_all, wkv, prm["k_gn"], prm["k_gr"])


def _attn_kernel(q_ref, k_ref, v_ref, o_ref, *, causal, tq, lq):
    for qi in range(lq // tq):
        rows = slice(qi * tq, (qi + 1) * tq)
        q = q_ref[0, rows, :]
        if causal:
            past = qi * tq
            sd = lax.dot_general(q, k_ref[0, past:past + tq, :], NT_DIMS, preferred_element_type=F32)
            ri = lax.broadcasted_iota(jnp.int32, (tq, tq), 0) // CHUNK
            ci = lax.broadcasted_iota(jnp.int32, (tq, tq), 1) // CHUNK
            sd = jnp.where(ri >= ci, sd, -jnp.inf)
            m = jnp.max(sd, axis=-1, keepdims=True)
            if past > 0:
                sp = lax.dot_general(q, k_ref[0, 0:past, :], NT_DIMS, preferred_element_type=F32)
                m = jnp.maximum(m, jnp.max(sp, axis=-1, keepdims=True))
                pp = jnp.exp(sp - m)
                l = jnp.sum(pp, axis=-1, keepdims=True)
                acc = jnp.dot(pp.astype(BF16), v_ref[0, 0:past, :], preferred_element_type=F32)
            pd = jnp.exp(sd - m)
            ld = jnp.sum(pd, axis=-1, keepdims=True)
            accd = jnp.dot(pd.astype(BF16), v_ref[0, past:past + tq, :], preferred_element_type=F32)
            if past > 0:
                l, acc = l + ld, acc + accd
            else:
                l, acc = ld, accd
        else:
            s = lax.dot_general(q, k_ref[0], NT_DIMS, preferred_element_type=F32)
            m = jnp.max(s, axis=-1, keepdims=True)
            p = jnp.exp(s - m)
            l = jnp.sum(p, axis=-1, keepdims=True)
            acc = jnp.dot(p.astype(BF16), v_ref[0], preferred_element_type=F32)
        o_ref[0, rows, :] = acc / l


def _attention(q, k, v, mh, causal, tq):
    b, lq, _ = q.shape
    lk = k.shape[1]
    return pl.pallas_call(
        functools.partial(_attn_kernel, causal=causal, tq=tq, lq=lq),
        grid=(b, mh),
        in_specs=[pl.BlockSpec((1, lq, 2 * LANES), lambda bi, h: (bi, 0, h)),
                  pl.BlockSpec((1, lk, 2 * LANES), lambda bi, h: (bi, 0, h)),
                  pl.BlockSpec((1, lk, MLA_V_DIM), lambda bi, h: (bi, 0, h))],
        out_specs=pl.BlockSpec((1, lq, MLA_V_DIM), lambda bi, h: (bi, 0, h)),
        out_shape=jax.ShapeDtypeStruct((b, lq, mh * MLA_V_DIM), F32),
        compiler_params=_params(("parallel", "parallel")),
        name="mla_attention",
    )(q, k, v)


def _oddeven_sort_pairs(lo, n):
    def merge(lo, n, r):
        step = r * 2
        if step < n:
            yield from merge(lo, n, step)
            yield from merge(lo + r, n, step)
            for i in range(lo + r, lo + n - r, step):
                yield (i, i + r)
        else:
            yield (lo, lo + r)
    if n > 1:
        m = n // 2
        yield from _oddeven_sort_pairs(lo, m)
        yield from _oddeven_sort_pairs(lo + m, m)
        yield from merge(lo, n, 1)


def _sort_desc(xs):
    xs = list(xs)
    for i, j in _oddeven_sort_pairs(0, len(xs)):
        xs[i], xs[j] = jnp.maximum(xs[i], xs[j]), jnp.minimum(xs[i], xs[j])
    return xs


def _bitonic_merge_desc(xs):
    xs = list(xs)
    n = len(xs)
    d = n // 2
    while d >= 1:
        for i in range(n):
            if i & d == 0:
                xs[i], xs[i + d] = jnp.maximum(xs[i], xs[i + d]), jnp.minimum(xs[i], xs[i + d])
        d //= 2
    return xs


def _merge_sublanes_top(xs, k):
    shift = SUBLANES // 2
    while shift >= 1:
        other = [pltpu.roll(x, shift, 0) for x in xs]
        if len(xs) < k:
            xs = _bitonic_merge_desc(xs + other[::-1])
        else:
            xs = _bitonic_merge_desc([jnp.maximum(xs[i], other[k - 1 - i]) for i in range(k)])
        shift //= 2
    return xs


def _top_sorted(x, k):
    groups = [x[i * SUBLANES:(i + 1) * SUBLANES, :] for i in range(x.shape[0] // SUBLANES)]
    return _merge_sublanes_top(_sort_desc(groups)[:k], k)


def _prefix_count(rows, test):
    w = jnp.where
    g8 = test(rows[7])
    g4 = test(w(g8, rows[11], rows[3]))
    g2 = test(w(g8, w(g4, rows[13], rows[9]), w(g4, rows[5], rows[1])))
    g1 = test(w(g8, w(g4, w(g2, rows[14], rows[12]), w(g2, rows[10], rows[8])),
                w(g4, w(g2, rows[6], rows[4]), w(g2, rows[2], rows[0]))))
    g16 = test(rows[15])
    return w(g8, 8.0, 0.0) + w(g4, 4.0, 0.0) + w(g2, 2.0, 0.0) + w(g1, 1.0, 0.0) + w(g16, 1.0, 0.0)


def _peer_select_kernel(pq_ref, keys_ref, c1_ref, e1_ref, r2_ref, e2_ref, *, PH, HALF, TOPK):
    assert TOPK == 2 * SUBLANES
    tt = pq_ref.shape[1]
    sub = lax.broadcasted_iota(jnp.int32, (SUBLANES, tt), 0)

    def by_sublane(rows):
        out = rows[-1]
        for s in range(len(rows) - 2, -1, -1):
            out = jnp.where(sub == s, rows[s], out)
        return out

    for h in range(PH):
        q1 = pq_ref[h * 2 * HALF: h * 2 * HALF + HALF, :].astype(BF16)
        q2 = pq_ref[h * 2 * HALF + HALF:(h + 1) * 2 * HALF, :].astype(BF16)
        s1 = jnp.dot(keys_ref[h, 0], q1, preferred_element_type=F32)
        s2 = jnp.dot(keys_ref[h, 1], q2, preferred_element_type=F32)
        v1 = _top_sorted(s1, TOPK)
        v2 = _top_sorted(s2, TOPK)
        v2_lo, v2_hi = by_sublane(v2[:SUBLANES]), by_sublane(v2[SUBLANES:])
        cands = [v1[0] + v2_lo, v1[0] + v2_hi, v1[1] + v2_lo, v1[2] + v2_lo, v1[3] + v2_lo, v1[4] + v2_lo,
                 by_sublane(v1[SUBLANES:]) + v2[0],
                 jnp.where(sub < 6,
                           by_sublane([v1[5], v1[5], v1[6], v1[6], v1[7], v1[7], v1[7], v1[7]])
                           + jnp.where(sub % 2 == 0, v2[0], v2[1]), -jnp.inf)]
        top = _merge_sublanes_top(_sort_desc(cands), TOPK)
        tau = top[TOPK - 1]
        m = top[0]
        zsum = None
        for c in cands:
            z = jnp.where(c >= tau, jnp.exp(c - m), 0.0)
            zsum = z if zsum is None else zsum + z
        zsum = jnp.sum(zsum, axis=0, keepdims=True)
        inv_z = 1.0 / zsum
        for g16 in range(s1.shape[0] // ROWS_BF16):
            ranks, gates = [], []
            for g8 in (2 * g16, 2 * g16 + 1):
                rows = slice(g8 * SUBLANES, (g8 + 1) * SUBLANES)
                a, b2 = s1[rows, :], s2[rows, :]
                count1 = _prefix_count(v2, lambda probe: a + probe >= tau)
                rank2 = _prefix_count(v2, lambda probe: probe > b2)
                c1_ref[h, rows, :] = count1
                e1_ref[h, rows, :] = jnp.exp(a - v1[0])
                ranks.append(rank2)
                gates.append(jnp.exp(b2 - v2[0]) * inv_z)
            rows16 = slice(g16 * ROWS_BF16, (g16 + 1) * ROWS_BF16)
            r2_ref[h, rows16, :] = jnp.concatenate(ranks, axis=0).astype(BF16)
            e2_ref[h, rows16, :] = jnp.concatenate(gates, axis=0).astype(BF16)


def _peer_select(pq_t, keys_b, tt=256):
    t = pq_t.shape[1]
    ph, _, nk, half = keys_b.shape
    tt = _pick(t, tt, LANES)
    big = lambda: pl.BlockSpec((ph, nk, tt), lambda i: (0, 0, i))
    shp = lambda dt: jax.ShapeDtypeStruct((ph, nk, t), dt)
    return pl.pallas_call(
        functools.partial(_peer_select_kernel, PH=ph, HALF=half, TOPK=PEER_TOPK),
        grid=(t // tt,),
        in_specs=[pl.BlockSpec((ph * 2 * half, tt), lambda i: (0, i)),
                  pl.BlockSpec(keys_b.shape, lambda i: (0, 0, 0, 0))],
        out_specs=[big(), big(), big(), big()],
        out_shape=[shp(F32), shp(F32), shp(BF16), shp(BF16)],
        compiler_params=_params(("parallel",)),
        name="peer_select",
    )(pq_t, keys_b)


def _gelu(x):
    return 0.5 * x * (1.0 + lax.erf(x * (1.0 / math.sqrt(2.0))))


ROWS_BF16 = 16
GATE_ROWS = 128


def _peer_main_kernel(hn_ref, u_ref, vt_ref, c1_ref, e1_ref, r2_ref, e2_ref, out_ref,
                      ata_ref, atb_ref, cta_ref, ctb_ref, *, PH, NK, NI, NJ):
    s = pl.program_id(0)
    tt = ata_ref.shape[1]

    @pl.when(s == 0)
    def _init():
        for ref in (ata_ref, atb_ref, cta_ref, ctb_ref):
            ref[...] = jnp.zeros(ref.shape, ref.dtype)

    @pl.when(lax.rem(jnp.maximum(s - 2, 0), NJ) == 0)
    def _zero_out():
        out_ref[...] = jnp.zeros(out_ref.shape, F32)

    def step(at_new, at_prev, ct_new, ct_prev):
        d = out_ref.shape[0]
        blk = lax.rem(jnp.maximum(s - 1, 0), NJ)

        def mix_rows(m0, m1):
            out_ref[m0:m1, :] += jnp.dot(vt_ref[m0:m1, :], ct_prev[...], preferred_element_type=F32)

        def score_block(m0, m1, n0, n1):
            at_new[m0:m1, n0:n1] = jnp.dot(u_ref[m0:m1, :], hn_ref[:, n0:n1], preferred_element_type=F32)

        def gate_tile(ii, tc, r0):
            row = blk * NI + ii
            cols = slice(tc * LANES, (tc + 1) * LANES)
            groups = [slice(r0 + k * ROWS_BF16, r0 + (k + 1) * ROWS_BF16) for k in range(GATE_ROWS // ROWS_BF16)]
            g = [None] * len(groups)
            zero = jnp.zeros((ROWS_BF16, LANES), BF16)
            for h in range(PH):
                bcast = lambda r: jnp.broadcast_to(r[:, cols], (ROWS_BF16, LANES)).astype(BF16)
                c1b = bcast(c1_ref[h, pl.ds(row, 1), :])
                e1b = bcast(e1_ref[h, pl.ds(row, 1), :])
                for k, rws in enumerate(groups):
                    term = e1b * jnp.where(r2_ref[h, rws, cols] < c1b, e2_ref[h, rws, cols], zero)
                    g[k] = term if g[k] is None else g[k] + term
            for k, rws in enumerate(groups):
                orow = slice(ii * NK + rws.start, ii * NK + rws.stop)
                ct_new[orow, cols] = _gelu(at_prev[orow, cols]).astype(BF16) * g[k]

        tiles = [(ii, tc, r0) for ii in range(NI) for tc in range(tt // LANES) for r0 in range(0, NK, GATE_ROWS)]
        n_mix = 4
        n_sn = 1
        n_sm = 1
        ne = at_new.shape[0]
        mxu_items = [functools.partial(mix_rows, k * d // n_mix, (k + 1) * d // n_mix) for k in range(n_mix)]
        mxu_items += [functools.partial(score_block, km * ne // n_sm, (km + 1) * ne // n_sm,
                                        kn * tt // n_sn, (kn + 1) * tt // n_sn)
                      for kn in range(n_sn) for km in range(n_sm)]
        weights = [n_sn * n_sm] * n_mix + [n_mix] * (n_sn * n_sm)
        done, total = 0, sum(weights)
        for item, w in zip(mxu_items, weights):
            upto = len(tiles) * (done + w) // total
            for tile_args in tiles[len(tiles) * done // total: upto]:
                gate_tile(*tile_args)
            item()
            done += w

    @pl.when(s % 2 == 0)
    def _even():
        step(ata_ref, atb_ref, ctb_ref, cta_ref)

    @pl.when(s % 2 == 1)
    def _odd():
        step(atb_ref, ata_ref, cta_ref, ctb_ref)


def _peer_main(hn_t, u_b, vt_b, sel, tt=512, ne=512):
    d, t = hn_t.shape
    c1, e1, r2, e2 = sel
    ph, nk, _ = c1.shape
    nexp = u_b.shape[0]
    tt = _pick(t, tt, LANES)
    ne = _pick(nexp, ne, nk)
    ni = ne // nk
    nj = nexp // ne
    n_tiles = t // tt
    n_blocks = n_tiles * nj
    assert nj % 2 == 0
    once = dict(pipeline_mode=pl.Buffered(1))
    tile = lambda s, lag: jnp.minimum(jnp.maximum(s - lag, 0) // nj, n_tiles - 1)
    big = lambda: pl.BlockSpec((ph, nk, tt), lambda s: (0, 0, tile(s, 1)), **once)
    return pl.pallas_call(
        functools.partial(_peer_main_kernel, PH=ph, NK=nk, NI=ni, NJ=nj),
        grid=(n_blocks + 2,),
        in_specs=[pl.BlockSpec((d, tt), lambda s: (0, tile(s, 0)), **once),
                  pl.BlockSpec((ne, d), lambda s: (lax.rem(s, nj), 0)),
                  pl.BlockSpec((d, ne), lambda s: (0, lax.rem(jnp.maximum(s - 2, 0), nj))),
                  big(), big(), big(), big()],
        out_specs=pl.BlockSpec((d, tt), lambda s: (0, tile(s, 2))),
        out_shape=jax.ShapeDtypeStruct((d, t), F32),
        scratch_shapes=[pltpu.VMEM((ne, tt), F32), pltpu.VMEM((ne, tt), F32),
                        pltpu.VMEM((ne, tt), BF16), pltpu.VMEM((ne, tt), BF16)],
        compiler_params=_params(("arbitrary",)),
        name="peer_main",
    )(hn_t, u_b, vt_b, c1, e1, r2, e2)


def _tadd_kernel(x_ref, yt_ref, o_ref):
    o_ref[...] = x_ref[...] + yt_ref[...].T


def _transpose_add(x2d, yt, tm=256):
    t, d = x2d.shape
    tm = _pick(t, tm, LANES)
    return pl.pallas_call(
        _tadd_kernel,
        grid=(t // tm,),
        in_specs=[pl.BlockSpec((tm, d), lambda i: (i, 0)), pl.BlockSpec((d, tm), lambda i: (0, i))],
        out_specs=pl.BlockSpec((tm, d), lambda i: (i, 0)),
        out_shape=jax.ShapeDtypeStruct((t, d), F32),
        compiler_params=_params(("parallel",)),
        name="peer_residual",
    )(x2d, yt)


def _prepare(norm_mix, w_in, conv_w, conv_b, dt_bias, a_log, d_skip, ssd_norm,
             q_a_norm, w_q_up, kv_a_norm, w_kv_up, q_norm, k_norm, attn_out_norm,
             w_out, norm_ffn, peer_w_q, peer_keys, peer_u, peer_v, rope, N):
    d = w_in.shape[0]
    W = ssd_norm.shape[0]
    H = dt_bias.shape[0]
    P = W // H
    xbc = conv_w.shape[1]
    GN = (xbc - W) // 2
    G = GN // N
    QL = q_a_norm.shape[0]
    KVL = kv_a_norm.shape[0]
    MW = attn_out_norm.shape[0]
    MH = MW // MLA_V_DIM
    QK = MLA_NOPE + rope
    HL = _round_up(H, LANES)
    half = rope // 2
    assert 2 * rope == LANES and MH % 2 == 0 and P == SSD_HEAD_DIM

    s = np.cumsum([0, W, xbc, H, QL, KVL, rope])
    wz, wxbc, wdt, wcq, wckv, wkr = (w_in[:, s[i]:s[i + 1]] for i in range(6))
    wrot = jnp.concatenate([-wkr[:, half:], wkr[:, :half]], axis=1)
    pieces = [wz, wxbc, wcq, wckv, wkr, wkr, wrot, wrot, wdt]
    offs, o = {}, 0
    for name, width in (("z", W), ("xs", W), ("bm", GN), ("cm", GN), ("cq", QL), ("ckv", KVL),
                        ("krr", LANES), ("krot", LANES), ("dt", HL)):
        assert o % width == 0, (name, o, width)
        offs[name] = o
        o += width
    NP = _round_up(o, 1024)
    w_in_p = jnp.concatenate(pieces + [jnp.zeros((d, NP - (o - HL + H)), w_in.dtype)], axis=1).astype(BF16)

    wq3 = w_q_up.reshape(QL, MH, QK)
    wq_nope = wq3[:, :, :MLA_NOPE].reshape(QL, MH * MLA_NOPE)
    wq_rope = wq3[:, :, MLA_NOPE:]
    wq_rot = jnp.concatenate([-wq_rope[:, :, half:], wq_rope[:, :, :half]], axis=2)
    wq = jnp.concatenate([wq_nope, wq_rope.reshape(QL, MH * rope), wq_rot.reshape(QL, MH * rope)], axis=1).astype(BF16)

    wkv3 = w_kv_up.reshape(KVL, MH, MLA_NOPE + MLA_V_DIM)
    wkv = jnp.concatenate([wkv3[:, :, :MLA_NOPE].reshape(KVL, MH * MLA_NOPE),
                           wkv3[:, :, MLA_NOPE:].reshape(KVL, MH * MLA_V_DIM)], axis=1).astype(BF16)

    scale = QK ** -0.5
    zeros_h = jnp.zeros((rope,), F32)

    def gains(g, sc):
        gn = (g[:MLA_NOPE] * sc).reshape(1, LANES)
        gr = jnp.stack([jnp.concatenate([g[MLA_NOPE:] * sc, zeros_h]), jnp.concatenate([zeros_h, g[MLA_NOPE:] * sc])])
        return gn.astype(F32), gr.astype(F32)

    q_gn, q_gr = gains(q_norm, scale)
    k_gn, k_gr = gains(k_norm, 1.0)

    pad_h = lambda v: jnp.concatenate([v.astype(F32), jnp.zeros((HL - H,), F32)]).reshape(1, HL)
    head_of_col = np.arange(W) // P
    expand = (np.arange(HL)[:, None] == head_of_col[None, :]).astype(np.float32)

    return dict(
        H=H, P=P, G=G, N=N, HL=HL, QL=QL, KVL=KVL, MH=MH, QK=QK, ROPE=rope, NP=NP, offs=offs,
        norm_mix=norm_mix, w_in=w_in_p,
        cwx=conv_w[:, :W], cwb=conv_w[:, W:W + GN], cwc=conv_w[:, W + GN:],
        cbx=conv_b[:W].reshape(1, W), cbb=conv_b[W:W + GN].reshape(1, GN), cbc=conv_b[W + GN:].reshape(1, GN),
        dtb=pad_h(dt_bias), alog=pad_h(a_log),
        dskip_x=jnp.repeat(d_skip.astype(F32), P).reshape(1, W), ssd_gn=ssd_norm.reshape(1, W).astype(F32),
        expand=jnp.asarray(expand, BF16),
        q_a_norm=q_a_norm.reshape(1, QL).astype(F32), kv_a_norm=kv_a_norm.reshape(1, KVL).astype(F32),
        wq=wq, wkv=wkv, q_gn=q_gn, q_gr=q_gr, k_gn=k_gn, k_gr=k_gr,
        attn_out_norm=attn_out_norm, w_out_a=w_out[:W].astype(BF16), w_out_b=w_out[W:].astype(BF16),
        norm_ffn=norm_ffn, peer_w_q_t=peer_w_q.T.astype(BF16), peer_keys=peer_keys.astype(BF16),
        peer_u=peer_u.astype(BF16), peer_vt=peer_v.T.astype(BF16),
    )


def _rope_tables(pos, rope):
    inv = 1.0 / (ROPE_BASE ** (jnp.arange(0, rope, 2, dtype=F32) / rope))
    ang = pos.astype(F32)[:, None] * inv[None, :]
    reps = LANES // (rope // 2)
    return jnp.tile(jnp.cos(ang), (1, reps)), jnp.tile(jnp.sin(ang), (1, reps))


def _layer(x, pos, past_ckv, past_krope, conv_hist, ssm_h0, prm):
    b, l, d = x.shape
    t = b * l
    H, P, G, N = prm["H"], prm["P"], prm["G"], prm["N"]
    W, GN = H * P, G * N
    offs = prm["offs"]
    x2d = x.reshape(t, d)

    h = _rms_cast(x2d, prm["norm_mix"])
    proj = _matmul([(h, prm["w_in"])], tm=1024, tn=1024, name="in_proj")
    proj3 = proj.reshape(b, l, prm["NP"])

    hist = (conv_hist[:, :, :W], conv_hist[:, :, W:W + GN], conv_hist[:, :, W + GN:])
    y_ssd, h_last, nx, nb, ncm = _ssd(proj3, offs, hist, ssm_h0.reshape(b, W, N), prm, _pick(l, SSD_CHUNK, 8))
    conv_new = jnp.concatenate([nx, nb, ncm], axis=-1)
    h_last = h_last.reshape(b, H, P, N)

    cos4, sin4 = _rope_tables(pos, prm["ROPE"])
    c_new, kr2_new, kr_new = _latent(proj3, offs, prm, cos4, sin4)
    q = _q_proj(proj3, offs, prm, cos4, sin4)
    if past_ckv is None:
        c_all, kr2_all = c_new, kr2_new
    else:
        c_all = jnp.concatenate([past_ckv, c_new], axis=1)
        kr2_all = jnp.concatenate([jnp.concatenate([past_krope, past_krope], axis=-1), kr2_new], axis=1)
    lk = c_all.shape[1]
    k, v = _kv_proj(c_all, kr2_all, prm, tm=_pick(lk, 512, 16))
    if past_ckv is None:
        o = _attention(q, k, v, prm["MH"], True, _pick(l, 256, CHUNK))
    else:
        assert l <= CHUNK and (lk - l) % CHUNK == 0
        o = _attention(q, k, v, prm["MH"], False, l)
    o_n = _rms_cast(o.reshape(t, -1), prm["attn_out_norm"])

    x_mid = _matmul([(y_ssd.reshape(t, W), prm["w_out_a"]), (o_n, prm["w_out_b"])], res=x2d,
                    tm=1024, tn=1024, name="out_proj")

    hn_t = _rms_cast(x_mid, prm["norm_ffn"], transpose=True)
    pq_t = _matmul([(prm["peer_w_q_t"], hn_t)], tm=1024, tn=512, name="peer_query")
    sel = _peer_select(pq_t, prm["peer_keys"])
    yt = _peer_main(hn_t, prm["peer_u"], prm["peer_vt"], sel)
    out = _transpose_add(x_mid, yt)
    return out.reshape(b, l, d), c_new, kr_new, h_last.astype(ssm_h0.dtype), conv_new


def kernel(x_prompt, x_sample, cache_mla_ckv, cache_mla_krope, state_ssm, state_conv, norm_mix, w_in, conv_w, conv_b, dt_bias, a_log, d_skip, ssd_norm, q_a_norm, w_q_up, kv_a_norm, w_kv_up, q_norm, k_norm, attn_out_norm, w_out, norm_ffn, peer_w_q, peer_keys, peer_u, peer_v):
    bp, lp, _ = x_prompt.shape
    ls = x_sample.shape[1]
    depth = norm_mix.shape[0]
    past = cache_mla_ckv.shape[2]
    rope = cache_mla_krope.shape[-1]
    nstate = state_ssm.shape[-1]
    xbc = state_conv.shape[-1]
    pos_p = jnp.arange(lp, dtype=jnp.int32)
    pos_s = past + jnp.arange(ls, dtype=jnp.int32)
    yp, ys = x_prompt, x_sample
    outs_p, outs_s = [], []
    for layer in range(depth):
        prm = _prepare(norm_mix[layer], w_in[layer], conv_w[layer], conv_b[layer], dt_bias[layer], a_log[layer],
                       d_skip[layer], ssd_norm[layer], q_a_norm[layer], w_q_up[layer], kv_a_norm[layer],
                       w_kv_up[layer], q_norm[layer], k_norm[layer], attn_out_norm[layer], w_out[layer],
                       norm_ffn[layer], peer_w_q[layer], peer_keys[layer], peer_u[layer], peer_v[layer],
                       rope, nstate)
        zero_conv = jnp.zeros((bp, state_conv.shape[2], xbc), x_prompt.dtype)
        zero_ssm = jnp.zeros((bp,) + state_ssm.shape[2:], state_ssm.dtype)
        yp, *rest_p = _layer(yp, pos_p, None, None, zero_conv, zero_ssm, prm)
        outs_p.append(rest_p)
        ys, *rest_s = _layer(ys, pos_s, cache_mla_ckv[layer], cache_mla_krope[layer],
                             state_conv[layer], state_ssm[layer], prm)
        outs_s.append(rest_s)
    stack = lambda outs, i: jnp.stack([o[i] for o in outs])
    return (yp, ys,
            stack(outs_p, 0), stack(outs_p, 1), stack(outs_p, 2), stack(outs_p, 3),
            stack(outs_s, 0), stack(outs_s, 1), stack(outs_s, 2), stack(outs_s, 3))
```

```python
import functools
import math

import numpy as np
import jax
import jax.numpy as jnp
from jax import lax
from jax.experimental import pallas as pl
from jax.experimental.pallas import tpu as pltpu

F32 = jnp.float32
BF16 = jnp.bfloat16

EPS = 1e-6
CHUNK = 64
SSD_CHUNK = 128
SSD_HEAD_DIM = 64
MLA_NOPE = 128
MLA_V_DIM = 128
ROPE_BASE = 10000.0
PEER_TOPK = 16
LANES = 128
SUBLANES = 8
VMEM_LIMIT = 56 * 1024 * 1024

TM_MATMUL, TN_MATMUL = 1024, 1024
TN_PEER_QUERY = 1024
TM_ROWS = 256
TM_KV = 512
TQ_ATTN = 256
TT_SELECT = 256
TT_PEER, NE_PEER = 512, 512

NT_DIMS = (((1,), (1,)), ((), ()))
TN_DIMS = (((0,), (0,)), ((), ()))


def _pick(n, pref, mult):
    if n <= pref:
        return n
    t = (pref // mult) * mult
    while t >= mult:
        if n % t == 0:
            return t
        t -= mult
    return n


def _round_up(n, m):
    return (n + m - 1) // m * m


def _params(sem, vmem=VMEM_LIMIT):
    return pltpu.CompilerParams(dimension_semantics=sem, vmem_limit_bytes=vmem)


def _split3(x):
    hi = x.astype(BF16)
    r1 = x - hi.astype(F32)
    mid = r1.astype(BF16)
    lo = (r1 - mid.astype(F32)).astype(BF16)
    return hi, mid, lo


def _dot3(a01, x, dims=None):
    out = None
    for piece in _split3(x):
        if dims is None:
            d = jnp.dot(a01, piece, preferred_element_type=F32)
        else:
            d = lax.dot_general(a01, piece, dims, preferred_element_type=F32)
        out = d if out is None else out + d
    return out


def _dot3_right(x, b01, dims=None):
    out = None
    for piece in _split3(x):
        if dims is None:
            d = jnp.dot(piece, b01, preferred_element_type=F32)
        else:
            d = lax.dot_general(piece, b01, dims, preferred_element_type=F32)
        out = d if out is None else out + d
    return out


def _rms_cast_kernel(x_ref, g_ref, o_ref):
    x = x_ref[...].astype(F32)
    ms = jnp.mean(x * x, axis=-1, keepdims=True)
    o_ref[...] = (x * lax.rsqrt(ms + EPS) * g_ref[...]).astype(o_ref.dtype)


def _rms_cast_t_kernel(x_ref, g_ref, o_ref):
    x = x_ref[...].astype(F32)
    ms = jnp.mean(x * x, axis=-1, keepdims=True)
    o_ref[...] = (x * lax.rsqrt(ms + EPS) * g_ref[...]).T.astype(o_ref.dtype)


def _rms_cast(x2d, g, out_dtype=BF16, tm=TM_ROWS, transpose=False):
    t, d = x2d.shape
    tm = _pick(t, tm, LANES if transpose else 16)
    if transpose:
        body, out_spec, out_shape = _rms_cast_t_kernel, pl.BlockSpec((d, tm), lambda i: (0, i)), (d, t)
    else:
        body, out_spec, out_shape = _rms_cast_kernel, pl.BlockSpec((tm, d), lambda i: (i, 0)), (t, d)
    return pl.pallas_call(
        body,
        grid=(t // tm,),
        in_specs=[pl.BlockSpec((tm, d), lambda i: (i, 0)), pl.BlockSpec((1, d), lambda i: (0, 0))],
        out_specs=out_spec,
        out_shape=jax.ShapeDtypeStruct(out_shape, out_dtype),
        compiler_params=_params(("parallel",)),
        name="rms_cast_t" if transpose else "rms_cast",
    )(x2d, g.reshape(1, d).astype(F32))


def _mm_kernel(*refs, n_pairs, has_res):
    o_ref = refs[-1]
    acc = None
    for p in range(n_pairs):
        d = jnp.dot(refs[2 * p][...], refs[2 * p + 1][...], preferred_element_type=F32)
        acc = d if acc is None else acc + d
    if has_res:
        acc = refs[2 * n_pairs][...] + acc
    o_ref[...] = acc.astype(o_ref.dtype)


def _matmul(pairs, res=None, out_dtype=F32, tm=TM_MATMUL, tn=TN_MATMUL, name="matmul"):
    m = pairs[0][0].shape[0]
    n = pairs[0][1].shape[1]
    tm = _pick(m, tm, 16)
    tn = _pick(n, tn, LANES)
    in_specs, args = [], []
    for a, w in pairs:
        k = a.shape[1]
        in_specs += [pl.BlockSpec((tm, k), lambda i, j: (i, 0)), pl.BlockSpec((k, tn), lambda i, j: (0, j))]
        args += [a, w]
    if res is not None:
        in_specs.append(pl.BlockSpec((tm, tn), lambda i, j: (i, j)))
        args.append(res)
    return pl.pallas_call(
        functools.partial(_mm_kernel, n_pairs=len(pairs), has_res=res is not None),
        grid=(m // tm, n // tn),
        in_specs=in_specs,
        out_specs=pl.BlockSpec((tm, tn), lambda i, j: (i, j)),
        out_shape=jax.ShapeDtypeStruct((m, n), out_dtype),
        compiler_params=_params(("parallel", "arbitrary")),
        name=name,
    )(*args)


def _silu(x):
    hx = 0.5 * x
    return hx + hx * jnp.tanh(hx)


def _softplus(x):
    return jnp.maximum(x, 0.0) + jnp.log1p(jnp.exp(-jnp.abs(x)))


def _ssd_kernel(z_ref, xs_ref, bm_ref, cm_ref, dt_ref,
                hx_ref, hb_ref, hc_ref, h0_ref,
                cwx_ref, cwb_ref, cwc_ref, cbx_ref, cbb_ref, cbc_ref,
                dtb_ref, alog_ref, dskip_ref, gn_ref,
                tri_ref, exp_ref,
                y_ref, hlast_ref, nx_ref, nb_ref, nc_ref,
                bx_ref, bb_ref, bc_ref, state_ref, ydiag_ref,
                *, Q, H, P, G, N):
    c = pl.program_id(1)
    nc = pl.num_programs(1)
    R = H // G
    W = H * P

    @pl.when(c == 0)
    def _init():
        for buf, hist in ((bx_ref, hx_ref), (bb_ref, hb_ref), (bc_ref, hc_ref)):
            buf[0:8, :] = jnp.zeros((8, buf.shape[1]), F32)
            buf[5:8, :] = hist[0]
        state_ref[...] = h0_ref[0]

    def conv(buf, x_ref, cw_ref, cb_ref):
        x = x_ref[0]
        buf[8:8 + Q, :] = x
        win = buf[...]
        acc = None
        for k in range(3):
            tap = pltpu.roll(win, Q + 3 - k, 0)[0:Q, :] * cw_ref[k:k + 1, :]
            acc = tap if acc is None else acc + tap
        acc = acc + x * cw_ref[3:4, :]
        buf[0:8, :] = buf[Q:Q + 8, :]
        return _silu(cb_ref[...] + acc)

    xs = conv(bx_ref, xs_ref, cwx_ref, cbx_ref)
    bm = conv(bb_ref, bm_ref, cwb_ref, cbb_ref)
    cm = conv(bc_ref, cm_ref, cwc_ref, cbc_ref)

    @pl.when(c == nc - 1)
    def _tail():
        nx_ref[0] = bx_ref[5:8, :]
        nb_ref[0] = bb_ref[5:8, :]
        nc_ref[0] = bc_ref[5:8, :]

    dt = _softplus(dt_ref[0] + dtb_ref[...])
    a_neg = -jnp.exp(alog_ref[...])
    dta = dt * a_neg
    tri = tri_ref[...]
    acum = _dot3(tri, dta)
    eye = (lax.broadcasted_iota(jnp.int32, (Q, Q), 0) == lax.broadcasted_iota(jnp.int32, (Q, Q), 1)).astype(BF16)
    acum_t = _dot3_right(acum, eye, TN_DIMS)
    dt_t = _dot3_right(dt, eye, TN_DIMS)

    acum_x = _dot3_right(acum, exp_ref[...])
    dt_x = _dot3_right(dt, exp_ref[...])

    ii = lax.broadcasted_iota(jnp.int32, (Q, Q), 0)
    jj = lax.broadcasted_iota(jnp.int32, (Q, Q), 1)
    causal = ii >= jj
    xs_b = xs.astype(BF16)
    bm_b = bm.astype(BF16)
    cm_b = cm.astype(BF16)
    st_b = state_ref[...].astype(BF16)

    exp_acum_x = jnp.exp(acum_x)
    decay_x = jnp.exp(acum_x[Q - 1:Q, :] - acum_x) * dt_x
    xd_b = (xs * decay_x).astype(BF16)

    chunk_decay = jnp.exp(acum_t[:, Q - 1:Q])

    for g in range(G):
        cg = cm_b[:, g * N:(g + 1) * N]
        bg = bm_b[:, g * N:(g + 1) * N]
        cb = lax.dot_general(cg, bg, NT_DIMS, preferred_element_type=F32)
        for r in range(R):
            h = g * R + r
            dm = acum[:, h:h + 1] - acum_t[h:h + 1, :]
            lm = jnp.where(causal, jnp.exp(dm), 0.0)
            wm = (cb * lm * dt_t[h:h + 1, :]).astype(BF16)
            ydiag_ref[:, h * P:(h + 1) * P] = jnp.dot(wm, xs_b[:, h * P:(h + 1) * P],
                                                      preferred_element_type=F32)
        rows = slice(g * R * P, (g + 1) * R * P)
        y_off = lax.dot_general(cg, st_b[rows, :], NT_DIMS, preferred_element_type=F32)
        ydiag_ref[:, rows] = ydiag_ref[:, rows] + y_off * exp_acum_x[:, rows]
        s_new = lax.dot_general(xd_b[:, rows], bg, TN_DIMS, preferred_element_type=F32)
        for r in range(R):
            h = g * R + r
            hr = slice(h * P, (h + 1) * P)
            state_ref[hr, :] = chunk_decay[h:h + 1, :] * state_ref[hr, :] + s_new[r * P:(r + 1) * P, :]

    y = ydiag_ref[...] + dskip_ref[...] * xs
    yg = y * _silu(z_ref[0])
    ms = jnp.mean(yg * yg, axis=-1, keepdims=True)
    y_ref[0] = (yg * lax.rsqrt(ms + EPS) * gn_ref[...]).astype(y_ref.dtype)

    @pl.when(c == nc - 1)
    def _final():
        hlast_ref[0] = state_ref[...]


def _ssd(proj3, offs, hist, h0, prm, Q):
    b, l, _ = proj3.shape
    H, P, G, N = prm["H"], prm["P"], prm["G"], prm["N"]
    W, GN, HL = H * P, G * N, prm["HL"]
    nchunk = l // Q
    hx, hb, hc = hist
    full = lambda shape: pl.BlockSpec(shape, lambda i, j: (0,) * len(shape))
    per_b = lambda shape: pl.BlockSpec(shape, lambda i, j: (i,) + (0,) * (len(shape) - 1))
    col = lambda width, off: pl.BlockSpec((1, Q, width), lambda i, j, o=off // width: (i, j, o))
    in_specs = [
        col(W, offs["z"]), col(W, offs["xs"]), col(GN, offs["bm"]), col(GN, offs["cm"]), col(HL, offs["dt"]),
        per_b((1, 3, W)), per_b((1, 3, GN)), per_b((1, 3, GN)), per_b((1, W, N)),
        full((4, W)), full((4, GN)), full((4, GN)), full((1, W)), full((1, GN)), full((1, GN)),
        full((1, HL)), full((1, HL)), full((1, W)), full((1, W)),
        full((Q, Q)), full((HL, W)),
    ]
    out_specs = [
        pl.BlockSpec((1, Q, W), lambda i, j: (i, j, 0)),
        per_b((1, W, N)), per_b((1, 3, W)), per_b((1, 3, GN)), per_b((1, 3, GN)),
    ]
    out_shape = [
        jax.ShapeDtypeStruct((b, l, W), BF16),
        jax.ShapeDtypeStruct((b, W, N), F32),
        jax.ShapeDtypeStruct((b, 3, W), F32),
        jax.ShapeDtypeStruct((b, 3, GN), F32),
        jax.ShapeDtypeStruct((b, 3, GN), F32),
    ]
    scratch = [
        pltpu.VMEM((Q + 8, W), F32), pltpu.VMEM((Q + 8, GN), F32), pltpu.VMEM((Q + 8, GN), F32),
        pltpu.VMEM((W, N), F32), pltpu.VMEM((Q, W), F32),
    ]
    return pl.pallas_call(
        functools.partial(_ssd_kernel, Q=Q, H=H, P=P, G=G, N=N),
        grid=(b, nchunk),
        in_specs=in_specs, out_specs=out_specs, out_shape=out_shape, scratch_shapes=scratch,
        compiler_params=_params(("parallel", "arbitrary")),
        name="ssd_scan",
    )(proj3, proj3, proj3, proj3, proj3, hx, hb, hc, h0,
      prm["cwx"], prm["cwb"], prm["cwc"], prm["cbx"], prm["cbb"], prm["cbc"],
      prm["dtb"], prm["alog"], prm["dskip_x"], prm["ssd_gn"],
      jnp.asarray(np.tril(np.ones((Q, Q), np.float32)), BF16), prm["expand"])


def _latent_kernel(ckv_ref, krr_ref, krot_ref, g_ref, cos_ref, sin_ref, c_ref, kr2_ref, kr_ref, *, rope):
    x = ckv_ref[0]
    ms = jnp.mean(x * x, axis=-1, keepdims=True)
    c_ref[0] = x * lax.rsqrt(ms + EPS) * g_ref[...]
    kr2 = krr_ref[0] * cos_ref[...] + krot_ref[0] * sin_ref[...]
    kr2_ref[0] = kr2
    kr_ref[0] = kr2[:, :rope]


def _latent(proj3, offs, prm, cos4, sin4, tm=TM_ROWS):
    b, l, _ = proj3.shape
    kvl, rope = prm["KVL"], prm["ROPE"]
    tm = _pick(l, tm, 8)
    col = lambda width, off: pl.BlockSpec((1, tm, width), lambda i, j, o=off // width: (i, j, o))
    return pl.pallas_call(
        functools.partial(_latent_kernel, rope=rope),
        grid=(b, l // tm),
        in_specs=[col(kvl, offs["ckv"]), col(LANES, offs["krr"]), col(LANES, offs["krot"]),
                  pl.BlockSpec((1, kvl), lambda i, j: (0, 0)),
                  pl.BlockSpec((tm, LANES), lambda i, j: (j, 0)), pl.BlockSpec((tm, LANES), lambda i, j: (j, 0))],
        out_specs=[pl.BlockSpec((1, tm, kvl), lambda i, j: (i, j, 0)),
                   pl.BlockSpec((1, tm, LANES), lambda i, j: (i, j, 0)),
                   pl.BlockSpec((1, tm, rope), lambda i, j: (i, j, 0))],
        out_shape=[jax.ShapeDtypeStruct((b, l, kvl), F32), jax.ShapeDtypeStruct((b, l, LANES), F32),
                   jax.ShapeDtypeStruct((b, l, rope), F32)],
        compiler_params=_params(("parallel", "parallel")),
        name="mla_latent",
    )(proj3, proj3, proj3, prm["kv_a_norm"], cos4, sin4)


def _q_kernel(cq_ref, g_ref, w_ref, cos_ref, sin_ref, gn_ref, gr_ref, q_ref, *, MH, qk_dim):
    x = cq_ref[0]
    ms = jnp.mean(x * x, axis=-1, keepdims=True)
    xn = (x * lax.rsqrt(ms + EPS) * g_ref[...]).astype(BF16)
    qf = jnp.dot(xn, w_ref[...], preferred_element_type=F32)
    cos = cos_ref[...]
    sin = sin_ref[...]
    lane = lax.broadcasted_iota(jnp.int32, (1, LANES), 1)
    half_mask = [(lane < LANES // 2).astype(F32), (lane >= LANES // 2).astype(F32)]
    rope0 = MH * MLA_NOPE
    rot0 = rope0 + MH * (LANES // 2)
    for p in range(MH // 2):
        rp = (qf[:, rope0 + p * LANES: rope0 + (p + 1) * LANES] * cos
              + qf[:, rot0 + p * LANES: rot0 + (p + 1) * LANES] * sin)
        for e in (0, 1):
            h = 2 * p + e
            nope = qf[:, h * MLA_NOPE:(h + 1) * MLA_NOPE]
            rh = rp * half_mask[e]
            ssq = jnp.sum(nope * nope, axis=-1, keepdims=True) + jnp.sum(rh * rh, axis=-1, keepdims=True)
            inv = lax.rsqrt(ssq * (1.0 / qk_dim) + EPS)
            q_ref[0, :, h * 2 * LANES: h * 2 * LANES + LANES] = (nope * inv * gn_ref[...]).astype(q_ref.dtype)
            q_ref[0, :, h * 2 * LANES + LANES:(h + 1) * 2 * LANES] = (rh * inv * gr_ref[e:e + 1, :]).astype(q_ref.dtype)


def _q_proj(proj3, offs, prm, cos4, sin4, tm=TM_ROWS):
    b, l, _ = proj3.shape
    ql, mh = prm["QL"], prm["MH"]
    tm = _pick(l, tm, 16)
    wq = prm["wq"]
    return pl.pallas_call(
        functools.partial(_q_kernel, MH=mh, qk_dim=prm["QK"]),
        grid=(b, l // tm),
        in_specs=[pl.BlockSpec((1, tm, ql), lambda i, j, o=offs["cq"] // ql: (i, j, o)),
                  pl.BlockSpec((1, ql), lambda i, j: (0, 0)),
                  pl.BlockSpec(wq.shape, lambda i, j: (0, 0)),
                  pl.BlockSpec((tm, LANES), lambda i, j: (j, 0)), pl.BlockSpec((tm, LANES), lambda i, j: (j, 0)),
                  pl.BlockSpec((1, LANES), lambda i, j: (0, 0)), pl.BlockSpec((2, LANES), lambda i, j: (0, 0))],
        out_specs=pl.BlockSpec((1, tm, mh * 2 * LANES), lambda i, j: (i, j, 0)),
        out_shape=jax.ShapeDtypeStruct((b, l, mh * 2 * LANES), BF16),
        compiler_params=_params(("parallel", "parallel")),
        name="mla_q",
    )(proj3, prm["q_a_norm"], wq, cos4, sin4, prm["q_gn"], prm["q_gr"])


def _kv_kernel(c_ref, kr2_ref, w_ref, gn_ref, gr_ref, k_ref, v_ref, *, MH, qk_dim):
    c = c_ref[0].astype(BF16)
    kv = jnp.dot(c, w_ref[...], preferred_element_type=F32)
    kr2 = kr2_ref[0]
    lane = lax.broadcasted_iota(jnp.int32, (1, LANES), 1)
    kr_lo = kr2 * (lane < LANES // 2).astype(F32)
    kr_ss = jnp.sum(kr_lo * kr_lo, axis=-1, keepdims=True)
    for h in range(MH):
        nope = kv[:, h * MLA_NOPE:(h + 1) * MLA_NOPE]
        ssq = jnp.sum(nope * nope, axis=-1, keepdims=True) + kr_ss
        inv = lax.rsqrt(ssq * (1.0 / qk_dim) + EPS)
        k_ref[0, :, h * 2 * LANES: h * 2 * LANES + LANES] = (nope * inv * gn_ref[...]).astype(k_ref.dtype)
        k_ref[0, :, h * 2 * LANES + LANES:(h + 1) * 2 * LANES] = (kr2 * inv * gr_ref[h % 2:h % 2 + 1, :]).astype(k_ref.dtype)
    v_ref[0] = kv[:, MH * MLA_NOPE:].astype(v_ref.dtype)


def _kv_proj(c_all, kr2_all, prm, tm=TM_KV):
    b, lk, kvl = c_all.shape
    mh = prm["MH"]
    tm = _pick(lk, tm, 16)
    wkv = prm["wkv"]
    return pl.pallas_call(
        functools.partial(_kv_kernel, MH=mh, qk_dim=prm["QK"]),
        grid=(b, lk // tm),
        in_specs=[pl.BlockSpec((1, tm, kvl), lambda i, j: (i, j, 0)),
                  pl.BlockSpec((1, tm, LANES), lambda i, j: (i, j, 0)),
                  pl.BlockSpec(wkv.shape, lambda i, j: (0, 0)),
                  pl.BlockSpec((1, LANES), lambda i, j: (0, 0)), pl.BlockSpec((2, LANES), lambda i, j: (0, 0))],
        out_specs=[pl.BlockSpec((1, tm, mh * 2 * LANES), lambda i, j: (i, j, 0)),
                   pl.BlockSpec((1, tm, mh * MLA_V_DIM), lambda i, j: (i, j, 0))],
        out_shape=[jax.ShapeDtypeStruct((b, lk, mh * 2 * LANES), BF16),
                   jax.ShapeDtypeStruct((b, lk, mh * MLA_V_DIM), BF16)],
        compiler_params=_params(("parallel", "parallel")),
        name="mla_kv",
    )(c_all, kr2_all, wkv, prm["k_gn"], prm["k_gr"])


def _attn_causal_kernel(q_ref, k_ref, v_ref, o_ref, *, tq, lq):
    for qi in range(lq // tq):
        rows = slice(qi * tq, (qi + 1) * tq)
        q = q_ref[0, rows, :]
        past = qi * tq
        sd = lax.dot_general(q, k_ref[0, past:past + tq, :], NT_DIMS, preferred_element_type=F32)
        ri = lax.broadcasted_iota(jnp.int32, (tq, tq), 0) // CHUNK
        ci = lax.broadcasted_iota(jnp.int32, (tq, tq), 1) // CHUNK
        sd = jnp.where(ri >= ci, sd, -jnp.inf)
        m = jnp.max(sd, axis=-1, keepdims=True)
        if past > 0:
            sp = lax.dot_general(q, k_ref[0, 0:past, :], NT_DIMS, preferred_element_type=F32)
            m = jnp.maximum(m, jnp.max(sp, axis=-1, keepdims=True))
            pp = jnp.exp(sp - m)
            l = jnp.sum(pp, axis=-1, keepdims=True)
            acc = jnp.dot(pp.astype(BF16), v_ref[0, 0:past, :], preferred_element_type=F32)
        pd = jnp.exp(sd - m)
        ld = jnp.sum(pd, axis=-1, keepdims=True)
        accd = jnp.dot(pd.astype(BF16), v_ref[0, past:past + tq, :], preferred_element_type=F32)
        if past > 0:
            l, acc = l + ld, acc + accd
        else:
            l, acc = ld, accd
        o_ref[0, rows, :] = acc / l


def _attn_cached_kernel(q_ref, kc_ref, vc_ref, kn_ref, vn_ref, o_ref):
    q = q_ref[0]
    sc = lax.dot_general(q, kc_ref[0], NT_DIMS, preferred_element_type=F32)
    sn = lax.dot_general(q, kn_ref[0], NT_DIMS, preferred_element_type=F32)
    m = jnp.maximum(jnp.max(sc, axis=-1, keepdims=True), jnp.max(sn, axis=-1, keepdims=True))
    pc = jnp.exp(sc - m)
    pn = jnp.exp(sn - m)
    l = jnp.sum(pc, axis=-1, keepdims=True) + jnp.sum(pn, axis=-1, keepdims=True)
    acc = (jnp.dot(pc.astype(BF16), vc_ref[0], preferred_element_type=F32)
           + jnp.dot(pn.astype(BF16), vn_ref[0], preferred_element_type=F32))
    o_ref[0] = acc / l


def _attention(q, kvs, mh, tq=None):
    b, lq, _ = q.shape
    head = lambda rows, width: pl.BlockSpec((1, rows, width), lambda bi, h: (bi, 0, h))
    in_specs, args = [head(lq, 2 * LANES)], [q]
    for k, v in kvs:
        in_specs += [head(k.shape[1], 2 * LANES), head(v.shape[1], MLA_V_DIM)]
        args += [k, v]
    body = functools.partial(_attn_causal_kernel, tq=tq, lq=lq) if len(kvs) == 1 else _attn_cached_kernel
    return pl.pallas_call(
        body,
        grid=(b, mh),
        in_specs=in_specs,
        out_specs=head(lq, MLA_V_DIM),
        out_shape=jax.ShapeDtypeStruct((b, lq, mh * MLA_V_DIM), F32),
        compiler_params=_params(("parallel", "parallel")),
        name="mla_attention",
    )(*args)


def _oddeven_sort_pairs(lo, n):
    def merge(lo, n, r):
        step = r * 2
        if step < n:
            yield from merge(lo, n, step)
            yield from merge(lo + r, n, step)
            for i in range(lo + r, lo + n - r, step):
                yield (i, i + r)
        else:
            yield (lo, lo + r)
    if n > 1:
        m = n // 2
        yield from _oddeven_sort_pairs(lo, m)
        yield from _oddeven_sort_pairs(lo + m, m)
        yield from merge(lo, n, 1)


def _sort_desc(xs):
    xs = list(xs)
    for i, j in _oddeven_sort_pairs(0, len(xs)):
        xs[i], xs[j] = jnp.maximum(xs[i], xs[j]), jnp.minimum(xs[i], xs[j])
    return xs


def _bitonic_merge_desc(xs):
    xs = list(xs)
    n = len(xs)
    d = n // 2
    while d >= 1:
        for i in range(n):
            if i & d == 0:
                xs[i], xs[i + d] = jnp.maximum(xs[i], xs[i + d]), jnp.minimum(xs[i], xs[i + d])
        d //= 2
    return xs


def _merge_sublanes_top(xs, k):
    shift = SUBLANES // 2
    while shift >= 1:
        other = [pltpu.roll(x, shift, 0) for x in xs]
        if len(xs) < k:
            xs = _bitonic_merge_desc(xs + other[::-1])
        else:
            xs = _bitonic_merge_desc([jnp.maximum(xs[i], other[k - 1 - i]) for i in range(k)])
        shift //= 2
    return xs


def _top_sorted(x, k):
    groups = [x[i * SUBLANES:(i + 1) * SUBLANES, :] for i in range(x.shape[0] // SUBLANES)]
    return _merge_sublanes_top(_sort_desc(groups)[:k], k)


def _prefix_count(rows, test):
    w = jnp.where
    g8 = test(rows[7])
    g4 = test(w(g8, rows[11], rows[3]))
    g2 = test(w(g8, w(g4, rows[13], rows[9]), w(g4, rows[5], rows[1])))
    g1 = test(w(g8, w(g4, w(g2, rows[14], rows[12]), w(g2, rows[10], rows[8])),
                w(g4, w(g2, rows[6], rows[4]), w(g2, rows[2], rows[0]))))
    g16 = test(rows[15])
    return w(g8, 8.0, 0.0) + w(g4, 4.0, 0.0) + w(g2, 2.0, 0.0) + w(g1, 1.0, 0.0) + w(g16, 1.0, 0.0)


def _peer_select_kernel(pq_ref, keys_ref, c1_ref, e1_ref, r2_ref, e2_ref, *, PH, HALF, TOPK):
    assert TOPK == 2 * SUBLANES
    tt = pq_ref.shape[1]
    sub = lax.broadcasted_iota(jnp.int32, (SUBLANES, tt), 0)

    def by_sublane(rows):
        out = rows[-1]
        for s in range(len(rows) - 2, -1, -1):
            out = jnp.where(sub == s, rows[s], out)
        return out

    for h in range(PH):
        q1 = pq_ref[h * 2 * HALF: h * 2 * HALF + HALF, :].astype(BF16)
        q2 = pq_ref[h * 2 * HALF + HALF:(h + 1) * 2 * HALF, :].astype(BF16)
        s1 = jnp.dot(keys_ref[h, 0], q1, preferred_element_type=F32)
        s2 = jnp.dot(keys_ref[h, 1], q2, preferred_element_type=F32)
        v1 = _top_sorted(s1, TOPK)
        v2 = _top_sorted(s2, TOPK)
        v2_lo, v2_hi = by_sublane(v2[:SUBLANES]), by_sublane(v2[SUBLANES:])
        cands = [v1[0] + v2_lo, v1[0] + v2_hi, v1[1] + v2_lo, v1[2] + v2_lo, v1[3] + v2_lo, v1[4] + v2_lo,
                 by_sublane(v1[SUBLANES:]) + v2[0],
                 jnp.where(sub < 6,
                           by_sublane([v1[5], v1[5], v1[6], v1[6], v1[7], v1[7], v1[7], v1[7]])
                           + jnp.where(sub % 2 == 0, v2[0], v2[1]), -jnp.inf)]
        top = _merge_sublanes_top(_sort_desc(cands), TOPK)
        tau = top[TOPK - 1]
        m = top[0]
        zsum = None
        for c in cands:
            z = jnp.where(c >= tau, jnp.exp(c - m), 0.0)
            zsum = z if zsum is None else zsum + z
        zsum = jnp.sum(zsum, axis=0, keepdims=True)
        inv_z = 1.0 / zsum
        for g16 in range(s1.shape[0] // ROWS_BF16):
            ranks, gates = [], []
            for g8 in (2 * g16, 2 * g16 + 1):
                rows = slice(g8 * SUBLANES, (g8 + 1) * SUBLANES)
                a, b2 = s1[rows, :], s2[rows, :]
                count1 = _prefix_count(v2, lambda probe: a + probe >= tau)
                rank2 = _prefix_count(v2, lambda probe: probe > b2)
                c1_ref[h, rows, :] = count1
                e1_ref[h, rows, :] = jnp.exp(a - v1[0])
                ranks.append(rank2)
                gates.append(jnp.exp(b2 - v2[0]) * inv_z)
            rows16 = slice(g16 * ROWS_BF16, (g16 + 1) * ROWS_BF16)
            r2_ref[h, rows16, :] = jnp.concatenate(ranks, axis=0).astype(BF16)
            e2_ref[h, rows16, :] = jnp.concatenate(gates, axis=0).astype(BF16)


def _peer_select(pq_t, keys_b, tt=TT_SELECT):
    t = pq_t.shape[1]
    ph, _, nk, half = keys_b.shape
    tt = _pick(t, tt, LANES)
    big = lambda: pl.BlockSpec((ph, nk, tt), lambda i: (0, 0, i))
    shp = lambda dt: jax.ShapeDtypeStruct((ph, nk, t), dt)
    return pl.pallas_call(
        functools.partial(_peer_select_kernel, PH=ph, HALF=half, TOPK=PEER_TOPK),
        grid=(t // tt,),
        in_specs=[pl.BlockSpec((ph * 2 * half, tt), lambda i: (0, i)),
                  pl.BlockSpec(keys_b.shape, lambda i: (0, 0, 0, 0))],
        out_specs=[big(), big(), big(), big()],
        out_shape=[shp(F32), shp(F32), shp(BF16), shp(BF16)],
        compiler_params=_params(("parallel",)),
        name="peer_select",
    )(pq_t, keys_b)


def _gelu(x):
    return 0.5 * x * (1.0 + lax.erf(x * (1.0 / math.sqrt(2.0))))


ROWS_BF16 = 16
GATE_ROWS = 128


def _peer_main_kernel(hn_ref, u_ref, vt_ref, c1_ref, e1_ref, r2_ref, e2_ref, out_ref,
                      ata_ref, atb_ref, cta_ref, ctb_ref, *, PH, NK, NI, NJ):
    s = pl.program_id(0)
    tt = ata_ref.shape[1]

    @pl.when(s == 0)
    def _init():
        for ref in (ata_ref, atb_ref, cta_ref, ctb_ref):
            ref[...] = jnp.zeros(ref.shape, ref.dtype)

    @pl.when(lax.rem(jnp.maximum(s - 2, 0), NJ) == 0)
    def _zero_out():
        out_ref[...] = jnp.zeros(out_ref.shape, F32)

    def step(at_new, at_prev, ct_new, ct_prev):
        d = out_ref.shape[0]
        blk = lax.rem(jnp.maximum(s - 1, 0), NJ)

        def mix_rows(m0, m1):
            out_ref[m0:m1, :] += jnp.dot(vt_ref[m0:m1, :], ct_prev[...], preferred_element_type=F32)

        def score_block(m0, m1, n0, n1):
            at_new[m0:m1, n0:n1] = jnp.dot(u_ref[m0:m1, :], hn_ref[:, n0:n1], preferred_element_type=F32)

        def gate_tile(ii, tc, r0):
            row = blk * NI + ii
            cols = slice(tc * LANES, (tc + 1) * LANES)
            groups = [slice(r0 + k * ROWS_BF16, r0 + (k + 1) * ROWS_BF16) for k in range(GATE_ROWS // ROWS_BF16)]
            g = [None] * len(groups)
            zero = jnp.zeros((ROWS_BF16, LANES), BF16)
            for h in range(PH):
                bcast = lambda r: jnp.broadcast_to(r[:, cols], (ROWS_BF16, LANES)).astype(BF16)
                c1b = bcast(c1_ref[h, pl.ds(row, 1), :])
                e1b = bcast(e1_ref[h, pl.ds(row, 1), :])
                for k, rws in enumerate(groups):
                    term = e1b * jnp.where(r2_ref[h, rws, cols] < c1b, e2_ref[h, rws, cols], zero)
                    g[k] = term if g[k] is None else g[k] + term
            for k, rws in enumerate(groups):
                orow = slice(ii * NK + rws.start, ii * NK + rws.stop)
                ct_new[orow, cols] = _gelu(at_prev[orow, cols]).astype(BF16) * g[k]

        tiles = [(ii, tc, r0) for ii in range(NI) for tc in range(tt // LANES) for r0 in range(0, NK, GATE_ROWS)]
        n_mix = 4
        n_sn = 1
        n_sm = 1
        ne = at_new.shape[0]
        mxu_items = [functools.partial(mix_rows, k * d // n_mix, (k + 1) * d // n_mix) for k in range(n_mix)]
        mxu_items += [functools.partial(score_block, km * ne // n_sm, (km + 1) * ne // n_sm,
                                        kn * tt // n_sn, (kn + 1) * tt // n_sn)
                      for kn in range(n_sn) for km in range(n_sm)]
        weights = [n_sn * n_sm] * n_mix + [n_mix] * (n_sn * n_sm)
        done, total = 0, sum(weights)
        for item, w in zip(mxu_items, weights):
            upto = len(tiles) * (done + w) // total
            for tile_args in tiles[len(tiles) * done // total: upto]:
                gate_tile(*tile_args)
            item()
            done += w

    @pl.when(s % 2 == 0)
    def _even():
        step(ata_ref, atb_ref, ctb_ref, cta_ref)

    @pl.when(s % 2 == 1)
    def _odd():
        step(atb_ref, ata_ref, cta_ref, ctb_ref)


def _peer_main(hn_t, u_b, vt_b, sel, tt=TT_PEER, ne=NE_PEER):
    d, t = hn_t.shape
    c1, e1, r2, e2 = sel
    ph, nk, _ = c1.shape
    nexp = u_b.shape[0]
    tt = _pick(t, tt, LANES)
    ne = _pick(nexp, ne, nk)
    ni = ne // nk
    nj = nexp // ne
    n_tiles = t // tt
    n_blocks = n_tiles * nj
    assert nj % 2 == 0
    once = dict(pipeline_mode=pl.Buffered(1))
    tile = lambda s, lag: jnp.minimum(jnp.maximum(s - lag, 0) // nj, n_tiles - 1)
    big = lambda: pl.BlockSpec((ph, nk, tt), lambda s: (0, 0, tile(s, 1)), **once)
    return pl.pallas_call(
        functools.partial(_peer_main_kernel, PH=ph, NK=nk, NI=ni, NJ=nj),
        grid=(n_blocks + 2,),
        in_specs=[pl.BlockSpec((d, tt), lambda s: (0, tile(s, 0)), **once),
                  pl.BlockSpec((ne, d), lambda s: (lax.rem(s, nj), 0)),
                  pl.BlockSpec((d, ne), lambda s: (0, lax.rem(jnp.maximum(s - 2, 0), nj))),
                  big(), big(), big(), big()],
        out_specs=pl.BlockSpec((d, tt), lambda s: (0, tile(s, 2))),
        out_shape=jax.ShapeDtypeStruct((d, t), F32),
        scratch_shapes=[pltpu.VMEM((ne, tt), F32), pltpu.VMEM((ne, tt), F32),
                        pltpu.VMEM((ne, tt), BF16), pltpu.VMEM((ne, tt), BF16)],
        compiler_params=_params(("arbitrary",)),
        name="peer_main",
    )(hn_t, u_b, vt_b, c1, e1, r2, e2)


def _tadd_kernel(x_ref, yt_ref, o_ref):
    o_ref[...] = x_ref[...] + yt_ref[...].T


def _transpose_add(x2d, yt, tm=TM_ROWS):
    t, d = x2d.shape
    tm = _pick(t, tm, LANES)
    return pl.pallas_call(
        _tadd_kernel,
        grid=(t // tm,),
        in_specs=[pl.BlockSpec((tm, d), lambda i: (i, 0)), pl.BlockSpec((d, tm), lambda i: (0, i))],
        out_specs=pl.BlockSpec((tm, d), lambda i: (i, 0)),
        out_shape=jax.ShapeDtypeStruct((t, d), F32),
        compiler_params=_params(("parallel",)),
        name="peer_residual",
    )(x2d, yt)


def _prepare(norm_mix, w_in, conv_w, conv_b, dt_bias, a_log, d_skip, ssd_norm,
             q_a_norm, w_q_up, kv_a_norm, w_kv_up, q_norm, k_norm, attn_out_norm,
             w_out, norm_ffn, peer_w_q, peer_keys, peer_u, peer_v, rope, N):
    d = w_in.shape[0]
    W = ssd_norm.shape[0]
    H = dt_bias.shape[0]
    P = W // H
    xbc = conv_w.shape[1]
    GN = (xbc - W) // 2
    G = GN // N
    QL = q_a_norm.shape[0]
    KVL = kv_a_norm.shape[0]
    MW = attn_out_norm.shape[0]
    MH = MW // MLA_V_DIM
    QK = MLA_NOPE + rope
    HL = _round_up(H, LANES)
    half = rope // 2
    assert 2 * rope == LANES and MH % 2 == 0 and P == SSD_HEAD_DIM

    s = np.cumsum([0, W, xbc, H, QL, KVL, rope])
    wz, wxbc, wdt, wcq, wckv, wkr = (w_in[:, s[i]:s[i + 1]] for i in range(6))
    wrot = jnp.concatenate([-wkr[:, half:], wkr[:, :half]], axis=1)
    pieces = [wz, wxbc, wcq, wckv, wkr, wkr, wrot, wrot, wdt]
    offs, o = {}, 0
    for name, width in (("z", W), ("xs", W), ("bm", GN), ("cm", GN), ("cq", QL), ("ckv", KVL),
                        ("krr", LANES), ("krot", LANES), ("dt", HL)):
        assert o % width == 0, (name, o, width)
        offs[name] = o
        o += width
    NP = _round_up(o, 1024)
    w_in_p = jnp.concatenate(pieces + [jnp.zeros((d, NP - (o - HL + H)), w_in.dtype)], axis=1).astype(BF16)

    wq3 = w_q_up.reshape(QL, MH, QK)
    wq_nope = wq3[:, :, :MLA_NOPE].reshape(QL, MH * MLA_NOPE)
    wq_rope = wq3[:, :, MLA_NOPE:]
    wq_rot = jnp.concatenate([-wq_rope[:, :, half:], wq_rope[:, :, :half]], axis=2)
    wq = jnp.concatenate([wq_nope, wq_rope.reshape(QL, MH * rope), wq_rot.reshape(QL, MH * rope)], axis=1).astype(BF16)

    wkv3 = w_kv_up.reshape(KVL, MH, MLA_NOPE + MLA_V_DIM)
    wkv = jnp.concatenate([wkv3[:, :, :MLA_NOPE].reshape(KVL, MH * MLA_NOPE),
                           wkv3[:, :, MLA_NOPE:].reshape(KVL, MH * MLA_V_DIM)], axis=1).astype(BF16)

    scale = QK ** -0.5
    zeros_h = jnp.zeros((rope,), F32)

    def gains(g, sc):
        gn = (g[:MLA_NOPE] * sc).reshape(1, LANES)
        gr = jnp.stack([jnp.concatenate([g[MLA_NOPE:] * sc, zeros_h]), jnp.concatenate([zeros_h, g[MLA_NOPE:] * sc])])
        return gn.astype(F32), gr.astype(F32)

    q_gn, q_gr = gains(q_norm, scale)
    k_gn, k_gr = gains(k_norm, 1.0)

    pad_h = lambda v: jnp.concatenate([v.astype(F32), jnp.zeros((HL - H,), F32)]).reshape(1, HL)
    head_of_col = np.arange(W) // P
    expand = (np.arange(HL)[:, None] == head_of_col[None, :]).astype(np.float32)

    return dict(
        H=H, P=P, G=G, N=N, HL=HL, QL=QL, KVL=KVL, MH=MH, QK=QK, ROPE=rope, NP=NP, offs=offs,
        norm_mix=norm_mix, w_in=w_in_p,
        cwx=conv_w[:, :W], cwb=conv_w[:, W:W + GN], cwc=conv_w[:, W + GN:],
        cbx=conv_b[:W].reshape(1, W), cbb=conv_b[W:W + GN].reshape(1, GN), cbc=conv_b[W + GN:].reshape(1, GN),
        dtb=pad_h(dt_bias), alog=pad_h(a_log),
        dskip_x=jnp.repeat(d_skip.astype(F32), P).reshape(1, W), ssd_gn=ssd_norm.reshape(1, W).astype(F32),
        expand=jnp.asarray(expand, BF16),
        q_a_norm=q_a_norm.reshape(1, QL).astype(F32), kv_a_norm=kv_a_norm.reshape(1, KVL).astype(F32),
        wq=wq, wkv=wkv, q_gn=q_gn, q_gr=q_gr, k_gn=k_gn, k_gr=k_gr,
        attn_out_norm=attn_out_norm, w_out_a=w_out[:W].astype(BF16), w_out_b=w_out[W:].astype(BF16),
        norm_ffn=norm_ffn, peer_w_q_t=peer_w_q.T.astype(BF16), peer_keys=peer_keys.astype(BF16),
        peer_u=peer_u.astype(BF16), peer_vt=peer_v.T.astype(BF16),
    )


def _rope_tables(pos, rope):
    inv = 1.0 / (ROPE_BASE ** (jnp.arange(0, rope, 2, dtype=F32) / rope))
    ang = pos.astype(F32)[:, None] * inv[None, :]
    reps = LANES // (rope // 2)
    return jnp.tile(jnp.cos(ang), (1, reps)), jnp.tile(jnp.sin(ang), (1, reps))


def _layer(x, pos, past_ckv, past_krope, conv_hist, ssm_h0, prm):
    b, l, d = x.shape
    t = b * l
    H, P, G, N = prm["H"], prm["P"], prm["G"], prm["N"]
    W, GN = H * P, G * N
    offs = prm["offs"]
    x2d = x.reshape(t, d)

    h = _rms_cast(x2d, prm["norm_mix"])
    proj = _matmul([(h, prm["w_in"])], name="in_proj")
    proj3 = proj.reshape(b, l, prm["NP"])

    hist = (conv_hist[:, :, :W], conv_hist[:, :, W:W + GN], conv_hist[:, :, W + GN:])
    y_ssd, h_last, nx, nb, ncm = _ssd(proj3, offs, hist, ssm_h0.reshape(b, W, N), prm, _pick(l, SSD_CHUNK, 8))
    conv_new = jnp.concatenate([nx, nb, ncm], axis=-1)
    h_last = h_last.reshape(b, H, P, N)

    cos4, sin4 = _rope_tables(pos, prm["ROPE"])
    c_new, kr2_new, kr_new = _latent(proj3, offs, prm, cos4, sin4)
    q = _q_proj(proj3, offs, prm, cos4, sin4)
    kv_new = _kv_proj(c_new, kr2_new, prm)
    if past_ckv is None:
        o = _attention(q, [kv_new], prm["MH"], tq=_pick(l, TQ_ATTN, CHUNK))
    else:
        lp = past_ckv.shape[1]
        assert l <= CHUNK and lp % CHUNK == 0
        kv_past = _kv_proj(past_ckv, jnp.concatenate([past_krope, past_krope], axis=-1), prm)
        o = _attention(q, [kv_past, kv_new], prm["MH"])
    o_n = _rms_cast(o.reshape(t, -1), prm["attn_out_norm"])

    x_mid = _matmul([(y_ssd.reshape(t, W), prm["w_out_a"]), (o_n, prm["w_out_b"])], res=x2d,
                    name="out_proj")

    hn_t = _rms_cast(x_mid, prm["norm_ffn"], transpose=True)
    pq_t = _matmul([(prm["peer_w_q_t"], hn_t)], tn=TN_PEER_QUERY, name="peer_query")
    sel = _peer_select(pq_t, prm["peer_keys"])
    yt = _peer_main(hn_t, prm["peer_u"], prm["peer_vt"], sel)
    out = _transpose_add(x_mid, yt)
    return out.reshape(b, l, d), c_new, kr_new, h_last.astype(ssm_h0.dtype), conv_new


def kernel(x_prompt, x_sample, cache_mla_ckv, cache_mla_krope, state_ssm, state_conv, norm_mix, w_in, conv_w, conv_b, dt_bias, a_log, d_skip, ssd_norm, q_a_norm, w_q_up, kv_a_norm, w_kv_up, q_norm, k_norm, attn_out_norm, w_out, norm_ffn, peer_w_q, peer_keys, peer_u, peer_v):
    bp, lp, _ = x_prompt.shape
    ls = x_sample.shape[1]
    depth = norm_mix.shape[0]
    past = cache_mla_ckv.shape[2]
    rope = cache_mla_krope.shape[-1]
    nstate = state_ssm.shape[-1]
    xbc = state_conv.shape[-1]
    pos_p = jnp.arange(lp, dtype=jnp.int32)
    pos_s = past + jnp.arange(ls, dtype=jnp.int32)
    yp, ys = x_prompt, x_sample
    outs_p, outs_s = [], []
    for layer in range(depth):
        prm = _prepare(norm_mix[layer], w_in[layer], conv_w[layer], conv_b[layer], dt_bias[layer], a_log[layer],
                       d_skip[layer], ssd_norm[layer], q_a_norm[layer], w_q_up[layer], kv_a_norm[layer],
                       w_kv_up[layer], q_norm[layer], k_norm[layer], attn_out_norm[layer], w_out[layer],
                       norm_ffn[layer], peer_w_q[layer], peer_keys[layer], peer_u[layer], peer_v[layer],
                       rope, nstate)
        zero_conv = jnp.zeros((bp, state_conv.shape[2], xbc), x_prompt.dtype)
        zero_ssm = jnp.zeros((bp,) + state_ssm.shape[2:], state_ssm.dtype)
        yp, *rest_p = _layer(yp, pos_p, None, None, zero_conv, zero_ssm, prm)
        outs_p.append(rest_p)
        ys, *rest_s = _layer(ys, pos_s, cache_mla_ckv[layer], cache_mla_krope[layer],
                             state_conv[layer], state_ssm[layer], prm)
        outs_s.append(rest_s)
    stack = lambda outs, i: jnp.stack([o[i] for o in outs])
    return (yp, ys,
            stack(outs_p, 0), stack(outs_p, 1), stack(outs_p, 2), stack(outs_p, 3),
            stack(outs_s, 0), stack(outs_s, 1), stack(outs_s, 2), stack(outs_s, 3))
```

```python
import functools
import math

import numpy as np
import jax
import jax.numpy as jnp
from jax import lax
from jax.experimental import pallas as pl
from jax.experimental.pallas import tpu as pltpu

F32 = jnp.float32
BF16 = jnp.bfloat16

EPS = 1e-6
CHUNK = 64
SSD_CHUNK = 128
SSD_HEAD_DIM = 64
MLA_NOPE = 128
MLA_V_DIM = 128
ROPE_BASE = 10000.0
PEER_TOPK = 16
LANES = 128
SUBLANES = 8
VMEM_LIMIT = 56 * 1024 * 1024

TM_MATMUL, TN_MATMUL = 1024, 1024
TN_PEER_QUERY = 1024
TM_ROWS = 256
TM_KV = 512
TQ_ATTN = 256
TT_SELECT = 256
TT_PEER, NE_PEER = 512, 512

NT_DIMS = (((1,), (1,)), ((), ()))
TN_DIMS = (((0,), (0,)), ((), ()))


def _pick(n, pref, mult):
    if n <= pref:
        return n
    t = (pref // mult) * mult
    while t >= mult:
        if n % t == 0:
            return t
        t -= mult
    return n


def _round_up(n, m):
    return (n + m - 1) // m * m


def _params(sem, vmem=VMEM_LIMIT):
    return pltpu.CompilerParams(dimension_semantics=sem, vmem_limit_bytes=vmem)


def _split3(x):
    hi = x.astype(BF16)
    r1 = x - hi.astype(F32)
    mid = r1.astype(BF16)
    lo = (r1 - mid.astype(F32)).astype(BF16)
    return hi, mid, lo


def _dot3(a01, x, dims=None):
    out = None
    for piece in _split3(x):
        if dims is None:
            d = jnp.dot(a01, piece, preferred_element_type=F32)
        else:
            d = lax.dot_general(a01, piece, dims, preferred_element_type=F32)
        out = d if out is None else out + d
    return out


def _dot3_right(x, b01, dims=None):
    out = None
    for piece in _split3(x):
        if dims is None:
            d = jnp.dot(piece, b01, preferred_element_type=F32)
        else:
            d = lax.dot_general(piece, b01, dims, preferred_element_type=F32)
        out = d if out is None else out + d
    return out


def _rms_cast_kernel(x_ref, g_ref, o_ref):
    x = x_ref[...].astype(F32)
    ms = jnp.mean(x * x, axis=-1, keepdims=True)
    o_ref[...] = (x * lax.rsqrt(ms + EPS) * g_ref[...]).astype(o_ref.dtype)


def _rms_cast_t_kernel(x_ref, g_ref, o_ref):
    x = x_ref[...].astype(F32)
    ms = jnp.mean(x * x, axis=-1, keepdims=True)
    o_ref[...] = (x * lax.rsqrt(ms + EPS) * g_ref[...]).T.astype(o_ref.dtype)


def _rms_cast(x2d, g, out_dtype=BF16, tm=TM_ROWS, transpose=False):
    t, d = x2d.shape
    tm = _pick(t, tm, LANES if transpose else 16)
    if transpose:
        body, out_spec, out_shape = _rms_cast_t_kernel, pl.BlockSpec((d, tm), lambda i: (0, i)), (d, t)
    else:
        body, out_spec, out_shape = _rms_cast_kernel, pl.BlockSpec((tm, d), lambda i: (i, 0)), (t, d)
    return pl.pallas_call(
        body,
        grid=(t // tm,),
        in_specs=[pl.BlockSpec((tm, d), lambda i: (i, 0)), pl.BlockSpec((1, d), lambda i: (0, 0))],
        out_specs=out_spec,
        out_shape=jax.ShapeDtypeStruct(out_shape, out_dtype),
        compiler_params=_params(("parallel",)),
        name="rms_cast_t" if transpose else "rms_cast",
    )(x2d, g.reshape(1, d).astype(F32))


def _transpose_cast_kernel(w_ref, o_ref):
    o_ref[...] = w_ref[...].T.astype(o_ref.dtype)


def _transpose_cast(w, out_dtype=BF16, tr=512, tc=1024):
    r, c = w.shape
    tr, tc = _pick(r, tr, LANES), _pick(c, tc, LANES)
    return pl.pallas_call(
        _transpose_cast_kernel,
        grid=(r // tr, c // tc),
        in_specs=[pl.BlockSpec((tr, tc), lambda i, j: (i, j))],
        out_specs=pl.BlockSpec((tc, tr), lambda i, j: (j, i)),
        out_shape=jax.ShapeDtypeStruct((c, r), out_dtype),
        compiler_params=_params(("parallel", "parallel")),
        name="transpose_cast",
    )(w)


def _mm_kernel(*refs, n_pairs, has_res):
    o_ref = refs[-1]
    acc = None
    for p in range(n_pairs):
        d = jnp.dot(refs[2 * p][...], refs[2 * p + 1][...], preferred_element_type=F32)
        acc = d if acc is None else acc + d
    if has_res:
        acc = refs[2 * n_pairs][...] + acc
    o_ref[...] = acc.astype(o_ref.dtype)


def _matmul(pairs, res=None, out_dtype=F32, tm=TM_MATMUL, tn=TN_MATMUL, name="matmul"):
    m = pairs[0][0].shape[0]
    n = pairs[0][1].shape[1]
    tm = _pick(m, tm, 16)
    tn = _pick(n, tn, LANES)
    in_specs, args = [], []
    for a, w in pairs:
        k = a.shape[1]
        in_specs += [pl.BlockSpec((tm, k), lambda i, j: (i, 0)), pl.BlockSpec((k, tn), lambda i, j: (0, j))]
        args += [a, w]
    if res is not None:
        in_specs.append(pl.BlockSpec((tm, tn), lambda i, j: (i, j)))
        args.append(res)
    return pl.pallas_call(
        functools.partial(_mm_kernel, n_pairs=len(pairs), has_res=res is not None),
        grid=(m // tm, n // tn),
        in_specs=in_specs,
        out_specs=pl.BlockSpec((tm, tn), lambda i, j: (i, j)),
        out_shape=jax.ShapeDtypeStruct((m, n), out_dtype),
        compiler_params=_params(("parallel", "arbitrary")),
        name=name,
    )(*args)


def _silu(x):
    hx = 0.5 * x
    return hx + hx * jnp.tanh(hx)


def _softplus(x):
    return jnp.maximum(x, 0.0) + jnp.log1p(jnp.exp(-jnp.abs(x)))


def _ssd_kernel(z_ref, xs_ref, bm_ref, cm_ref, dt_ref,
                hx_ref, hb_ref, hc_ref, h0_ref,
                cwx_ref, cwb_ref, cwc_ref, cbx_ref, cbb_ref, cbc_ref,
                dtb_ref, alog_ref, dskip_ref, gn_ref,
                tri_ref, exp_ref,
                y_ref, hlast_ref, nx_ref, nb_ref, nc_ref,
                bx_ref, bb_ref, bc_ref, state_ref, ydiag_ref,
                *, Q, H, P, G, N):
    c = pl.program_id(1)
    nc = pl.num_programs(1)
    R = H // G
    W = H * P

    @pl.when(c == 0)
    def _init():
        for buf, hist in ((bx_ref, hx_ref), (bb_ref, hb_ref), (bc_ref, hc_ref)):
            buf[0:8, :] = jnp.zeros((8, buf.shape[1]), F32)
            buf[5:8, :] = hist[0]
        state_ref[...] = h0_ref[0]

    def conv(buf, x_ref, cw_ref, cb_ref):
        x = x_ref[0]
        buf[8:8 + Q, :] = x
        win = buf[...]
        acc = None
        for k in range(3):
            tap = pltpu.roll(win, Q + 3 - k, 0)[0:Q, :] * cw_ref[k:k + 1, :]
            acc = tap if acc is None else acc + tap
        acc = acc + x * cw_ref[3:4, :]
        buf[0:8, :] = buf[Q:Q + 8, :]
        return _silu(cb_ref[...] + acc)

    xs = conv(bx_ref, xs_ref, cwx_ref, cbx_ref)
    bm = conv(bb_ref, bm_ref, cwb_ref, cbb_ref)
    cm = conv(bc_ref, cm_ref, cwc_ref, cbc_ref)

    @pl.when(c == nc - 1)
    def _tail():
        nx_ref[0] = bx_ref[5:8, :]
        nb_ref[0] = bb_ref[5:8, :]
        nc_ref[0] = bc_ref[5:8, :]

    dt = _softplus(dt_ref[0] + dtb_ref[...])
    a_neg = -jnp.exp(alog_ref[...])
    dta = dt * a_neg
    tri = tri_ref[...]
    acum = _dot3(tri, dta)
    eye = (lax.broadcasted_iota(jnp.int32, (Q, Q), 0) == lax.broadcasted_iota(jnp.int32, (Q, Q), 1)).astype(BF16)
    acum_t = _dot3_right(acum, eye, TN_DIMS)
    dt_t = _dot3_right(dt, eye, TN_DIMS)

    acum_x = _dot3_right(acum, exp_ref[...])
    dt_x = _dot3_right(dt, exp_ref[...])

    ii = lax.broadcasted_iota(jnp.int32, (Q, Q), 0)
    jj = lax.broadcasted_iota(jnp.int32, (Q, Q), 1)
    causal = ii >= jj
    xs_b = xs.astype(BF16)
    bm_b = bm.astype(BF16)
    cm_b = cm.astype(BF16)
    st_b = state_ref[...].astype(BF16)

    exp_acum_x = jnp.exp(acum_x)
    decay_x = jnp.exp(acum_x[Q - 1:Q, :] - acum_x) * dt_x
    xd_b = (xs * decay_x).astype(BF16)

    chunk_decay = jnp.exp(acum_t[:, Q - 1:Q])

    for g in range(G):
        cg = cm_b[:, g * N:(g + 1) * N]
        bg = bm_b[:, g * N:(g + 1) * N]
        cb = lax.dot_general(cg, bg, NT_DIMS, preferred_element_type=F32)
        for r in range(R):
            h = g * R + r
            dm = acum[:, h:h + 1] - acum_t[h:h + 1, :]
            lm = jnp.where(causal, jnp.exp(dm), 0.0)
            wm = (cb * lm * dt_t[h:h + 1, :]).astype(BF16)
            ydiag_ref[:, h * P:(h + 1) * P] = jnp.dot(wm, xs_b[:, h * P:(h + 1) * P],
                                                      preferred_element_type=F32)
        rows = slice(g * R * P, (g + 1) * R * P)
        y_off = lax.dot_general(cg, st_b[rows, :], NT_DIMS, preferred_element_type=F32)
        ydiag_ref[:, rows] = ydiag_ref[:, rows] + y_off * exp_acum_x[:, rows]
        s_new = lax.dot_general(xd_b[:, rows], bg, TN_DIMS, preferred_element_type=F32)
        for r in range(R):
            h = g * R + r
            hr = slice(h * P, (h + 1) * P)
            state_ref[hr, :] = chunk_decay[h:h + 1, :] * state_ref[hr, :] + s_new[r * P:(r + 1) * P, :]

    y = ydiag_ref[...] + dskip_ref[...] * xs
    yg = y * _silu(z_ref[0])
    ms = jnp.mean(yg * yg, axis=-1, keepdims=True)
    y_ref[0] = (yg * lax.rsqrt(ms + EPS) * gn_ref[...]).astype(y_ref.dtype)

    @pl.when(c == nc - 1)
    def _final():
        hlast_ref[0] = state_ref[...]


def _ssd(proj3, offs, hist, h0, prm, Q):
    b, l, _ = proj3.shape
    H, P, G, N = prm["H"], prm["P"], prm["G"], prm["N"]
    W, GN, HL = H * P, G * N, prm["HL"]
    nchunk = l // Q
    hx, hb, hc = hist
    full = lambda shape: pl.BlockSpec(shape, lambda i, j: (0,) * len(shape))
    per_b = lambda shape: pl.BlockSpec(shape, lambda i, j: (i,) + (0,) * (len(shape) - 1))
    col = lambda width, off: pl.BlockSpec((1, Q, width), lambda i, j, o=off // width: (i, j, o))
    in_specs = [
        col(W, offs["z"]), col(W, offs["xs"]), col(GN, offs["bm"]), col(GN, offs["cm"]), col(HL, offs["dt"]),
        per_b((1, 3, W)), per_b((1, 3, GN)), per_b((1, 3, GN)), per_b((1, W, N)),
        full((4, W)), full((4, GN)), full((4, GN)), full((1, W)), full((1, GN)), full((1, GN)),
        full((1, HL)), full((1, HL)), full((1, W)), full((1, W)),
        full((Q, Q)), full((HL, W)),
    ]
    out_specs = [
        pl.BlockSpec((1, Q, W), lambda i, j: (i, j, 0)),
        per_b((1, W, N)), per_b((1, 3, W)), per_b((1, 3, GN)), per_b((1, 3, GN)),
    ]
    out_shape = [
        jax.ShapeDtypeStruct((b, l, W), BF16),
        jax.ShapeDtypeStruct((b, W, N), F32),
        jax.ShapeDtypeStruct((b, 3, W), F32),
        jax.ShapeDtypeStruct((b, 3, GN), F32),
        jax.ShapeDtypeStruct((b, 3, GN), F32),
    ]
    scratch = [
        pltpu.VMEM((Q + 8, W), F32), pltpu.VMEM((Q + 8, GN), F32), pltpu.VMEM((Q + 8, GN), F32),
        pltpu.VMEM((W, N), F32), pltpu.VMEM((Q, W), F32),
    ]
    return pl.pallas_call(
        functools.partial(_ssd_kernel, Q=Q, H=H, P=P, G=G, N=N),
        grid=(b, nchunk),
        in_specs=in_specs, out_specs=out_specs, out_shape=out_shape, scratch_shapes=scratch,
        compiler_params=_params(("parallel", "arbitrary")),
        name="ssd_scan",
    )(proj3, proj3, proj3, proj3, proj3, hx, hb, hc, h0,
      prm["cwx"], prm["cwb"], prm["cwc"], prm["cbx"], prm["cbb"], prm["cbc"],
      prm["dtb"], prm["alog"], prm["dskip_x"], prm["ssd_gn"],
      jnp.asarray(np.tril(np.ones((Q, Q), np.float32)), BF16), prm["expand"])


def _latent_kernel(ckv_ref, krr_ref, krot_ref, g_ref, cos_ref, sin_ref, c_ref, kr2_ref, kr_ref, *, rope):
    x = ckv_ref[0]
    ms = jnp.mean(x * x, axis=-1, keepdims=True)
    c_ref[0] = x * lax.rsqrt(ms + EPS) * g_ref[...]
    kr2 = krr_ref[0] * cos_ref[...] + krot_ref[0] * sin_ref[...]
    kr2_ref[0] = kr2
    kr_ref[0] = kr2[:, :rope]


def _latent(proj3, offs, prm, cos4, sin4, tm=TM_ROWS):
    b, l, _ = proj3.shape
    kvl, rope = prm["KVL"], prm["ROPE"]
    tm = _pick(l, tm, 8)
    col = lambda width, off: pl.BlockSpec((1, tm, width), lambda i, j, o=off // width: (i, j, o))
    return pl.pallas_call(
        functools.partial(_latent_kernel, rope=rope),
        grid=(b, l // tm),
        in_specs=[col(kvl, offs["ckv"]), col(LANES, offs["krr"]), col(LANES, offs["krot"]),
                  pl.BlockSpec((1, kvl), lambda i, j: (0, 0)),
                  pl.BlockSpec((tm, LANES), lambda i, j: (j, 0)), pl.BlockSpec((tm, LANES), lambda i, j: (j, 0))],
        out_specs=[pl.BlockSpec((1, tm, kvl), lambda i, j: (i, j, 0)),
                   pl.BlockSpec((1, tm, LANES), lambda i, j: (i, j, 0)),
                   pl.BlockSpec((1, tm, rope), lambda i, j: (i, j, 0))],
        out_shape=[jax.ShapeDtypeStruct((b, l, kvl), F32), jax.ShapeDtypeStruct((b, l, LANES), F32),
                   jax.ShapeDtypeStruct((b, l, rope), F32)],
        compiler_params=_params(("parallel", "parallel")),
        name="mla_latent",
    )(proj3, proj3, proj3, prm["kv_a_norm"], cos4, sin4)


def _q_kernel(cq_ref, g_ref, w_ref, cos_ref, sin_ref, gn_ref, gr_ref, q_ref, *, MH, qk_dim):
    x = cq_ref[0]
    ms = jnp.mean(x * x, axis=-1, keepdims=True)
    xn = (x * lax.rsqrt(ms + EPS) * g_ref[...]).astype(BF16)
    qf = jnp.dot(xn, w_ref[...], preferred_element_type=F32)
    cos = cos_ref[...]
    sin = sin_ref[...]
    lane = lax.broadcasted_iota(jnp.int32, (1, LANES), 1)
    half_mask = [(lane < LANES // 2).astype(F32), (lane >= LANES // 2).astype(F32)]
    rope0 = MH * MLA_NOPE
    rot0 = rope0 + MH * (LANES // 2)
    for p in range(MH // 2):
        rp = (qf[:, rope0 + p * LANES: rope0 + (p + 1) * LANES] * cos
              + qf[:, rot0 + p * LANES: rot0 + (p + 1) * LANES] * sin)
        for e in (0, 1):
            h = 2 * p + e
            nope = qf[:, h * MLA_NOPE:(h + 1) * MLA_NOPE]
            rh = rp * half_mask[e]
            ssq = jnp.sum(nope * nope, axis=-1, keepdims=True) + jnp.sum(rh * rh, axis=-1, keepdims=True)
            inv = lax.rsqrt(ssq * (1.0 / qk_dim) + EPS)
            q_ref[0, :, h * 2 * LANES: h * 2 * LANES + LANES] = (nope * inv * gn_ref[...]).astype(q_ref.dtype)
            q_ref[0, :, h * 2 * LANES + LANES:(h + 1) * 2 * LANES] = (rh * inv * gr_ref[e:e + 1, :]).astype(q_ref.dtype)


def _q_proj(proj3, offs, prm, cos4, sin4, tm=TM_ROWS):
    b, l, _ = proj3.shape
    ql, mh = prm["QL"], prm["MH"]
    tm = _pick(l, tm, 16)
    wq = prm["wq"]
    return pl.pallas_call(
        functools.partial(_q_kernel, MH=mh, qk_dim=prm["QK"]),
        grid=(b, l // tm),
        in_specs=[pl.BlockSpec((1, tm, ql), lambda i, j, o=offs["cq"] // ql: (i, j, o)),
                  pl.BlockSpec((1, ql), lambda i, j: (0, 0)),
                  pl.BlockSpec(wq.shape, lambda i, j: (0, 0)),
                  pl.BlockSpec((tm, LANES), lambda i, j: (j, 0)), pl.BlockSpec((tm, LANES), lambda i, j: (j, 0)),
                  pl.BlockSpec((1, LANES), lambda i, j: (0, 0)), pl.BlockSpec((2, LANES), lambda i, j: (0, 0))],
        out_specs=pl.BlockSpec((1, tm, mh * 2 * LANES), lambda i, j: (i, j, 0)),
        out_shape=jax.ShapeDtypeStruct((b, l, mh * 2 * LANES), BF16),
        compiler_params=_params(("parallel", "parallel")),
        name="mla_q",
    )(proj3, prm["q_a_norm"], wq, cos4, sin4, prm["q_gn"], prm["q_gr"])


def _kv_kernel(c_ref, kr2_ref, w_ref, gn_ref, gr_ref, k_ref, v_ref, *, MH, qk_dim):
    c = c_ref[0].astype(BF16)
    kv = jnp.dot(c, w_ref[...], preferred_element_type=F32)
    kr2 = kr2_ref[0]
    lane = lax.broadcasted_iota(jnp.int32, (1, LANES), 1)
    kr_lo = kr2 * (lane < LANES // 2).astype(F32)
    kr_ss = jnp.sum(kr_lo * kr_lo, axis=-1, keepdims=True)
    for h in range(MH):
        nope = kv[:, h * MLA_NOPE:(h + 1) * MLA_NOPE]
        ssq = jnp.sum(nope * nope, axis=-1, keepdims=True) + kr_ss
        inv = lax.rsqrt(ssq * (1.0 / qk_dim) + EPS)
        k_ref[0, :, h * 2 * LANES: h * 2 * LANES + LANES] = (nope * inv * gn_ref[...]).astype(k_ref.dtype)
        k_ref[0, :, h * 2 * LANES + LANES:(h + 1) * 2 * LANES] = (kr2 * inv * gr_ref[h % 2:h % 2 + 1, :]).astype(k_ref.dtype)
    v_ref[0] = kv[:, MH * MLA_NOPE:].astype(v_ref.dtype)


def _kv_proj(c_all, kr2_all, prm, tm=TM_KV):
    b, lk, kvl = c_all.shape
    mh = prm["MH"]
    tm = _pick(lk, tm, 16)
    wkv = prm["wkv"]
    return pl.pallas_call(
        functools.partial(_kv_kernel, MH=mh, qk_dim=prm["QK"]),
        grid=(b, lk // tm),
        in_specs=[pl.BlockSpec((1, tm, kvl), lambda i, j: (i, j, 0)),
                  pl.BlockSpec((1, tm, LANES), lambda i, j: (i, j, 0)),
                  pl.BlockSpec(wkv.shape, lambda i, j: (0, 0)),
                  pl.BlockSpec((1, LANES), lambda i, j: (0, 0)), pl.BlockSpec((2, LANES), lambda i, j: (0, 0))],
        out_specs=[pl.BlockSpec((1, tm, mh * 2 * LANES), lambda i, j: (i, j, 0)),
                   pl.BlockSpec((1, tm, mh * MLA_V_DIM), lambda i, j: (i, j, 0))],
        out_shape=[jax.ShapeDtypeStruct((b, lk, mh * 2 * LANES), BF16),
                   jax.ShapeDtypeStruct((b, lk, mh * MLA_V_DIM), BF16)],
        compiler_params=_params(("parallel", "parallel")),
        name="mla_kv",
    )(c_all, kr2_all, wkv, prm["k_gn"], prm["k_gr"])


def _attn_causal_kernel(q_ref, k_ref, v_ref, o_ref, *, tq, lq):
    for qi in range(lq // tq):
        rows = slice(qi * tq, (qi + 1) * tq)
        q = q_ref[0, rows, :]
        past = qi * tq
        sd = lax.dot_general(q, k_ref[0, past:past + tq, :], NT_DIMS, preferred_element_type=F32)
        ri = lax.broadcasted_iota(jnp.int32, (tq, tq), 0) // CHUNK
        ci = lax.broadcasted_iota(jnp.int32, (tq, tq), 1) // CHUNK
        sd = jnp.where(ri >= ci, sd, -jnp.inf)
        m = jnp.max(sd, axis=-1, keepdims=True)
        if past > 0:
            sp = lax.dot_general(q, k_ref[0, 0:past, :], NT_DIMS, preferred_element_type=F32)
            m = jnp.maximum(m, jnp.max(sp, axis=-1, keepdims=True))
            pp = jnp.exp(sp - m)
            l = jnp.sum(pp, axis=-1, keepdims=True)
            acc = jnp.dot(pp.astype(BF16), v_ref[0, 0:past, :], preferred_element_type=F32)
        pd = jnp.exp(sd - m)
        ld = jnp.sum(pd, axis=-1, keepdims=True)
        accd = jnp.dot(pd.astype(BF16), v_ref[0, past:past + tq, :], preferred_element_type=F32)
        if past > 0:
            l, acc = l + ld, acc + accd
        else:
            l, acc = ld, accd
        o_ref[0, rows, :] = acc / l


def _attn_cached_kernel(q_ref, kc_ref, vc_ref, kn_ref, vn_ref, o_ref):
    q = q_ref[0]
    sc = lax.dot_general(q, kc_ref[0], NT_DIMS, preferred_element_type=F32)
    sn = lax.dot_general(q, kn_ref[0], NT_DIMS, preferred_element_type=F32)
    m = jnp.maximum(jnp.max(sc, axis=-1, keepdims=True), jnp.max(sn, axis=-1, keepdims=True))
    pc = jnp.exp(sc - m)
    pn = jnp.exp(sn - m)
    l = jnp.sum(pc, axis=-1, keepdims=True) + jnp.sum(pn, axis=-1, keepdims=True)
    acc = (jnp.dot(pc.astype(BF16), vc_ref[0], preferred_element_type=F32)
           + jnp.dot(pn.astype(BF16), vn_ref[0], preferred_element_type=F32))
    o_ref[0] = acc / l


def _attention(q, kvs, mh, tq=None):
    b, lq, _ = q.shape
    head = lambda rows, width: pl.BlockSpec((1, rows, width), lambda bi, h: (bi, 0, h))
    in_specs, args = [head(lq, 2 * LANES)], [q]
    for k, v in kvs:
        in_specs += [head(k.shape[1], 2 * LANES), head(v.shape[1], MLA_V_DIM)]
        args += [k, v]
    body = functools.partial(_attn_causal_kernel, tq=tq, lq=lq) if len(kvs) == 1 else _attn_cached_kernel
    return pl.pallas_call(
        body,
        grid=(b, mh),
        in_specs=in_specs,
        out_specs=head(lq, MLA_V_DIM),
        out_shape=jax.ShapeDtypeStruct((b, lq, mh * MLA_V_DIM), F32),
        compiler_params=_params(("parallel", "parallel")),
        name="mla_attention",
    )(*args)


def _oddeven_sort_pairs(lo, n):
    def merge(lo, n, r):
        step = r * 2
        if step < n:
            yield from merge(lo, n, step)
            yield from merge(lo + r, n, step)
            for i in range(lo + r, lo + n - r, step):
                yield (i, i + r)
        else:
            yield (lo, lo + r)
    if n > 1:
        m = n // 2
        yield from _oddeven_sort_pairs(lo, m)
        yield from _oddeven_sort_pairs(lo + m, m)
        yield from merge(lo, n, 1)


def _sort_desc(xs):
    xs = list(xs)
    for i, j in _oddeven_sort_pairs(0, len(xs)):
        xs[i], xs[j] = jnp.maximum(xs[i], xs[j]), jnp.minimum(xs[i], xs[j])
    return xs


def _bitonic_merge_desc(xs):
    xs = list(xs)
    n = len(xs)
    d = n // 2
    while d >= 1:
        for i in range(n):
            if i & d == 0:
                xs[i], xs[i + d] = jnp.maximum(xs[i], xs[i + d]), jnp.minimum(xs[i], xs[i + d])
        d //= 2
    return xs


def _merge_sublanes_top(xs, k):
    shift = SUBLANES // 2
    while shift >= 1:
        other = [pltpu.roll(x, shift, 0) for x in xs]
        if len(xs) < k:
            xs = _bitonic_merge_desc(xs + other[::-1])
        else:
            xs = _bitonic_merge_desc([jnp.maximum(xs[i], other[k - 1 - i]) for i in range(k)])
        shift //= 2
    return xs


def _top_sorted(x, k):
    groups = [x[i * SUBLANES:(i + 1) * SUBLANES, :] for i in range(x.shape[0] // SUBLANES)]
    return _merge_sublanes_top(_sort_desc(groups)[:k], k)


def _prefix_count(rows, test):
    w = jnp.where
    g8 = test(rows[7])
    g4 = test(w(g8, rows[11], rows[3]))
    g2 = test(w(g8, w(g4, rows[13], rows[9]), w(g4, rows[5], rows[1])))
    g1 = test(w(g8, w(g4, w(g2, rows[14], rows[12]), w(g2, rows[10], rows[8])),
                w(g4, w(g2, rows[6], rows[4]), w(g2, rows[2], rows[0]))))
    g16 = test(rows[15])
    return w(g8, 8.0, 0.0) + w(g4, 4.0, 0.0) + w(g2, 2.0, 0.0) + w(g1, 1.0, 0.0) + w(g16, 1.0, 0.0)


def _peer_select_kernel(pq_ref, keys_ref, c1_ref, e1_ref, r2_ref, e2_ref, *, PH, HALF, TOPK):
    assert TOPK == 2 * SUBLANES
    tt = pq_ref.shape[1]
    sub = lax.broadcasted_iota(jnp.int32, (SUBLANES, tt), 0)

    def by_sublane(rows):
        out = rows[-1]
        for s in range(len(rows) - 2, -1, -1):
            out = jnp.where(sub == s, rows[s], out)
        return out

    for h in range(PH):
        q1 = pq_ref[h * 2 * HALF: h * 2 * HALF + HALF, :].astype(BF16)
        q2 = pq_ref[h * 2 * HALF + HALF:(h + 1) * 2 * HALF, :].astype(BF16)
        s1 = jnp.dot(keys_ref[h, 0], q1, preferred_element_type=F32)
        s2 = jnp.dot(keys_ref[h, 1], q2, preferred_element_type=F32)
        v1 = _top_sorted(s1, TOPK)
        v2 = _top_sorted(s2, TOPK)
        v2_lo, v2_hi = by_sublane(v2[:SUBLANES]), by_sublane(v2[SUBLANES:])
        cands = [v1[0] + v2_lo, v1[0] + v2_hi, v1[1] + v2_lo, v1[2] + v2_lo, v1[3] + v2_lo, v1[4] + v2_lo,
                 by_sublane(v1[SUBLANES:]) + v2[0],
                 jnp.where(sub < 6,
                           by_sublane([v1[5], v1[5], v1[6], v1[6], v1[7], v1[7], v1[7], v1[7]])
                           + jnp.where(sub % 2 == 0, v2[0], v2[1]), -jnp.inf)]
        top = _merge_sublanes_top(_sort_desc(cands), TOPK)
        tau = top[TOPK - 1]
        m = top[0]
        zsum = None
        for c in cands:
            z = jnp.where(c >= tau, jnp.exp(c - m), 0.0)
            zsum = z if zsum is None else zsum + z
        zsum = jnp.sum(zsum, axis=0, keepdims=True)
        inv_z = 1.0 / zsum
        for g16 in range(s1.shape[0] // ROWS_BF16):
            ranks, gates = [], []
            for g8 in (2 * g16, 2 * g16 + 1):
                rows = slice(g8 * SUBLANES, (g8 + 1) * SUBLANES)
                a, b2 = s1[rows, :], s2[rows, :]
                count1 = _prefix_count(v2, lambda probe: a + probe >= tau)
                rank2 = _prefix_count(v2, lambda probe: probe > b2)
                c1_ref[h, rows, :] = count1
                e1_ref[h, rows, :] = jnp.exp(a - v1[0])
                ranks.append(rank2)
                gates.append(jnp.exp(b2 - v2[0]) * inv_z)
            rows16 = slice(g16 * ROWS_BF16, (g16 + 1) * ROWS_BF16)
            r2_ref[h, rows16, :] = jnp.concatenate(ranks, axis=0).astype(BF16)
            e2_ref[h, rows16, :] = jnp.concatenate(gates, axis=0).astype(BF16)


def _peer_select(pq_t, keys_b, tt=TT_SELECT):
    t = pq_t.shape[1]
    ph, _, nk, half = keys_b.shape
    tt = _pick(t, tt, LANES)
    big = lambda: pl.BlockSpec((ph, nk, tt), lambda i: (0, 0, i))
    shp = lambda dt: jax.ShapeDtypeStruct((ph, nk, t), dt)
    return pl.pallas_call(
        functools.partial(_peer_select_kernel, PH=ph, HALF=half, TOPK=PEER_TOPK),
        grid=(t // tt,),
        in_specs=[pl.BlockSpec((ph * 2 * half, tt), lambda i: (0, i)),
                  pl.BlockSpec(keys_b.shape, lambda i: (0, 0, 0, 0))],
        out_specs=[big(), big(), big(), big()],
        out_shape=[shp(F32), shp(F32), shp(BF16), shp(BF16)],
        compiler_params=_params(("parallel",)),
        name="peer_select",
    )(pq_t, keys_b)


def _gelu(x):
    return 0.5 * x * (1.0 + lax.erf(x * (1.0 / math.sqrt(2.0))))


ROWS_BF16 = 16
GATE_ROWS = 128


def _peer_main_kernel(hn_ref, u_ref, vt_ref, c1_ref, e1_ref, r2_ref, e2_ref, out_ref,
                      ata_ref, atb_ref, cta_ref, ctb_ref, *, PH, NK, NI, NJ):
    s = pl.program_id(0)
    tt = ata_ref.shape[1]

    @pl.when(s == 0)
    def _init():
        for ref in (ata_ref, atb_ref, cta_ref, ctb_ref):
            ref[...] = jnp.zeros(ref.shape, ref.dtype)

    @pl.when(lax.rem(jnp.maximum(s - 2, 0), NJ) == 0)
    def _zero_out():
        out_ref[...] = jnp.zeros(out_ref.shape, F32)

    def step(at_new, at_prev, ct_new, ct_prev):
        d = out_ref.shape[0]
        blk = lax.rem(jnp.maximum(s - 1, 0), NJ)

        def mix_rows(m0, m1):
            out_ref[m0:m1, :] += jnp.dot(vt_ref[m0:m1, :], ct_prev[...], preferred_element_type=F32)

        def score_block(m0, m1, n0, n1):
            at_new[m0:m1, n0:n1] = jnp.dot(u_ref[m0:m1, :], hn_ref[:, n0:n1], preferred_element_type=F32)

        def gate_tile(ii, tc, r0):
            row = blk * NI + ii
            cols = slice(tc * LANES, (tc + 1) * LANES)
            groups = [slice(r0 + k * ROWS_BF16, r0 + (k + 1) * ROWS_BF16) for k in range(GATE_ROWS // ROWS_BF16)]
            g = [None] * len(groups)
            zero = jnp.zeros((ROWS_BF16, LANES), BF16)
            for h in range(PH):
                bcast = lambda r: jnp.broadcast_to(r[:, cols], (ROWS_BF16, LANES)).astype(BF16)
                c1b = bcast(c1_ref[h, pl.ds(row, 1), :])
                e1b = bcast(e1_ref[h, pl.ds(row, 1), :])
                for k, rws in enumerate(groups):
                    term = e1b * jnp.where(r2_ref[h, rws, cols] < c1b, e2_ref[h, rws, cols], zero)
                    g[k] = term if g[k] is None else g[k] + term
            for k, rws in enumerate(groups):
                orow = slice(ii * NK + rws.start, ii * NK + rws.stop)
                ct_new[orow, cols] = _gelu(at_prev[orow, cols]).astype(BF16) * g[k]

        tiles = [(ii, tc, r0) for ii in range(NI) for tc in range(tt // LANES) for r0 in range(0, NK, GATE_ROWS)]
        n_mix = 16
        n_sn = 1
        n_sm = 1
        ne = at_new.shape[0]
        mxu_items = [functools.partial(mix_rows, k * d // n_mix, (k + 1) * d // n_mix) for k in range(n_mix)]
        mxu_items += [functools.partial(score_block, km * ne // n_sm, (km + 1) * ne // n_sm,
                                        kn * tt // n_sn, (kn + 1) * tt // n_sn)
                      for kn in range(n_sn) for km in range(n_sm)]
        weights = [1] * n_mix + [0] * (n_sn * n_sm)
        done, total = 0, sum(weights)
        for item, w in zip(mxu_items, weights):
            upto = len(tiles) * (done + w) // total
            for tile_args in tiles[len(tiles) * done // total: upto]:
                gate_tile(*tile_args)
            item()
            done += w

    @pl.when(s % 2 == 0)
    def _even():
        step(ata_ref, atb_ref, ctb_ref, cta_ref)

    @pl.when(s % 2 == 1)
    def _odd():
        step(atb_ref, ata_ref, cta_ref, ctb_ref)


def _peer_main(hn_t, u_b, vt_b, sel, tt=TT_PEER, ne=NE_PEER):
    d, t = hn_t.shape
    c1, e1, r2, e2 = sel
    ph, nk, _ = c1.shape
    nexp = u_b.shape[0]
    tt = _pick(t, tt, LANES)
    ne = _pick(nexp, ne, nk)
    ni = ne // nk
    nj = nexp // ne
    n_tiles = t // tt
    n_blocks = n_tiles * nj
    assert nj % 2 == 0
    once = dict(pipeline_mode=pl.Buffered(1))
    tile = lambda s, lag: jnp.minimum(jnp.maximum(s - lag, 0) // nj, n_tiles - 1)
    big = lambda: pl.BlockSpec((ph, nk, tt), lambda s: (0, 0, tile(s, 1)), **once)
    return pl.pallas_call(
        functools.partial(_peer_main_kernel, PH=ph, NK=nk, NI=ni, NJ=nj),
        grid=(n_blocks + 2,),
        in_specs=[pl.BlockSpec((d, tt), lambda s: (0, tile(s, 0)), **once),
                  pl.BlockSpec((ne, d), lambda s: (lax.rem(s, nj), 0)),
                  pl.BlockSpec((d, ne), lambda s: (0, lax.rem(jnp.maximum(s - 2, 0), nj))),
                  big(), big(), big(), big()],
        out_specs=pl.BlockSpec((d, tt), lambda s: (0, tile(s, 2))),
        out_shape=jax.ShapeDtypeStruct((d, t), F32),
        scratch_shapes=[pltpu.VMEM((ne, tt), F32), pltpu.VMEM((ne, tt), F32),
                        pltpu.VMEM((ne, tt), BF16), pltpu.VMEM((ne, tt), BF16)],
        compiler_params=_params(("arbitrary",)),
        name="peer_main",
    )(hn_t, u_b, vt_b, c1, e1, r2, e2)


def _tadd_kernel(x_ref, yt_ref, o_ref):
    o_ref[...] = x_ref[...] + yt_ref[...].T


def _transpose_add(x2d, yt, tm=TM_ROWS):
    t, d = x2d.shape
    tm = _pick(t, tm, LANES)
    return pl.pallas_call(
        _tadd_kernel,
        grid=(t // tm,),
        in_specs=[pl.BlockSpec((tm, d), lambda i: (i, 0)), pl.BlockSpec((d, tm), lambda i: (0, i))],
        out_specs=pl.BlockSpec((tm, d), lambda i: (i, 0)),
        out_shape=jax.ShapeDtypeStruct((t, d), F32),
        compiler_params=_params(("parallel",)),
        name="peer_residual",
    )(x2d, yt)


def _prepare(norm_mix, w_in, conv_w, conv_b, dt_bias, a_log, d_skip, ssd_norm,
             q_a_norm, w_q_up, kv_a_norm, w_kv_up, q_norm, k_norm, attn_out_norm,
             w_out, norm_ffn, peer_w_q, peer_keys, peer_u, peer_v, rope, N):
    d = w_in.shape[0]
    W = ssd_norm.shape[0]
    H = dt_bias.shape[0]
    P = W // H
    xbc = conv_w.shape[1]
    GN = (xbc - W) // 2
    G = GN // N
    QL = q_a_norm.shape[0]
    KVL = kv_a_norm.shape[0]
    MW = attn_out_norm.shape[0]
    MH = MW // MLA_V_DIM
    QK = MLA_NOPE + rope
    HL = _round_up(H, LANES)
    half = rope // 2
    assert 2 * rope == LANES and MH % 2 == 0 and P == SSD_HEAD_DIM

    s = np.cumsum([0, W, xbc, H, QL, KVL, rope])
    wz, wxbc, wdt, wcq, wckv, wkr = (w_in[:, s[i]:s[i + 1]] for i in range(6))
    wrot = jnp.concatenate([-wkr[:, half:], wkr[:, :half]], axis=1)
    pieces = [wz, wxbc, wcq, wckv, wkr, wkr, wrot, wrot, wdt]
    offs, o = {}, 0
    for name, width in (("z", W), ("xs", W), ("bm", GN), ("cm", GN), ("cq", QL), ("ckv", KVL),
                        ("krr", LANES), ("krot", LANES), ("dt", HL)):
        assert o % width == 0, (name, o, width)
        offs[name] = o
        o += width
    NP = _round_up(o, 1024)
    w_in_p = jnp.concatenate(pieces + [jnp.zeros((d, NP - (o - HL + H)), w_in.dtype)], axis=1).astype(BF16)

    wq3 = w_q_up.reshape(QL, MH, QK)
    wq_nope = wq3[:, :, :MLA_NOPE].reshape(QL, MH * MLA_NOPE)
    wq_rope = wq3[:, :, MLA_NOPE:]
    wq_rot = jnp.concatenate([-wq_rope[:, :, half:], wq_rope[:, :, :half]], axis=2)
    wq = jnp.concatenate([wq_nope, wq_rope.reshape(QL, MH * rope), wq_rot.reshape(QL, MH * rope)], axis=1).astype(BF16)

    wkv3 = w_kv_up.reshape(KVL, MH, MLA_NOPE + MLA_V_DIM)
    wkv = jnp.concatenate([wkv3[:, :, :MLA_NOPE].reshape(KVL, MH * MLA_NOPE),
                           wkv3[:, :, MLA_NOPE:].reshape(KVL, MH * MLA_V_DIM)], axis=1).astype(BF16)

    scale = QK ** -0.5
    zeros_h = jnp.zeros((rope,), F32)

    def gains(g, sc):
        gn = (g[:MLA_NOPE] * sc).reshape(1, LANES)
        gr = jnp.stack([jnp.concatenate([g[MLA_NOPE:] * sc, zeros_h]), jnp.concatenate([zeros_h, g[MLA_NOPE:] * sc])])
        return gn.astype(F32), gr.astype(F32)

    q_gn, q_gr = gains(q_norm, scale)
    k_gn, k_gr = gains(k_norm, 1.0)

    pad_h = lambda v: jnp.concatenate([v.astype(F32), jnp.zeros((HL - H,), F32)]).reshape(1, HL)
    head_of_col = np.arange(W) // P
    expand = (np.arange(HL)[:, None] == head_of_col[None, :]).astype(np.float32)

    return dict(
        H=H, P=P, G=G, N=N, HL=HL, QL=QL, KVL=KVL, MH=MH, QK=QK, ROPE=rope, NP=NP, offs=offs,
        norm_mix=norm_mix, w_in=w_in_p,
        cwx=conv_w[:, :W], cwb=conv_w[:, W:W + GN], cwc=conv_w[:, W + GN:],
        cbx=conv_b[:W].reshape(1, W), cbb=conv_b[W:W + GN].reshape(1, GN), cbc=conv_b[W + GN:].reshape(1, GN),
        dtb=pad_h(dt_bias), alog=pad_h(a_log),
        dskip_x=jnp.repeat(d_skip.astype(F32), P).reshape(1, W), ssd_gn=ssd_norm.reshape(1, W).astype(F32),
        expand=jnp.asarray(expand, BF16),
        q_a_norm=q_a_norm.reshape(1, QL).astype(F32), kv_a_norm=kv_a_norm.reshape(1, KVL).astype(F32),
        wq=wq, wkv=wkv, q_gn=q_gn, q_gr=q_gr, k_gn=k_gn, k_gr=k_gr,
        attn_out_norm=attn_out_norm, w_out_a=w_out[:W].astype(BF16), w_out_b=w_out[W:].astype(BF16),
        norm_ffn=norm_ffn, peer_w_q_t=_transpose_cast(peer_w_q), peer_keys=peer_keys.astype(BF16),
        peer_u=peer_u.astype(BF16), peer_vt=_transpose_cast(peer_v),
    )


def _rope_tables(pos, rope):
    inv = 1.0 / (ROPE_BASE ** (jnp.arange(0, rope, 2, dtype=F32) / rope))
    ang = pos.astype(F32)[:, None] * inv[None, :]
    reps = LANES // (rope // 2)
    return jnp.tile(jnp.cos(ang), (1, reps)), jnp.tile(jnp.sin(ang), (1, reps))


def _layer(x, pos, past_ckv, past_krope, conv_hist, ssm_h0, prm):
    b, l, d = x.shape
    t = b * l
    H, P, G, N = prm["H"], prm["P"], prm["G"], prm["N"]
    W, GN = H * P, G * N
    offs = prm["offs"]
    x2d = x.reshape(t, d)

    h = _rms_cast(x2d, prm["norm_mix"])
    proj = _matmul([(h, prm["w_in"])], name="in_proj")
    proj3 = proj.reshape(b, l, prm["NP"])

    hist = (conv_hist[:, :, :W], conv_hist[:, :, W:W + GN], conv_hist[:, :, W + GN:])
    y_ssd, h_last, nx, nb, ncm = _ssd(proj3, offs, hist, ssm_h0.reshape(b, W, N), prm, _pick(l, SSD_CHUNK, 8))
    conv_new = jnp.concatenate([nx, nb, ncm], axis=-1)
    h_last = h_last.reshape(b, H, P, N)

    cos4, sin4 = _rope_tables(pos, prm["ROPE"])
    c_new, kr2_new, kr_new = _latent(proj3, offs, prm, cos4, sin4)
    q = _q_proj(proj3, offs, prm, cos4, sin4)
    kv_new = _kv_proj(c_new, kr2_new, prm)
    if past_ckv is None:
        o = _attention(q, [kv_new], prm["MH"], tq=_pick(l, TQ_ATTN, CHUNK))
    else:
        lp = past_ckv.shape[1]
        assert l <= CHUNK and lp % CHUNK == 0
        kv_past = _kv_proj(past_ckv, jnp.concatenate([past_krope, past_krope], axis=-1), prm)
        o = _attention(q, [kv_past, kv_new], prm["MH"])
    o_n = _rms_cast(o.reshape(t, -1), prm["attn_out_norm"])

    x_mid = _matmul([(y_ssd.reshape(t, W), prm["w_out_a"]), (o_n, prm["w_out_b"])], res=x2d,
                    name="out_proj")

    hn_t = _rms_cast(x_mid, prm["norm_ffn"], transpose=True)
    pq_t = _matmul([(prm["peer_w_q_t"], hn_t)], tn=TN_PEER_QUERY, name="peer_query")
    sel = _peer_select(pq_t, prm["peer_keys"])
    yt = _peer_main(hn_t, prm["peer_u"], prm["peer_vt"], sel)
    out = _transpose_add(x_mid, yt)
    return out.reshape(b, l, d), c_new, kr_new, h_last.astype(ssm_h0.dtype), conv_new


def kernel(x_prompt, x_sample, cache_mla_ckv, cache_mla_krope, state_ssm, state_conv, norm_mix, w_in, conv_w, conv_b, dt_bias, a_log, d_skip, ssd_norm, q_a_norm, w_q_up, kv_a_norm, w_kv_up, q_norm, k_norm, attn_out_norm, w_out, norm_ffn, peer_w_q, peer_keys, peer_u, peer_v):
    bp, lp, _ = x_prompt.shape
    ls = x_sample.shape[1]
    depth = norm_mix.shape[0]
    past = cache_mla_ckv.shape[2]
    rope = cache_mla_krope.shape[-1]
    nstate = state_ssm.shape[-1]
    xbc = state_conv.shape[-1]
    pos_p = jnp.arange(lp, dtype=jnp.int32)
    pos_s = past + jnp.arange(ls, dtype=jnp.int32)
    yp, ys = x_prompt, x_sample
    outs_p, outs_s = [], []
    for layer in range(depth):
        prm = _prepare(norm_mix[layer], w_in[layer], conv_w[layer], conv_b[layer], dt_bias[layer], a_log[layer],
                       d_skip[layer], ssd_norm[layer], q_a_norm[layer], w_q_up[layer], kv_a_norm[layer],
                       w_kv_up[layer], q_norm[layer], k_norm[layer], attn_out_norm[layer], w_out[layer],
                       norm_ffn[layer], peer_w_q[layer], peer_keys[layer], peer_u[layer], peer_v[layer],
                       rope, nstate)
        zero_conv = jnp.zeros((bp, state_conv.shape[2], xbc), x_prompt.dtype)
        zero_ssm = jnp.zeros((bp,) + state_ssm.shape[2:], state_ssm.dtype)
        yp, *rest_p = _layer(yp, pos_p, None, None, zero_conv, zero_ssm, prm)
        outs_p.append(rest_p)
        ys, *rest_s = _layer(ys, pos_s, cache_mla_ckv[layer], cache_mla_krope[layer],
                             state_conv[layer], state_ssm[layer], prm)
        outs_s.append(rest_s)
    stack = lambda outs, i: jnp.stack([o[i] for o in outs])
    return (yp, ys,
            stack(outs_p, 0), stack(outs_p, 1), stack(outs_p, 2), stack(outs_p, 3),
            stack(outs_s, 0), stack(outs_s, 1), stack(outs_s, 2), stack(outs_s, 3))
```

```python
import functools
import math

import numpy as np
import jax
import jax.numpy as jnp
from jax import lax
from jax.experimental import pallas as pl
from jax.experimental.pallas import tpu as pltpu

F32 = jnp.float32
BF16 = jnp.bfloat16

EPS = 1e-6
CHUNK = 64
SSD_CHUNK = 128
SSD_HEAD_DIM = 64
MLA_NOPE = 128
MLA_V_DIM = 128
ROPE_BASE = 10000.0
PEER_TOPK = 16
LANES = 128
SUBLANES = 8
VMEM_LIMIT = 56 * 1024 * 1024

TM_MATMUL, TN_MATMUL = 1024, 1024
TN_PEER_QUERY = 1024
TM_ROWS = 256
TM_KV = 512
TQ_ATTN = 256
TT_SELECT = 256
TT_PEER, NE_PEER = 512, 512

NT_DIMS = (((1,), (1,)), ((), ()))
TN_DIMS = (((0,), (0,)), ((), ()))


def _pick(n, pref, mult):
    if n <= pref:
        return n
    t = (pref // mult) * mult
    while t >= mult:
        if n % t == 0:
            return t
        t -= mult
    return n


def _round_up(n, m):
    return (n + m - 1) // m * m


def _params(sem, vmem=VMEM_LIMIT):
    return pltpu.CompilerParams(dimension_semantics=sem, vmem_limit_bytes=vmem)


def _split3(x):
    hi = x.astype(BF16)
    r1 = x - hi.astype(F32)
    mid = r1.astype(BF16)
    lo = (r1 - mid.astype(F32)).astype(BF16)
    return hi, mid, lo


def _dot3(a01, x, dims=None):
    out = None
    for piece in _split3(x):
        if dims is None:
            d = jnp.dot(a01, piece, preferred_element_type=F32)
        else:
            d = lax.dot_general(a01, piece, dims, preferred_element_type=F32)
        out = d if out is None else out + d
    return out


def _dot3_right(x, b01, dims=None):
    out = None
    for piece in _split3(x):
        if dims is None:
            d = jnp.dot(piece, b01, preferred_element_type=F32)
        else:
            d = lax.dot_general(piece, b01, dims, preferred_element_type=F32)
        out = d if out is None else out + d
    return out


def _rms_cast_kernel(x_ref, g_ref, o_ref):
    x = x_ref[...].astype(F32)
    ms = jnp.mean(x * x, axis=-1, keepdims=True)
    o_ref[...] = (x * lax.rsqrt(ms + EPS) * g_ref[...]).astype(o_ref.dtype)


def _rms_cast_t_kernel(x_ref, g_ref, o_ref):
    x = x_ref[...].astype(F32)
    ms = jnp.mean(x * x, axis=-1, keepdims=True)
    o_ref[...] = (x * lax.rsqrt(ms + EPS) * g_ref[...]).T.astype(o_ref.dtype)


def _rms_cast(x2d, g, out_dtype=BF16, tm=TM_ROWS, transpose=False):
    t, d = x2d.shape
    tm = _pick(t, tm, LANES if transpose else 16)
    if transpose:
        body, out_spec, out_shape = _rms_cast_t_kernel, pl.BlockSpec((d, tm), lambda i: (0, i)), (d, t)
    else:
        body, out_spec, out_shape = _rms_cast_kernel, pl.BlockSpec((tm, d), lambda i: (i, 0)), (t, d)
    return pl.pallas_call(
        body,
        grid=(t // tm,),
        in_specs=[pl.BlockSpec((tm, d), lambda i: (i, 0)), pl.BlockSpec((1, d), lambda i: (0, 0))],
        out_specs=out_spec,
        out_shape=jax.ShapeDtypeStruct(out_shape, out_dtype),
        compiler_params=_params(("parallel",)),
        name="rms_cast_t" if transpose else "rms_cast",
    )(x2d, g.reshape(1, d).astype(F32))


def _transpose_cast_kernel(w_ref, o_ref):
    o_ref[...] = w_ref[...].T.astype(o_ref.dtype)


def _transpose_cast(w, out_dtype=BF16, tr=512, tc=1024):
    r, c = w.shape
    tr, tc = _pick(r, tr, LANES), _pick(c, tc, LANES)
    return pl.pallas_call(
        _transpose_cast_kernel,
        grid=(r // tr, c // tc),
        in_specs=[pl.BlockSpec((tr, tc), lambda i, j: (i, j))],
        out_specs=pl.BlockSpec((tc, tr), lambda i, j: (j, i)),
        out_shape=jax.ShapeDtypeStruct((c, r), out_dtype),
        compiler_params=_params(("parallel", "parallel")),
        name="transpose_cast",
    )(w)


def _mm_kernel(*refs, n_pairs, has_res):
    o_ref = refs[-1]
    acc = None
    for p in range(n_pairs):
        d = jnp.dot(refs[2 * p][...], refs[2 * p + 1][...], preferred_element_type=F32)
        acc = d if acc is None else acc + d
    if has_res:
        acc = refs[2 * n_pairs][...] + acc
    o_ref[...] = acc.astype(o_ref.dtype)


def _matmul(pairs, res=None, out_dtype=F32, tm=TM_MATMUL, tn=TN_MATMUL, name="matmul"):
    m = pairs[0][0].shape[0]
    n = pairs[0][1].shape[1]
    tm = _pick(m, tm, 16)
    tn = _pick(n, tn, LANES)
    in_specs, args = [], []
    for a, w in pairs:
        k = a.shape[1]
        in_specs += [pl.BlockSpec((tm, k), lambda i, j: (i, 0)), pl.BlockSpec((k, tn), lambda i, j: (0, j))]
        args += [a, w]
    if res is not None:
        in_specs.append(pl.BlockSpec((tm, tn), lambda i, j: (i, j)))
        args.append(res)
    return pl.pallas_call(
        functools.partial(_mm_kernel, n_pairs=len(pairs), has_res=res is not None),
        grid=(m // tm, n // tn),
        in_specs=in_specs,
        out_specs=pl.BlockSpec((tm, tn), lambda i, j: (i, j)),
        out_shape=jax.ShapeDtypeStruct((m, n), out_dtype),
        compiler_params=_params(("parallel", "arbitrary")),
        name=name,
    )(*args)


def _silu(x):
    hx = 0.5 * x
    return hx + hx * jnp.tanh(hx)


def _softplus(x):
    return jnp.maximum(x, 0.0) + jnp.log1p(jnp.exp(-jnp.abs(x)))


def _ssd_kernel(z_ref, xs_ref, bm_ref, cm_ref, dt_ref,
                hx_ref, hb_ref, hc_ref, h0_ref,
                cwx_ref, cwb_ref, cwc_ref, cbx_ref, cbb_ref, cbc_ref,
                dtb_ref, alog_ref, dskip_ref, gn_ref,
                tri_ref, exp_ref,
                y_ref, hlast_ref, nx_ref, nb_ref, nc_ref,
                bx_ref, bb_ref, bc_ref, state_ref, ydiag_ref,
                *, Q, H, P, G, N):
    c = pl.program_id(1)
    nc = pl.num_programs(1)
    R = H // G
    W = H * P

    @pl.when(c == 0)
    def _init():
        for buf, hist in ((bx_ref, hx_ref), (bb_ref, hb_ref), (bc_ref, hc_ref)):
            buf[0:8, :] = jnp.zeros((8, buf.shape[1]), F32)
            buf[5:8, :] = hist[0]
        state_ref[...] = h0_ref[0]

    def conv(buf, x_ref, cw_ref, cb_ref):
        x = x_ref[0]
        buf[8:8 + Q, :] = x
        win = buf[...]
        acc = None
        for k in range(3):
            tap = pltpu.roll(win, Q + 3 - k, 0)[0:Q, :] * cw_ref[k:k + 1, :]
            acc = tap if acc is None else acc + tap
        acc = acc + x * cw_ref[3:4, :]
        buf[0:8, :] = buf[Q:Q + 8, :]
        return _silu(cb_ref[...] + acc)

    xs = conv(bx_ref, xs_ref, cwx_ref, cbx_ref)
    bm = conv(bb_ref, bm_ref, cwb_ref, cbb_ref)
    cm = conv(bc_ref, cm_ref, cwc_ref, cbc_ref)

    @pl.when(c == nc - 1)
    def _tail():
        nx_ref[0] = bx_ref[5:8, :]
        nb_ref[0] = bb_ref[5:8, :]
        nc_ref[0] = bc_ref[5:8, :]

    dt = _softplus(dt_ref[0] + dtb_ref[...])
    a_neg = -jnp.exp(alog_ref[...])
    dta = dt * a_neg
    tri = tri_ref[...]
    acum = _dot3(tri, dta)
    eye = (lax.broadcasted_iota(jnp.int32, (Q, Q), 0) == lax.broadcasted_iota(jnp.int32, (Q, Q), 1)).astype(BF16)
    acum_t = _dot3_right(acum, eye, TN_DIMS)
    dt_t = _dot3_right(dt, eye, TN_DIMS)

    acum_x = _dot3_right(acum, exp_ref[...])
    dt_x = _dot3_right(dt, exp_ref[...])

    ii = lax.broadcasted_iota(jnp.int32, (Q, Q), 0)
    jj = lax.broadcasted_iota(jnp.int32, (Q, Q), 1)
    causal = ii >= jj
    xs_b = xs.astype(BF16)
    bm_b = bm.astype(BF16)
    cm_b = cm.astype(BF16)
    st_b = state_ref[...].astype(BF16)

    exp_acum_x = jnp.exp(acum_x)
    decay_x = jnp.exp(acum_x[Q - 1:Q, :] - acum_x) * dt_x
    xd_b = (xs * decay_x).astype(BF16)

    chunk_decay = jnp.exp(acum_t[:, Q - 1:Q])

    for g in range(G):
        cg = cm_b[:, g * N:(g + 1) * N]
        bg = bm_b[:, g * N:(g + 1) * N]
        cb = lax.dot_general(cg, bg, NT_DIMS, preferred_element_type=F32)
        for r in range(R):
            h = g * R + r
            dm = acum[:, h:h + 1] - acum_t[h:h + 1, :]
            lm = jnp.where(causal, jnp.exp(dm), 0.0)
            wm = (cb * lm * dt_t[h:h + 1, :]).astype(BF16)
            ydiag_ref[:, h * P:(h + 1) * P] = jnp.dot(wm, xs_b[:, h * P:(h + 1) * P],
                                                      preferred_element_type=F32)
        rows = slice(g * R * P, (g + 1) * R * P)
        y_off = lax.dot_general(cg, st_b[rows, :], NT_DIMS, preferred_element_type=F32)
        ydiag_ref[:, rows] = ydiag_ref[:, rows] + y_off * exp_acum_x[:, rows]
        s_new = lax.dot_general(xd_b[:, rows], bg, TN_DIMS, preferred_element_type=F32)
        for r in range(R):
            h = g * R + r
            hr = slice(h * P, (h + 1) * P)
            state_ref[hr, :] = chunk_decay[h:h + 1, :] * state_ref[hr, :] + s_new[r * P:(r + 1) * P, :]

    y = ydiag_ref[...] + dskip_ref[...] * xs
    yg = y * _silu(z_ref[0])
    ms = jnp.mean(yg * yg, axis=-1, keepdims=True)
    y_ref[0] = (yg * lax.rsqrt(ms + EPS) * gn_ref[...]).astype(y_ref.dtype)

    @pl.when(c == nc - 1)
    def _final():
        hlast_ref[0] = state_ref[...]


def _ssd(proj3, offs, hist, h0, prm, Q):
    b, l, _ = proj3.shape
    H, P, G, N = prm["H"], prm["P"], prm["G"], prm["N"]
    W, GN, HL = H * P, G * N, prm["HL"]
    nchunk = l // Q
    hx, hb, hc = hist
    full = lambda shape: pl.BlockSpec(shape, lambda i, j: (0,) * len(shape))
    per_b = lambda shape: pl.BlockSpec(shape, lambda i, j: (i,) + (0,) * (len(shape) - 1))
    col = lambda width, off: pl.BlockSpec((1, Q, width), lambda i, j, o=off // width: (i, j, o))
    in_specs = [
        col(W, offs["z"]), col(W, offs["xs"]), col(GN, offs["bm"]), col(GN, offs["cm"]), col(HL, offs["dt"]),
        per_b((1, 3, W)), per_b((1, 3, GN)), per_b((1, 3, GN)), per_b((1, W, N)),
        full((4, W)), full((4, GN)), full((4, GN)), full((1, W)), full((1, GN)), full((1, GN)),
        full((1, HL)), full((1, HL)), full((1, W)), full((1, W)),
        full((Q, Q)), full((HL, W)),
    ]
    out_specs = [
        pl.BlockSpec((1, Q, W), lambda i, j: (i, j, 0)),
        per_b((1, W, N)), per_b((1, 3, W)), per_b((1, 3, GN)), per_b((1, 3, GN)),
    ]
    out_shape = [
        jax.ShapeDtypeStruct((b, l, W), BF16),
        jax.ShapeDtypeStruct((b, W, N), F32),
        jax.ShapeDtypeStruct((b, 3, W), F32),
        jax.ShapeDtypeStruct((b, 3, GN), F32),
        jax.ShapeDtypeStruct((b, 3, GN), F32),
    ]
    scratch = [
        pltpu.VMEM((Q + 8, W), F32), pltpu.VMEM((Q + 8, GN), F32), pltpu.VMEM((Q + 8, GN), F32),
        pltpu.VMEM((W, N), F32), pltpu.VMEM((Q, W), F32),
    ]
    return pl.pallas_call(
        functools.partial(_ssd_kernel, Q=Q, H=H, P=P, G=G, N=N),
        grid=(b, nchunk),
        in_specs=in_specs, out_specs=out_specs, out_shape=out_shape, scratch_shapes=scratch,
        compiler_params=_params(("parallel", "arbitrary")),
        name="ssd_scan",
    )(proj3, proj3, proj3, proj3, proj3, hx, hb, hc, h0,
      prm["cwx"], prm["cwb"], prm["cwc"], prm["cbx"], prm["cbb"], prm["cbc"],
      prm["dtb"], prm["alog"], prm["dskip_x"], prm["ssd_gn"],
      jnp.asarray(np.tril(np.ones((Q, Q), np.float32)), BF16), prm["expand"])


def _latent_kernel(ckv_ref, krr_ref, krot_ref, g_ref, cos_ref, sin_ref, c_ref, kr2_ref, kr_ref, *, rope):
    x = ckv_ref[0]
    ms = jnp.mean(x * x, axis=-1, keepdims=True)
    c_ref[0] = x * lax.rsqrt(ms + EPS) * g_ref[...]
    kr2 = krr_ref[0] * cos_ref[...] + krot_ref[0] * sin_ref[...]
    kr2_ref[0] = kr2
    kr_ref[0] = kr2[:, :rope]


def _latent(proj3, offs, prm, cos4, sin4, tm=TM_ROWS):
    b, l, _ = proj3.shape
    kvl, rope = prm["KVL"], prm["ROPE"]
    tm = _pick(l, tm, 8)
    col = lambda width, off: pl.BlockSpec((1, tm, width), lambda i, j, o=off // width: (i, j, o))
    return pl.pallas_call(
        functools.partial(_latent_kernel, rope=rope),
        grid=(b, l // tm),
        in_specs=[col(kvl, offs["ckv"]), col(LANES, offs["krr"]), col(LANES, offs["krot"]),
                  pl.BlockSpec((1, kvl), lambda i, j: (0, 0)),
                  pl.BlockSpec((tm, LANES), lambda i, j: (j, 0)), pl.BlockSpec((tm, LANES), lambda i, j: (j, 0))],
        out_specs=[pl.BlockSpec((1, tm, kvl), lambda i, j: (i, j, 0)),
                   pl.BlockSpec((1, tm, LANES), lambda i, j: (i, j, 0)),
                   pl.BlockSpec((1, tm, rope), lambda i, j: (i, j, 0))],
        out_shape=[jax.ShapeDtypeStruct((b, l, kvl), F32), jax.ShapeDtypeStruct((b, l, LANES), F32),
                   jax.ShapeDtypeStruct((b, l, rope), F32)],
        compiler_params=_params(("parallel", "parallel")),
        name="mla_latent",
    )(proj3, proj3, proj3, prm["kv_a_norm"], cos4, sin4)


def _q_kernel(cq_ref, g_ref, w_ref, cos_ref, sin_ref, gn_ref, gr_ref, q_ref, *, MH, qk_dim):
    x = cq_ref[0]
    ms = jnp.mean(x * x, axis=-1, keepdims=True)
    xn = (x * lax.rsqrt(ms + EPS) * g_ref[...]).astype(BF16)
    qf = jnp.dot(xn, w_ref[...], preferred_element_type=F32)
    cos = cos_ref[...]
    sin = sin_ref[...]
    lane = lax.broadcasted_iota(jnp.int32, (1, LANES), 1)
    half_mask = [(lane < LANES // 2).astype(F32), (lane >= LANES // 2).astype(F32)]
    rope0 = MH * MLA_NOPE
    rot0 = rope0 + MH * (LANES // 2)
    for p in range(MH // 2):
        rp = (qf[:, rope0 + p * LANES: rope0 + (p + 1) * LANES] * cos
              + qf[:, rot0 + p * LANES: rot0 + (p + 1) * LANES] * sin)
        for e in (0, 1):
            h = 2 * p + e
            nope = qf[:, h * MLA_NOPE:(h + 1) * MLA_NOPE]
            rh = rp * half_mask[e]
            ssq = jnp.sum(nope * nope, axis=-1, keepdims=True) + jnp.sum(rh * rh, axis=-1, keepdims=True)
            inv = lax.rsqrt(ssq * (1.0 / qk_dim) + EPS)
            q_ref[0, :, h * 2 * LANES: h * 2 * LANES + LANES] = (nope * inv * gn_ref[...]).astype(q_ref.dtype)
            q_ref[0, :, h * 2 * LANES + LANES:(h + 1) * 2 * LANES] = (rh * inv * gr_ref[e:e + 1, :]).astype(q_ref.dtype)


def _q_proj(proj3, offs, prm, cos4, sin4, tm=TM_ROWS):
    b, l, _ = proj3.shape
    ql, mh = prm["QL"], prm["MH"]
    tm = _pick(l, tm, 16)
    wq = prm["wq"]
    return pl.pallas_call(
        functools.partial(_q_kernel, MH=mh, qk_dim=prm["QK"]),
        grid=(b, l // tm),
        in_specs=[pl.BlockSpec((1, tm, ql), lambda i, j, o=offs["cq"] // ql: (i, j, o)),
                  pl.BlockSpec((1, ql), lambda i, j: (0, 0)),
                  pl.BlockSpec(wq.shape, lambda i, j: (0, 0)),
                  pl.BlockSpec((tm, LANES), lambda i, j: (j, 0)), pl.BlockSpec((tm, LANES), lambda i, j: (j, 0)),
                  pl.BlockSpec((1, LANES), lambda i, j: (0, 0)), pl.BlockSpec((2, LANES), lambda i, j: (0, 0))],
        out_specs=pl.BlockSpec((1, tm, mh * 2 * LANES), lambda i, j: (i, j, 0)),
        out_shape=jax.ShapeDtypeStruct((b, l, mh * 2 * LANES), BF16),
        compiler_params=_params(("parallel", "parallel")),
        name="mla_q",
    )(proj3, prm["q_a_norm"], wq, cos4, sin4, prm["q_gn"], prm["q_gr"])


def _kv_kernel(c_ref, kr2_ref, w_ref, gn_ref, gr_ref, k_ref, v_ref, *, MH, qk_dim):
    c = c_ref[0].astype(BF16)
    kv = jnp.dot(c, w_ref[...], preferred_element_type=F32)
    kr2 = kr2_ref[0]
    lane = lax.broadcasted_iota(jnp.int32, (1, LANES), 1)
    kr_lo = kr2 * (lane < LANES // 2).astype(F32)
    kr_ss = jnp.sum(kr_lo * kr_lo, axis=-1, keepdims=True)
    for h in range(MH):
        nope = kv[:, h * MLA_NOPE:(h + 1) * MLA_NOPE]
        ssq = jnp.sum(nope * nope, axis=-1, keepdims=True) + kr_ss
        inv = lax.rsqrt(ssq * (1.0 / qk_dim) + EPS)
        k_ref[0, :, h * 2 * LANES: h * 2 * LANES + LANES] = (nope * inv * gn_ref[...]).astype(k_ref.dtype)
        k_ref[0, :, h * 2 * LANES + LANES:(h + 1) * 2 * LANES] = (kr2 * inv * gr_ref[h % 2:h % 2 + 1, :]).astype(k_ref.dtype)
    v_ref[0] = kv[:, MH * MLA_NOPE:].astype(v_ref.dtype)


def _kv_proj(c_all, kr2_all, prm, tm=TM_KV):
    b, lk, kvl = c_all.shape
    mh = prm["MH"]
    tm = _pick(lk, tm, 16)
    wkv = prm["wkv"]
    return pl.pallas_call(
        functools.partial(_kv_kernel, MH=mh, qk_dim=prm["QK"]),
        grid=(b, lk // tm),
        in_specs=[pl.BlockSpec((1, tm, kvl), lambda i, j: (i, j, 0)),
                  pl.BlockSpec((1, tm, LANES), lambda i, j: (i, j, 0)),
                  pl.BlockSpec(wkv.shape, lambda i, j: (0, 0)),
                  pl.BlockSpec((1, LANES), lambda i, j: (0, 0)), pl.BlockSpec((2, LANES), lambda i, j: (0, 0))],
        out_specs=[pl.BlockSpec((1, tm, mh * 2 * LANES), lambda i, j: (i, j, 0)),
                   pl.BlockSpec((1, tm, mh * MLA_V_DIM), lambda i, j: (i, j, 0))],
        out_shape=[jax.ShapeDtypeStruct((b, lk, mh * 2 * LANES), BF16),
                   jax.ShapeDtypeStruct((b, lk, mh * MLA_V_DIM), BF16)],
        compiler_params=_params(("parallel", "parallel")),
        name="mla_kv",
    )(c_all, kr2_all, wkv, prm["k_gn"], prm["k_gr"])


def _attn_causal_kernel(q_ref, k_ref, v_ref, o_ref, *, tq, lq):
    for qi in range(lq // tq):
        rows = slice(qi * tq, (qi + 1) * tq)
        q = q_ref[0, rows, :]
        past = qi * tq
        sd = lax.dot_general(q, k_ref[0, past:past + tq, :], NT_DIMS, preferred_element_type=F32)
        ri = lax.broadcasted_iota(jnp.int32, (tq, tq), 0) // CHUNK
        ci = lax.broadcasted_iota(jnp.int32, (tq, tq), 1) // CHUNK
        sd = jnp.where(ri >= ci, sd, -jnp.inf)
        m = jnp.max(sd, axis=-1, keepdims=True)
        if past > 0:
            sp = lax.dot_general(q, k_ref[0, 0:past, :], NT_DIMS, preferred_element_type=F32)
            m = jnp.maximum(m, jnp.max(sp, axis=-1, keepdims=True))
            pp = jnp.exp(sp - m)
            l = jnp.sum(pp, axis=-1, keepdims=True)
            acc = jnp.dot(pp.astype(BF16), v_ref[0, 0:past, :], preferred_element_type=F32)
        pd = jnp.exp(sd - m)
        ld = jnp.sum(pd, axis=-1, keepdims=True)
        accd = jnp.dot(pd.astype(BF16), v_ref[0, past:past + tq, :], preferred_element_type=F32)
        if past > 0:
            l, acc = l + ld, acc + accd
        else:
            l, acc = ld, accd
        o_ref[0, rows, :] = acc / l


def _attn_cached_kernel(q_ref, kc_ref, vc_ref, kn_ref, vn_ref, o_ref):
    q = q_ref[0]
    sc = lax.dot_general(q, kc_ref[0], NT_DIMS, preferred_element_type=F32)
    sn = lax.dot_general(q, kn_ref[0], NT_DIMS, preferred_element_type=F32)
    m = jnp.maximum(jnp.max(sc, axis=-1, keepdims=True), jnp.max(sn, axis=-1, keepdims=True))
    pc = jnp.exp(sc - m)
    pn = jnp.exp(sn - m)
    l = jnp.sum(pc, axis=-1, keepdims=True) + jnp.sum(pn, axis=-1, keepdims=True)
    acc = (jnp.dot(pc.astype(BF16), vc_ref[0], preferred_element_type=F32)
           + jnp.dot(pn.astype(BF16), vn_ref[0], preferred_element_type=F32))
    o_ref[0] = acc / l


def _attention(q, kvs, mh, tq=None):
    b, lq, _ = q.shape
    head = lambda rows, width: pl.BlockSpec((1, rows, width), lambda bi, h: (bi, 0, h))
    in_specs, args = [head(lq, 2 * LANES)], [q]
    for k, v in kvs:
        in_specs += [head(k.shape[1], 2 * LANES), head(v.shape[1], MLA_V_DIM)]
        args += [k, v]
    body = functools.partial(_attn_causal_kernel, tq=tq, lq=lq) if len(kvs) == 1 else _attn_cached_kernel
    return pl.pallas_call(
        body,
        grid=(b, mh),
        in_specs=in_specs,
        out_specs=head(lq, MLA_V_DIM),
        out_shape=jax.ShapeDtypeStruct((b, lq, mh * MLA_V_DIM), F32),
        compiler_params=_params(("parallel", "parallel")),
        name="mla_attention",
    )(*args)


def _oddeven_sort_pairs(lo, n):
    def merge(lo, n, r):
        step = r * 2
        if step < n:
            yield from merge(lo, n, step)
            yield from merge(lo + r, n, step)
            for i in range(lo + r, lo + n - r, step):
                yield (i, i + r)
        else:
            yield (lo, lo + r)
    if n > 1:
        m = n // 2
        yield from _oddeven_sort_pairs(lo, m)
        yield from _oddeven_sort_pairs(lo + m, m)
        yield from merge(lo, n, 1)


def _sort_desc(xs):
    xs = list(xs)
    for i, j in _oddeven_sort_pairs(0, len(xs)):
        xs[i], xs[j] = jnp.maximum(xs[i], xs[j]), jnp.minimum(xs[i], xs[j])
    return xs


def _bitonic_merge_desc(xs):
    xs = list(xs)
    n = len(xs)
    d = n // 2
    while d >= 1:
        for i in range(n):
            if i & d == 0:
                xs[i], xs[i + d] = jnp.maximum(xs[i], xs[i + d]), jnp.minimum(xs[i], xs[i + d])
        d //= 2
    return xs


def _merge_sublanes_top(xs, k):
    shift = SUBLANES // 2
    while shift >= 1:
        other = [pltpu.roll(x, shift, 0) for x in xs]
        if len(xs) < k:
            xs = _bitonic_merge_desc(xs + other[::-1])
        else:
            xs = _bitonic_merge_desc([jnp.maximum(xs[i], other[k - 1 - i]) for i in range(k)])
        shift //= 2
    return xs


def _top_sorted(x, k):
    groups = [x[i * SUBLANES:(i + 1) * SUBLANES, :] for i in range(x.shape[0] // SUBLANES)]
    return _merge_sublanes_top(_sort_desc(groups)[:k], k)


def _prefix_count(rows, test):
    w = jnp.where
    g8 = test(rows[7])
    g4 = test(w(g8, rows[11], rows[3]))
    g2 = test(w(g8, w(g4, rows[13], rows[9]), w(g4, rows[5], rows[1])))
    g1 = test(w(g8, w(g4, w(g2, rows[14], rows[12]), w(g2, rows[10], rows[8])),
                w(g4, w(g2, rows[6], rows[4]), w(g2, rows[2], rows[0]))))
    g16 = test(rows[15])
    return w(g8, 8.0, 0.0) + w(g4, 4.0, 0.0) + w(g2, 2.0, 0.0) + w(g1, 1.0, 0.0) + w(g16, 1.0, 0.0)


def _peer_select_kernel(pq_ref, keys_ref, c1_ref, e1_ref, r2_ref, e2_ref, *, PH, HALF, TOPK):
    assert TOPK == 2 * SUBLANES
    tt = pq_ref.shape[1]
    sub = lax.broadcasted_iota(jnp.int32, (SUBLANES, tt), 0)

    def by_sublane(rows):
        out = rows[-1]
        for s in range(len(rows) - 2, -1, -1):
            out = jnp.where(sub == s, rows[s], out)
        return out

    for h in range(PH):
        q1 = pq_ref[h * 2 * HALF: h * 2 * HALF + HALF, :].astype(BF16)
        q2 = pq_ref[h * 2 * HALF + HALF:(h + 1) * 2 * HALF, :].astype(BF16)
        s1 = jnp.dot(keys_ref[h, 0], q1, preferred_element_type=F32)
        s2 = jnp.dot(keys_ref[h, 1], q2, preferred_element_type=F32)
        v1 = _top_sorted(s1, TOPK)
        v2 = _top_sorted(s2, TOPK)
        v2_lo, v2_hi = by_sublane(v2[:SUBLANES]), by_sublane(v2[SUBLANES:])
        cands = [v1[0] + v2_lo, v1[0] + v2_hi, v1[1] + v2_lo, v1[2] + v2_lo, v1[3] + v2_lo, v1[4] + v2_lo,
                 by_sublane(v1[SUBLANES:]) + v2[0],
                 jnp.where(sub < 6,
                           by_sublane([v1[5], v1[5], v1[6], v1[6], v1[7], v1[7], v1[7], v1[7]])
                           + jnp.where(sub % 2 == 0, v2[0], v2[1]), -jnp.inf)]
        top = _merge_sublanes_top(_sort_desc(cands), TOPK)
        tau = top[TOPK - 1]
        m = top[0]
        zsum = None
        for c in cands:
            z = jnp.where(c >= tau, jnp.exp(c - m), 0.0)
            zsum = z if zsum is None else zsum + z
        zsum = jnp.sum(zsum, axis=0, keepdims=True)
        inv_z = 1.0 / zsum
        for g16 in range(s1.shape[0] // ROWS_BF16):
            ranks, gates = [], []
            for g8 in (2 * g16, 2 * g16 + 1):
                rows = slice(g8 * SUBLANES, (g8 + 1) * SUBLANES)
                a, b2 = s1[rows, :], s2[rows, :]
                count1 = _prefix_count(v2, lambda probe: a + probe >= tau)
                rank2 = _prefix_count(v2, lambda probe: probe > b2)
                c1_ref[h, rows, :] = count1
                e1_ref[h, rows, :] = jnp.exp(a - v1[0])
                ranks.append(rank2)
                gates.append(jnp.exp(b2 - v2[0]) * inv_z)
            rows16 = slice(g16 * ROWS_BF16, (g16 + 1) * ROWS_BF16)
            r2_ref[h, rows16, :] = jnp.concatenate(ranks, axis=0).astype(BF16)
            e2_ref[h, rows16, :] = jnp.concatenate(gates, axis=0).astype(BF16)


def _peer_select(pq_t, keys_b, tt=TT_SELECT):
    t = pq_t.shape[1]
    ph, _, nk, half = keys_b.shape
    tt = _pick(t, tt, LANES)
    big = lambda: pl.BlockSpec((ph, nk, tt), lambda i: (0, 0, i))
    shp = lambda dt: jax.ShapeDtypeStruct((ph, nk, t), dt)
    return pl.pallas_call(
        functools.partial(_peer_select_kernel, PH=ph, HALF=half, TOPK=PEER_TOPK),
        grid=(t // tt,),
        in_specs=[pl.BlockSpec((ph * 2 * half, tt), lambda i: (0, i)),
                  pl.BlockSpec(keys_b.shape, lambda i: (0, 0, 0, 0))],
        out_specs=[big(), big(), big(), big()],
        out_shape=[shp(F32), shp(F32), shp(BF16), shp(BF16)],
        compiler_params=_params(("parallel",)),
        name="peer_select",
    )(pq_t, keys_b)


def _gelu(x):
    return 0.5 * x * (1.0 + lax.erf(x * (1.0 / math.sqrt(2.0))))


ROWS_BF16 = 16
GATE_ROWS = 128


def _peer_main_kernel(hn_ref, u_ref, vt_ref, c1_ref, e1_ref, r2_ref, e2_ref, out_ref,
                      ata_ref, atb_ref, cta_ref, ctb_ref, *, PH, NK, NI, NJ):
    s = pl.program_id(0)
    tt = ata_ref.shape[1]

    @pl.when(s == 0)
    def _init():
        for ref in (ata_ref, atb_ref, cta_ref, ctb_ref):
            ref[...] = jnp.zeros(ref.shape, ref.dtype)

    @pl.when(lax.rem(jnp.maximum(s - 2, 0), NJ) == 0)
    def _zero_out():
        out_ref[...] = jnp.zeros(out_ref.shape, F32)

    def step(at_new, at_prev, ct_new, ct_prev):
        d = out_ref.shape[0]
        blk = lax.rem(jnp.maximum(s - 1, 0), NJ)

        def mix_rows(m0, m1):
            out_ref[m0:m1, :] += jnp.dot(vt_ref[m0:m1, :], ct_prev[...], preferred_element_type=F32)

        def score_block(m0, m1, n0, n1):
            at_new[m0:m1, n0:n1] = jnp.dot(u_ref[m0:m1, :], hn_ref[:, n0:n1], preferred_element_type=F32)

        def gate_tile(ii, tc, r0):
            row = blk * NI + ii
            cols = slice(tc * LANES, (tc + 1) * LANES)
            groups = [slice(r0 + k * ROWS_BF16, r0 + (k + 1) * ROWS_BF16) for k in range(GATE_ROWS // ROWS_BF16)]
            g = [None] * len(groups)
            zero = jnp.zeros((ROWS_BF16, LANES), BF16)
            for h in range(PH):
                bcast = lambda r: jnp.broadcast_to(r[:, cols], (ROWS_BF16, LANES)).astype(BF16)
                c1b = bcast(c1_ref[h, pl.ds(row, 1), :])
                e1b = bcast(e1_ref[h, pl.ds(row, 1), :])
                for k, rws in enumerate(groups):
                    term = e1b * jnp.where(r2_ref[h, rws, cols] < c1b, e2_ref[h, rws, cols], zero)
                    g[k] = term if g[k] is None else g[k] + term
            for k, rws in enumerate(groups):
                orow = slice(ii * NK + rws.start, ii * NK + rws.stop)
                ct_new[orow, cols] = _gelu(at_prev[orow, cols]).astype(BF16) * g[k]

        tiles = [(ii, tc, r0) for ii in range(NI) for tc in range(tt // LANES) for r0 in range(0, NK, GATE_ROWS)]
        n_mix = 4
        n_sn = 1
        n_sm = 1
        ne = at_new.shape[0]
        mxu_items = [functools.partial(mix_rows, k * d // n_mix, (k + 1) * d // n_mix) for k in range(n_mix)]
        mxu_items += [functools.partial(score_block, km * ne // n_sm, (km + 1) * ne // n_sm,
                                        kn * tt // n_sn, (kn + 1) * tt // n_sn)
                      for kn in range(n_sn) for km in range(n_sm)]
        weights = [n_sn * n_sm] * n_mix + [n_mix] * (n_sn * n_sm)
        done, total = 0, sum(weights)
        for item, w in zip(mxu_items, weights):
            upto = len(tiles) * (done + w) // total
            for tile_args in tiles[len(tiles) * done // total: upto]:
                gate_tile(*tile_args)
            item()
            done += w

    @pl.when(s % 2 == 0)
    def _even():
        step(ata_ref, atb_ref, ctb_ref, cta_ref)

    @pl.when(s % 2 == 1)
    def _odd():
        step(atb_ref, ata_ref, cta_ref, ctb_ref)


def _peer_main(hn_t, u_b, vt_b, sel, tt=TT_PEER, ne=NE_PEER):
    d, t = hn_t.shape
    c1, e1, r2, e2 = sel
    ph, nk, _ = c1.shape
    nexp = u_b.shape[0]
    tt = _pick(t, tt, LANES)
    ne = _pick(nexp, ne, nk)
    ni = ne // nk
    nj = nexp // ne
    n_tiles = t // tt
    n_blocks = n_tiles * nj
    assert nj % 2 == 0
    once = dict(pipeline_mode=pl.Buffered(1))
    tile = lambda s, lag: jnp.minimum(jnp.maximum(s - lag, 0) // nj, n_tiles - 1)
    big = lambda: pl.BlockSpec((ph, nk, tt), lambda s: (0, 0, tile(s, 1)), **once)
    return pl.pallas_call(
        functools.partial(_peer_main_kernel, PH=ph, NK=nk, NI=ni, NJ=nj),
        grid=(n_blocks + 2,),
        in_specs=[pl.BlockSpec((d, tt), lambda s: (0, tile(s, 0)), **once),
                  pl.BlockSpec((ne, d), lambda s: (lax.rem(s, nj), 0)),
                  pl.BlockSpec((d, ne), lambda s: (0, lax.rem(jnp.maximum(s - 2, 0), nj))),
                  big(), big(), big(), big()],
        out_specs=pl.BlockSpec((d, tt), lambda s: (0, tile(s, 2))),
        out_shape=jax.ShapeDtypeStruct((d, t), F32),
        scratch_shapes=[pltpu.VMEM((ne, tt), F32), pltpu.VMEM((ne, tt), F32),
                        pltpu.VMEM((ne, tt), BF16), pltpu.VMEM((ne, tt), BF16)],
        compiler_params=_params(("arbitrary",)),
        name="peer_main",
    )(hn_t, u_b, vt_b, c1, e1, r2, e2)


def _tadd_kernel(x_ref, yt_ref, o_ref):
    o_ref[...] = x_ref[...] + yt_ref[...].T


def _transpose_add(x2d, yt, tm=TM_ROWS):
    t, d = x2d.shape
    tm = _pick(t, tm, LANES)
    return pl.pallas_call(
        _tadd_kernel,
        grid=(t // tm,),
        in_specs=[pl.BlockSpec((tm, d), lambda i: (i, 0)), pl.BlockSpec((d, tm), lambda i: (0, i))],
        out_specs=pl.BlockSpec((tm, d), lambda i: (i, 0)),
        out_shape=jax.ShapeDtypeStruct((t, d), F32),
        compiler_params=_params(("parallel",)),
        name="peer_residual",
    )(x2d, yt)


def _prepare(norm_mix, w_in, conv_w, conv_b, dt_bias, a_log, d_skip, ssd_norm,
             q_a_norm, w_q_up, kv_a_norm, w_kv_up, q_norm, k_norm, attn_out_norm,
             w_out, norm_ffn, peer_w_q, peer_keys, peer_u, peer_v, rope, N):
    d = w_in.shape[0]
    W = ssd_norm.shape[0]
    H = dt_bias.shape[0]
    P = W // H
    xbc = conv_w.shape[1]
    GN = (xbc - W) // 2
    G = GN // N
    QL = q_a_norm.shape[0]
    KVL = kv_a_norm.shape[0]
    MW = attn_out_norm.shape[0]
    MH = MW // MLA_V_DIM
    QK = MLA_NOPE + rope
    HL = _round_up(H, LANES)
    half = rope // 2
    assert 2 * rope == LANES and MH % 2 == 0 and P == SSD_HEAD_DIM

    s = np.cumsum([0, W, xbc, H, QL, KVL, rope])
    wz, wxbc, wdt, wcq, wckv, wkr = (w_in[:, s[i]:s[i + 1]] for i in range(6))
    wrot = jnp.concatenate([-wkr[:, half:], wkr[:, :half]], axis=1)
    pieces = [wz, wxbc, wcq, wckv, wkr, wkr, wrot, wrot, wdt]
    offs, o = {}, 0
    for name, width in (("z", W), ("xs", W), ("bm", GN), ("cm", GN), ("cq", QL), ("ckv", KVL),
                        ("krr", LANES), ("krot", LANES), ("dt", HL)):
        assert o % width == 0, (name, o, width)
        offs[name] = o
        o += width
    NP = _round_up(o, 1024)
    w_in_p = jnp.concatenate(pieces + [jnp.zeros((d, NP - (o - HL + H)), w_in.dtype)], axis=1).astype(BF16)

    wq3 = w_q_up.reshape(QL, MH, QK)
    wq_nope = wq3[:, :, :MLA_NOPE].reshape(QL, MH * MLA_NOPE)
    wq_rope = wq3[:, :, MLA_NOPE:]
    wq_rot = jnp.concatenate([-wq_rope[:, :, half:], wq_rope[:, :, :half]], axis=2)
    wq = jnp.concatenate([wq_nope, wq_rope.reshape(QL, MH * rope), wq_rot.reshape(QL, MH * rope)], axis=1).astype(BF16)

    wkv3 = w_kv_up.reshape(KVL, MH, MLA_NOPE + MLA_V_DIM)
    wkv = jnp.concatenate([wkv3[:, :, :MLA_NOPE].reshape(KVL, MH * MLA_NOPE),
                           wkv3[:, :, MLA_NOPE:].reshape(KVL, MH * MLA_V_DIM)], axis=1).astype(BF16)

    scale = QK ** -0.5
    zeros_h = jnp.zeros((rope,), F32)

    def gains(g, sc):
        gn = (g[:MLA_NOPE] * sc).reshape(1, LANES)
        gr = jnp.stack([jnp.concatenate([g[MLA_NOPE:] * sc, zeros_h]), jnp.concatenate([zeros_h, g[MLA_NOPE:] * sc])])
        return gn.astype(F32), gr.astype(F32)

    q_gn, q_gr = gains(q_norm, scale)
    k_gn, k_gr = gains(k_norm, 1.0)

    pad_h = lambda v: jnp.concatenate([v.astype(F32), jnp.zeros((HL - H,), F32)]).reshape(1, HL)
    head_of_col = np.arange(W) // P
    expand = (np.arange(HL)[:, None] == head_of_col[None, :]).astype(np.float32)

    return dict(
        H=H, P=P, G=G, N=N, HL=HL, QL=QL, KVL=KVL, MH=MH, QK=QK, ROPE=rope, NP=NP, offs=offs,
        norm_mix=norm_mix, w_in=w_in_p,
        cwx=conv_w[:, :W], cwb=conv_w[:, W:W + GN], cwc=conv_w[:, W + GN:],
        cbx=conv_b[:W].reshape(1, W), cbb=conv_b[W:W + GN].reshape(1, GN), cbc=conv_b[W + GN:].reshape(1, GN),
        dtb=pad_h(dt_bias), alog=pad_h(a_log),
        dskip_x=jnp.repeat(d_skip.astype(F32), P).reshape(1, W), ssd_gn=ssd_norm.reshape(1, W).astype(F32),
        expand=jnp.asarray(expand, BF16),
        q_a_norm=q_a_norm.reshape(1, QL).astype(F32), kv_a_norm=kv_a_norm.reshape(1, KVL).astype(F32),
        wq=wq, wkv=wkv, q_gn=q_gn, q_gr=q_gr, k_gn=k_gn, k_gr=k_gr,
        attn_out_norm=attn_out_norm, w_out_a=w_out[:W].astype(BF16), w_out_b=w_out[W:].astype(BF16),
        norm_ffn=norm_ffn, peer_w_q_t=_transpose_cast(peer_w_q), peer_keys=peer_keys.astype(BF16),
        peer_u=peer_u.astype(BF16), peer_vt=_transpose_cast(peer_v),
    )


def _rope_tables(pos, rope):
    inv = 1.0 / (ROPE_BASE ** (jnp.arange(0, rope, 2, dtype=F32) / rope))
    ang = pos.astype(F32)[:, None] * inv[None, :]
    reps = LANES // (rope // 2)
    return jnp.tile(jnp.cos(ang), (1, reps)), jnp.tile(jnp.sin(ang), (1, reps))


def _layer(x, pos, past_ckv, past_krope, conv_hist, ssm_h0, prm):
    b, l, d = x.shape
    t = b * l
    H, P, G, N = prm["H"], prm["P"], prm["G"], prm["N"]
    W, GN = H * P, G * N
    offs = prm["offs"]
    x2d = x.reshape(t, d)

    h = _rms_cast(x2d, prm["norm_mix"])
    proj = _matmul([(h, prm["w_in"])], name="in_proj")
    proj3 = proj.reshape(b, l, prm["NP"])

    hist = (conv_hist[:, :, :W], conv_hist[:, :, W:W + GN], conv_hist[:, :, W + GN:])
    y_ssd, h_last, nx, nb, ncm = _ssd(proj3, offs, hist, ssm_h0.reshape(b, W, N), prm, _pick(l, SSD_CHUNK, 8))
    conv_new = jnp.concatenate([nx, nb, ncm], axis=-1)
    h_last = h_last.reshape(b, H, P, N)

    cos4, sin4 = _rope_tables(pos, prm["ROPE"])
    c_new, kr2_new, kr_new = _latent(proj3, offs, prm, cos4, sin4)
    q = _q_proj(proj3, offs, prm, cos4, sin4)
    kv_new = _kv_proj(c_new, kr2_new, prm)
    if past_ckv is None:
        o = _attention(q, [kv_new], prm["MH"], tq=_pick(l, TQ_ATTN, CHUNK))
    else:
        lp = past_ckv.shape[1]
        assert l <= CHUNK and lp % CHUNK == 0
        kv_past = _kv_proj(past_ckv, jnp.concatenate([past_krope, past_krope], axis=-1), prm)
        o = _attention(q, [kv_past, kv_new], prm["MH"])
    o_n = _rms_cast(o.reshape(t, -1), prm["attn_out_norm"])

    x_mid = _matmul([(y_ssd.reshape(t, W), prm["w_out_a"]), (o_n, prm["w_out_b"])], res=x2d,
                    name="out_proj")

    hn_t = _rms_cast(x_mid, prm["norm_ffn"], transpose=True)
    pq_t = _matmul([(prm["peer_w_q_t"], hn_t)], tn=TN_PEER_QUERY, name="peer_query")
    sel = _peer_select(pq_t, prm["peer_keys"])
    yt = _peer_main(hn_t, prm["peer_u"], prm["peer_vt"], sel)
    out = _transpose_add(x_mid, yt)
    return out.reshape(b, l, d), c_new, kr_new, h_last.astype(ssm_h0.dtype), conv_new


def kernel(x_prompt, x_sample, cache_mla_ckv, cache_mla_krope, state_ssm, state_conv, norm_mix, w_in, conv_w, conv_b, dt_bias, a_log, d_skip, ssd_norm, q_a_norm, w_q_up, kv_a_norm, w_kv_up, q_norm, k_norm, attn_out_norm, w_out, norm_ffn, peer_w_q, peer_keys, peer_u, peer_v):
    bp, lp, _ = x_prompt.shape
    ls = x_sample.shape[1]
    depth = norm_mix.shape[0]
    past = cache_mla_ckv.shape[2]
    rope = cache_mla_krope.shape[-1]
    nstate = state_ssm.shape[-1]
    xbc = state_conv.shape[-1]
    pos_p = jnp.arange(lp, dtype=jnp.int32)
    pos_s = past + jnp.arange(ls, dtype=jnp.int32)
    yp, ys = x_prompt, x_sample
    outs_p, outs_s = [], []
    for layer in range(depth):
        prm = _prepare(norm_mix[layer], w_in[layer], conv_w[layer], conv_b[layer], dt_bias[layer], a_log[layer],
                       d_skip[layer], ssd_norm[layer], q_a_norm[layer], w_q_up[layer], kv_a_norm[layer],
                       w_kv_up[layer], q_norm[layer], k_norm[layer], attn_out_norm[layer], w_out[layer],
                       norm_ffn[layer], peer_w_q[layer], peer_keys[layer], peer_u[layer], peer_v[layer],
                       rope, nstate)
        zero_conv = jnp.zeros((bp, state_conv.shape[2], xbc), x_prompt.dtype)
        zero_ssm = jnp.zeros((bp,) + state_ssm.shape[2:], state_ssm.dtype)
        yp, *rest_p = _layer(yp, pos_p, None, None, zero_conv, zero_ssm, prm)
        outs_p.append(rest_p)
        ys, *rest_s = _layer(ys, pos_s, cache_mla_ckv[layer], cache_mla_krope[layer],
                             state_conv[layer], state_ssm[layer], prm)
        outs_s.append(rest_s)
    stack = lambda outs, i: jnp.stack([o[i] for o in outs])
    return (yp, ys,
            stack(outs_p, 0), stack(outs_p, 1), stack(outs_p, 2), stack(outs_p, 3),
            stack(outs_s, 0), stack(outs_s, 1), stack(outs_s, 2), stack(outs_s, 3))
```

```python
import functools
import math

import numpy as np
import jax
import jax.numpy as jnp
from jax import lax
from jax.experimental import pallas as pl
from jax.experimental.pallas import tpu as pltpu

F32 = jnp.float32
BF16 = jnp.bfloat16

EPS = 1e-6
CHUNK = 64
SSD_CHUNK = 128
SSD_HEAD_DIM = 64
MLA_NOPE = 128
MLA_V_DIM = 128
ROPE_BASE = 10000.0
PEER_TOPK = 16
LANES = 128
SUBLANES = 8
VMEM_LIMIT = 56 * 1024 * 1024

TM_MATMUL, TN_MATMUL = 1024, 1024
TN_PEER_QUERY = 1024
TM_ROWS = 256
TM_KV = 512
TQ_ATTN = 256
TT_SELECT = 256
TT_PEER, NE_PEER = 512, 512

NT_DIMS = (((1,), (1,)), ((), ()))
TN_DIMS = (((0,), (0,)), ((), ()))


def _pick(n, pref, mult):
    if n <= pref:
        return n
    t = (pref // mult) * mult
    while t >= mult:
        if n % t == 0:
            return t
        t -= mult
    return n


def _round_up(n, m):
    return (n + m - 1) // m * m


def _params(sem, vmem=VMEM_LIMIT):
    return pltpu.CompilerParams(dimension_semantics=sem, vmem_limit_bytes=vmem)


def _split3(x):
    hi = x.astype(BF16)
    r1 = x - hi.astype(F32)
    mid = r1.astype(BF16)
    lo = (r1 - mid.astype(F32)).astype(BF16)
    return hi, mid, lo


def _dot3(a01, x, dims=None):
    out = None
    for piece in _split3(x):
        if dims is None:
            d = jnp.dot(a01, piece, preferred_element_type=F32)
        else:
            d = lax.dot_general(a01, piece, dims, preferred_element_type=F32)
        out = d if out is None else out + d
    return out


def _dot3_right(x, b01, dims=None):
    out = None
    for piece in _split3(x):
        if dims is None:
            d = jnp.dot(piece, b01, preferred_element_type=F32)
        else:
            d = lax.dot_general(piece, b01, dims, preferred_element_type=F32)
        out = d if out is None else out + d
    return out


def _rms_cast_kernel(x_ref, g_ref, o_ref):
    x = x_ref[...].astype(F32)
    ms = jnp.mean(x * x, axis=-1, keepdims=True)
    o_ref[...] = (x * lax.rsqrt(ms + EPS) * g_ref[...]).astype(o_ref.dtype)


def _rms_cast_t_kernel(x_ref, g_ref, o_ref):
    x = x_ref[...].astype(F32)
    ms = jnp.mean(x * x, axis=-1, keepdims=True)
    o_ref[...] = (x * lax.rsqrt(ms + EPS) * g_ref[...]).T.astype(o_ref.dtype)


def _rms_cast(x2d, g, out_dtype=BF16, tm=TM_ROWS, transpose=False):
    t, d = x2d.shape
    tm = _pick(t, tm, LANES if transpose else 16)
    if transpose:
        body, out_spec, out_shape = _rms_cast_t_kernel, pl.BlockSpec((d, tm), lambda i: (0, i)), (d, t)
    else:
        body, out_spec, out_shape = _rms_cast_kernel, pl.BlockSpec((tm, d), lambda i: (i, 0)), (t, d)
    return pl.pallas_call(
        body,
        grid=(t // tm,),
        in_specs=[pl.BlockSpec((tm, d), lambda i: (i, 0)), pl.BlockSpec((1, d), lambda i: (0, 0))],
        out_specs=out_spec,
        out_shape=jax.ShapeDtypeStruct(out_shape, out_dtype),
        compiler_params=_params(("parallel",)),
        name="rms_cast_t" if transpose else "rms_cast",
    )(x2d, g.reshape(1, d).astype(F32))


def _transpose_cast_kernel(w_ref, o_ref):
    o_ref[...] = w_ref[...].T.astype(o_ref.dtype)


def _transpose_cast(w, out_dtype=BF16, tr=512, tc=1024):
    r, c = w.shape
    tr, tc = _pick(r, tr, LANES), _pick(c, tc, LANES)
    return pl.pallas_call(
        _transpose_cast_kernel,
        grid=(r // tr, c // tc),
        in_specs=[pl.BlockSpec((tr, tc), lambda i, j: (i, j))],
        out_specs=pl.BlockSpec((tc, tr), lambda i, j: (j, i)),
        out_shape=jax.ShapeDtypeStruct((c, r), out_dtype),
        compiler_params=_params(("parallel", "parallel")),
        name="transpose_cast",
    )(w)


def _mm_kernel(*refs, n_pairs, has_res):
    o_ref = refs[-1]
    acc = None
    for p in range(n_pairs):
        d = jnp.dot(refs[2 * p][...], refs[2 * p + 1][...], preferred_element_type=F32)
        acc = d if acc is None else acc + d
    if has_res:
        acc = refs[2 * n_pairs][...] + acc
    o_ref[...] = acc.astype(o_ref.dtype)


def _matmul(pairs, res=None, out_dtype=F32, tm=TM_MATMUL, tn=TN_MATMUL, name="matmul"):
    m = pairs[0][0].shape[0]
    n = pairs[0][1].shape[1]
    tm = _pick(m, tm, 16)
    tn = _pick(n, tn, LANES)
    in_specs, args = [], []
    for a, w in pairs:
        k = a.shape[1]
        in_specs += [pl.BlockSpec((tm, k), lambda i, j: (i, 0)), pl.BlockSpec((k, tn), lambda i, j: (0, j))]
        args += [a, w]
    if res is not None:
        in_specs.append(pl.BlockSpec((tm, tn), lambda i, j: (i, j)))
        args.append(res)
    return pl.pallas_call(
        functools.partial(_mm_kernel, n_pairs=len(pairs), has_res=res is not None),
        grid=(m // tm, n // tn),
        in_specs=in_specs,
        out_specs=pl.BlockSpec((tm, tn), lambda i, j: (i, j)),
        out_shape=jax.ShapeDtypeStruct((m, n), out_dtype),
        compiler_params=_params(("parallel", "arbitrary")),
        name=name,
    )(*args)


def _silu(x):
    hx = 0.5 * x
    return hx + hx * jnp.tanh(hx)


def _softplus(x):
    return jnp.maximum(x, 0.0) + jnp.log1p(jnp.exp(-jnp.abs(x)))


def _ssd_kernel(z_ref, xs_ref, bm_ref, cm_ref, dt_ref,
                hx_ref, hb_ref, hc_ref, h0_ref,
                cwx_ref, cwb_ref, cwc_ref, cbx_ref, cbb_ref, cbc_ref,
                dtb_ref, alog_ref, dskip_ref, gn_ref,
                tri_ref, exp_ref,
                y_ref, hlast_ref, nx_ref, nb_ref, nc_ref,
                bx_ref, bb_ref, bc_ref, state_ref, ydiag_ref,
                *, Q, H, P, G, N):
    c = pl.program_id(1)
    nc = pl.num_programs(1)
    R = H // G
    W = H * P

    @pl.when(c == 0)
    def _init():
        for buf, hist in ((bx_ref, hx_ref), (bb_ref, hb_ref), (bc_ref, hc_ref)):
            buf[0:8, :] = jnp.zeros((8, buf.shape[1]), F32)
            buf[5:8, :] = hist[0]
        state_ref[...] = h0_ref[0]

    def conv(buf, x_ref, cw_ref, cb_ref):
        x = x_ref[0]
        buf[8:8 + Q, :] = x
        win = buf[...]
        acc = None
        for k in range(3):
            tap = pltpu.roll(win, Q + 3 - k, 0)[0:Q, :] * cw_ref[k:k + 1, :]
            acc = tap if acc is None else acc + tap
        acc = acc + x * cw_ref[3:4, :]
        buf[0:8, :] = buf[Q:Q + 8, :]
        return _silu(cb_ref[...] + acc)

    xs = conv(bx_ref, xs_ref, cwx_ref, cbx_ref)
    bm = conv(bb_ref, bm_ref, cwb_ref, cbb_ref)
    cm = conv(bc_ref, cm_ref, cwc_ref, cbc_ref)

    @pl.when(c == nc - 1)
    def _tail():
        nx_ref[0] = bx_ref[5:8, :]
        nb_ref[0] = bb_ref[5:8, :]
        nc_ref[0] = bc_ref[5:8, :]

    dt = _softplus(dt_ref[0] + dtb_ref[...])
    a_neg = -jnp.exp(alog_ref[...])
    dta = dt * a_neg
    tri = tri_ref[...]
    acum = _dot3(tri, dta)
    eye = (lax.broadcasted_iota(jnp.int32, (Q, Q), 0) == lax.broadcasted_iota(jnp.int32, (Q, Q), 1)).astype(BF16)
    acum_t = _dot3_right(acum, eye, TN_DIMS)
    dt_t = _dot3_right(dt, eye, TN_DIMS)

    acum_x = _dot3_right(acum, exp_ref[...])
    dt_x = _dot3_right(dt, exp_ref[...])

    ii = lax.broadcasted_iota(jnp.int32, (Q, Q), 0)
    jj = lax.broadcasted_iota(jnp.int32, (Q, Q), 1)
    causal = ii >= jj
    xs_b = xs.astype(BF16)
    bm_b = bm.astype(BF16)
    cm_b = cm.astype(BF16)
    st_b = state_ref[...].astype(BF16)

    exp_acum_x = jnp.exp(acum_x)
    decay_x = jnp.exp(acum_x[Q - 1:Q, :] - acum_x) * dt_x
    xd_b = (xs * decay_x).astype(BF16)

    chunk_decay = jnp.exp(acum_t[:, Q - 1:Q])

    for g in range(G):
        cg = cm_b[:, g * N:(g + 1) * N]
        bg = bm_b[:, g * N:(g + 1) * N]
        cb = lax.dot_general(cg, bg, NT_DIMS, preferred_element_type=F32)
        for r in range(R):
            h = g * R + r
            dm = acum[:, h:h + 1] - acum_t[h:h + 1, :]
            lm = jnp.where(causal, jnp.exp(dm), 0.0)
            wm = (cb * lm * dt_t[h:h + 1, :]).astype(BF16)
            ydiag_ref[:, h * P:(h + 1) * P] = jnp.dot(wm, xs_b[:, h * P:(h + 1) * P],
                                                      preferred_element_type=F32)
        rows = slice(g * R * P, (g + 1) * R * P)
        y_off = lax.dot_general(cg, st_b[rows, :], NT_DIMS, preferred_element_type=F32)
        ydiag_ref[:, rows] = ydiag_ref[:, rows] + y_off * exp_acum_x[:, rows]
        s_new = lax.dot_general(xd_b[:, rows], bg, TN_DIMS, preferred_element_type=F32)
        for r in range(R):
            h = g * R + r
            hr = slice(h * P, (h + 1) * P)
            state_ref[hr, :] = chunk_decay[h:h + 1, :] * state_ref[hr, :] + s_new[r * P:(r + 1) * P, :]

    y = ydiag_ref[...] + dskip_ref[...] * xs
    yg = y * _silu(z_ref[0])
    ms = jnp.mean(yg * yg, axis=-1, keepdims=True)
    y_ref[0] = (yg * lax.rsqrt(ms + EPS) * gn_ref[...]).astype(y_ref.dtype)

    @pl.when(c == nc - 1)
    def _final():
        hlast_ref[0] = state_ref[...]


def _ssd(proj3, offs, hist, h0, prm, Q):
    b, l, _ = proj3.shape
    H, P, G, N = prm["H"], prm["P"], prm["G"], prm["N"]
    W, GN, HL = H * P, G * N, prm["HL"]
    nchunk = l // Q
    hx, hb, hc = hist
    full = lambda shape: pl.BlockSpec(shape, lambda i, j: (0,) * len(shape))
    per_b = lambda shape: pl.BlockSpec(shape, lambda i, j: (i,) + (0,) * (len(shape) - 1))
    col = lambda width, off: pl.BlockSpec((1, Q, width), lambda i, j, o=off // width: (i, j, o))
    in_specs = [
        col(W, offs["z"]), col(W, offs["xs"]), col(GN, offs["bm"]), col(GN, offs["cm"]), col(HL, offs["dt"]),
        per_b((1, 3, W)), per_b((1, 3, GN)), per_b((1, 3, GN)), per_b((1, W, N)),
        full((4, W)), full((4, GN)), full((4, GN)), full((1, W)), full((1, GN)), full((1, GN)),
        full((1, HL)), full((1, HL)), full((1, W)), full((1, W)),
        full((Q, Q)), full((HL, W)),
    ]
    out_specs = [
        pl.BlockSpec((1, Q, W), lambda i, j: (i, j, 0)),
        per_b((1, W, N)), per_b((1, 3, W)), per_b((1, 3, GN)), per_b((1, 3, GN)),
    ]
    out_shape = [
        jax.ShapeDtypeStruct((b, l, W), BF16),
        jax.ShapeDtypeStruct((b, W, N), F32),
        jax.ShapeDtypeStruct((b, 3, W), F32),
        jax.ShapeDtypeStruct((b, 3, GN), F32),
        jax.ShapeDtypeStruct((b, 3, GN), F32),
    ]
    scratch = [
        pltpu.VMEM((Q + 8, W), F32), pltpu.VMEM((Q + 8, GN), F32), pltpu.VMEM((Q + 8, GN), F32),
        pltpu.VMEM((W, N), F32), pltpu.VMEM((Q, W), F32),
    ]
    return pl.pallas_call(
        functools.partial(_ssd_kernel, Q=Q, H=H, P=P, G=G, N=N),
        grid=(b, nchunk),
        in_specs=in_specs, out_specs=out_specs, out_shape=out_shape, scratch_shapes=scratch,
        compiler_params=_params(("parallel", "arbitrary")),
        name="ssd_scan",
    )(proj3, proj3, proj3, proj3, proj3, hx, hb, hc, h0,
      prm["cwx"], prm["cwb"], prm["cwc"], prm["cbx"], prm["cbb"], prm["cbc"],
      prm["dtb"], prm["alog"], prm["dskip_x"], prm["ssd_gn"],
      jnp.asarray(np.tril(np.ones((Q, Q), np.float32)), BF16), prm["expand"])


def _latent_kernel(ckv_ref, krr_ref, krot_ref, g_ref, cos_ref, sin_ref, c_ref, kr2_ref, kr_ref, *, rope):
    x = ckv_ref[0]
    ms = jnp.mean(x * x, axis=-1, keepdims=True)
    c_ref[0] = x * lax.rsqrt(ms + EPS) * g_ref[...]
    kr2 = krr_ref[0] * cos_ref[...] + krot_ref[0] * sin_ref[...]
    kr2_ref[0] = kr2
    kr_ref[0] = kr2[:, :rope]


def _latent(proj3, offs, prm, cos4, sin4, tm=TM_ROWS):
    b, l, _ = proj3.shape
    kvl, rope = prm["KVL"], prm["ROPE"]
    tm = _pick(l, tm, 8)
    col = lambda width, off: pl.BlockSpec((1, tm, width), lambda i, j, o=off // width: (i, j, o))
    return pl.pallas_call(
        functools.partial(_latent_kernel, rope=rope),
        grid=(b, l // tm),
        in_specs=[col(kvl, offs["ckv"]), col(LANES, offs["krr"]), col(LANES, offs["krot"]),
                  pl.BlockSpec((1, kvl), lambda i, j: (0, 0)),
                  pl.BlockSpec((tm, LANES), lambda i, j: (j, 0)), pl.BlockSpec((tm, LANES), lambda i, j: (j, 0))],
        out_specs=[pl.BlockSpec((1, tm, kvl), lambda i, j: (i, j, 0)),
                   pl.BlockSpec((1, tm, LANES), lambda i, j: (i, j, 0)),
                   pl.BlockSpec((1, tm, rope), lambda i, j: (i, j, 0))],
        out_shape=[jax.ShapeDtypeStruct((b, l, kvl), F32), jax.ShapeDtypeStruct((b, l, LANES), F32),
                   jax.ShapeDtypeStruct((b, l, rope), F32)],
        compiler_params=_params(("parallel", "parallel")),
        name="mla_latent",
    )(proj3, proj3, proj3, prm["kv_a_norm"], cos4, sin4)


def _q_kernel(cq_ref, g_ref, w_ref, cos_ref, sin_ref, gn_ref, gr_ref, q_ref, *, MH, qk_dim):
    x = cq_ref[0]
    ms = jnp.mean(x * x, axis=-1, keepdims=True)
    xn = (x * lax.rsqrt(ms + EPS) * g_ref[...]).astype(BF16)
    qf = jnp.dot(xn, w_ref[...], preferred_element_type=F32)
    cos = cos_ref[...]
    sin = sin_ref[...]
    lane = lax.broadcasted_iota(jnp.int32, (1, LANES), 1)
    half_mask = [(lane < LANES // 2).astype(F32), (lane >= LANES // 2).astype(F32)]
    rope0 = MH * MLA_NOPE
    rot0 = rope0 + MH * (LANES // 2)
    for p in range(MH // 2):
        rp = (qf[:, rope0 + p * LANES: rope0 + (p + 1) * LANES] * cos
              + qf[:, rot0 + p * LANES: rot0 + (p + 1) * LANES] * sin)
        for e in (0, 1):
            h = 2 * p + e
            nope = qf[:, h * MLA_NOPE:(h + 1) * MLA_NOPE]
            rh = rp * half_mask[e]
            ssq = jnp.sum(nope * nope, axis=-1, keepdims=True) + jnp.sum(rh * rh, axis=-1, keepdims=True)
            inv = lax.rsqrt(ssq * (1.0 / qk_dim) + EPS)
            q_ref[0, :, h * 2 * LANES: h * 2 * LANES + LANES] = (nope * inv * gn_ref[...]).astype(q_ref.dtype)
            q_ref[0, :, h * 2 * LANES + LANES:(h + 1) * 2 * LANES] = (rh * inv * gr_ref[e:e + 1, :]).astype(q_ref.dtype)


def _q_proj(proj3, offs, prm, cos4, sin4, tm=TM_ROWS):
    b, l, _ = proj3.shape
    ql, mh = prm["QL"], prm["MH"]
    tm = _pick(l, tm, 16)
    wq = prm["wq"]
    return pl.pallas_call(
        functools.partial(_q_kernel, MH=mh, qk_dim=prm["QK"]),
        grid=(b, l // tm),
        in_specs=[pl.BlockSpec((1, tm, ql), lambda i, j, o=offs["cq"] // ql: (i, j, o)),
                  pl.BlockSpec((1, ql), lambda i, j: (0, 0)),
                  pl.BlockSpec(wq.shape, lambda i, j: (0, 0)),
                  pl.BlockSpec((tm, LANES), lambda i, j: (j, 0)), pl.BlockSpec((tm, LANES), lambda i, j: (j, 0)),
                  pl.BlockSpec((1, LANES), lambda i, j: (0, 0)), pl.BlockSpec((2, LANES), lambda i, j: (0, 0))],
        out_specs=pl.BlockSpec((1, tm, mh * 2 * LANES), lambda i, j: (i, j, 0)),
        out_shape=jax.ShapeDtypeStruct((b, l, mh * 2 * LANES), BF16),
        compiler_params=_params(("parallel", "parallel")),
        name="mla_q",
    )(proj3, prm["q_a_norm"], wq, cos4, sin4, prm["q_gn"], prm["q_gr"])


def _kv_kernel(c_ref, kr2_ref, w_ref, gn_ref, gr_ref, k_ref, v_ref, *, MH, qk_dim):
    c = c_ref[0].astype(BF16)
    kv = jnp.dot(c, w_ref[...], preferred_element_type=F32)
    kr2 = kr2_ref[0]
    lane = lax.broadcasted_iota(jnp.int32, (1, LANES), 1)
    kr_lo = kr2 * (lane < LANES // 2).astype(F32)
    kr_ss = jnp.sum(kr_lo * kr_lo, axis=-1, keepdims=True)
    for h in range(MH):
        nope = kv[:, h * MLA_NOPE:(h + 1) * MLA_NOPE]
        ssq = jnp.sum(nope * nope, axis=-1, keepdims=True) + kr_ss
        inv = lax.rsqrt(ssq * (1.0 / qk_dim) + EPS)
        k_ref[0, :, h * 2 * LANES: h * 2 * LANES + LANES] = (nope * inv * gn_ref[...]).astype(k_ref.dtype)
        k_ref[0, :, h * 2 * LANES + LANES:(h + 1) * 2 * LANES] = (kr2 * inv * gr_ref[h % 2:h % 2 + 1, :]).astype(k_ref.dtype)
    v_ref[0] = kv[:, MH * MLA_NOPE:].astype(v_ref.dtype)


def _kv_proj(c_all, kr2_all, prm, tm=TM_KV):
    b, lk, kvl = c_all.shape
    mh = prm["MH"]
    tm = _pick(lk, tm, 16)
    wkv = prm["wkv"]
    return pl.pallas_call(
        functools.partial(_kv_kernel, MH=mh, qk_dim=prm["QK"]),
        grid=(b, lk // tm),
        in_specs=[pl.BlockSpec((1, tm, kvl), lambda i, j: (i, j, 0)),
                  pl.BlockSpec((1, tm, LANES), lambda i, j: (i, j, 0)),
                  pl.BlockSpec(wkv.shape, lambda i, j: (0, 0)),
                  pl.BlockSpec((1, LANES), lambda i, j: (0, 0)), pl.BlockSpec((2, LANES), lambda i, j: (0, 0))],
        out_specs=[pl.BlockSpec((1, tm, mh * 2 * LANES), lambda i, j: (i, j, 0)),
                   pl.BlockSpec((1, tm, mh * MLA_V_DIM), lambda i, j: (i, j, 0))],
        out_shape=[jax.ShapeDtypeStruct((b, lk, mh * 2 * LANES), BF16),
                   jax.ShapeDtypeStruct((b, lk, mh * MLA_V_DIM), BF16)],
        compiler_params=_params(("parallel", "parallel")),
        name="mla_kv",
    )(c_all, kr2_all, wkv, prm["k_gn"], prm["k_gr"])


def _attn_causal_kernel(q_ref, k_ref, v_ref, o_ref, *, tq, lq):
    for qi in range(lq // tq):
        rows = slice(qi * tq, (qi + 1) * tq)
        q = q_ref[0, rows, :]
        past = qi * tq
        sd = lax.dot_general(q, k_ref[0, past:past + tq, :], NT_DIMS, preferred_element_type=F32)
        ri = lax.broadcasted_iota(jnp.int32, (tq, tq), 0) // CHUNK
        ci = lax.broadcasted_iota(jnp.int32, (tq, tq), 1) // CHUNK
        sd = jnp.where(ri >= ci, sd, -jnp.inf)
        m = jnp.max(sd, axis=-1, keepdims=True)
        if past > 0:
            sp = lax.dot_general(q, k_ref[0, 0:past, :], NT_DIMS, preferred_element_type=F32)
            m = jnp.maximum(m, jnp.max(sp, axis=-1, keepdims=True))
            pp = jnp.exp(sp - m)
            l = jnp.sum(pp, axis=-1, keepdims=True)
            acc = jnp.dot(pp.astype(BF16), v_ref[0, 0:past, :], preferred_element_type=F32)
        pd = jnp.exp(sd - m)
        ld = jnp.sum(pd, axis=-1, keepdims=True)
        accd = jnp.dot(pd.astype(BF16), v_ref[0, past:past + tq, :], preferred_element_type=F32)
        if past > 0:
            l, acc = l + ld, acc + accd
        else:
            l, acc = ld, accd
        o_ref[0, rows, :] = acc / l


def _attn_cached_kernel(q_ref, kc_ref, vc_ref, kn_ref, vn_ref, o_ref):
    q = q_ref[0]
    sc = lax.dot_general(q, kc_ref[0], NT_DIMS, preferred_element_type=F32)
    sn = lax.dot_general(q, kn_ref[0], NT_DIMS, preferred_element_type=F32)
    m = jnp.maximum(jnp.max(sc, axis=-1, keepdims=True), jnp.max(sn, axis=-1, keepdims=True))
    pc = jnp.exp(sc - m)
    pn = jnp.exp(sn - m)
    l = jnp.sum(pc, axis=-1, keepdims=True) + jnp.sum(pn, axis=-1, keepdims=True)
    acc = (jnp.dot(pc.astype(BF16), vc_ref[0], preferred_element_type=F32)
           + jnp.dot(pn.astype(BF16), vn_ref[0], preferred_element_type=F32))
    o_ref[0] = acc / l


def _attention(q, kvs, mh, tq=None):
    b, lq, _ = q.shape
    head = lambda rows, width: pl.BlockSpec((1, rows, width), lambda bi, h: (bi, 0, h))
    in_specs, args = [head(lq, 2 * LANES)], [q]
    for k, v in kvs:
        in_specs += [head(k.shape[1], 2 * LANES), head(v.shape[1], MLA_V_DIM)]
        args += [k, v]
    body = functools.partial(_attn_causal_kernel, tq=tq, lq=lq) if len(kvs) == 1 else _attn_cached_kernel
    return pl.pallas_call(
        body,
        grid=(b, mh),
        in_specs=in_specs,
        out_specs=head(lq, MLA_V_DIM),
        out_shape=jax.ShapeDtypeStruct((b, lq, mh * MLA_V_DIM), F32),
        compiler_params=_params(("parallel", "parallel")),
        name="mla_attention",
    )(*args)


def _oddeven_sort_pairs(lo, n):
    def merge(lo, n, r):
        step = r * 2
        if step < n:
            yield from merge(lo, n, step)
            yield from merge(lo + r, n, step)
            for i in range(lo + r, lo + n - r, step):
                yield (i, i + r)
        else:
            yield (lo, lo + r)
    if n > 1:
        m = n // 2
        yield from _oddeven_sort_pairs(lo, m)
        yield from _oddeven_sort_pairs(lo + m, m)
        yield from merge(lo, n, 1)


def _sort_desc(xs):
    xs = list(xs)
    for i, j in _oddeven_sort_pairs(0, len(xs)):
        xs[i], xs[j] = jnp.maximum(xs[i], xs[j]), jnp.minimum(xs[i], xs[j])
    return xs


def _bitonic_merge_desc(xs):
    xs = list(xs)
    n = len(xs)
    d = n // 2
    while d >= 1:
        for i in range(n):
            if i & d == 0:
                xs[i], xs[i + d] = jnp.maximum(xs[i], xs[i + d]), jnp.minimum(xs[i], xs[i + d])
        d //= 2
    return xs


def _merge_sublanes_top(xs, k):
    shift = SUBLANES // 2
    while shift >= 1:
        other = [pltpu.roll(x, shift, 0) for x in xs]
        if len(xs) < k:
            xs = _bitonic_merge_desc(xs + other[::-1])
        else:
            xs = _bitonic_merge_desc([jnp.maximum(xs[i], other[k - 1 - i]) for i in range(k)])
        shift //= 2
    return xs


def _top_sorted(x, k):
    groups = [x[i * SUBLANES:(i + 1) * SUBLANES, :] for i in range(x.shape[0] // SUBLANES)]
    return _merge_sublanes_top(_sort_desc(groups)[:k], k)


def _prefix_count(rows, test):
    w = jnp.where
    g8 = test(rows[7])
    g4 = test(w(g8, rows[11], rows[3]))
    g2 = test(w(g8, w(g4, rows[13], rows[9]), w(g4, rows[5], rows[1])))
    g1 = test(w(g8, w(g4, w(g2, rows[14], rows[12]), w(g2, rows[10], rows[8])),
                w(g4, w(g2, rows[6], rows[4]), w(g2, rows[2], rows[0]))))
    g16 = test(rows[15])
    return w(g8, 8.0, 0.0) + w(g4, 4.0, 0.0) + w(g2, 2.0, 0.0) + w(g1, 1.0, 0.0) + w(g16, 1.0, 0.0)


def _peer_select_kernel(pq_ref, keys_ref, c1_ref, e1_ref, r2_ref, e2_ref, *, PH, HALF, TOPK):
    assert TOPK == 2 * SUBLANES
    tt = pq_ref.shape[1]
    sub = lax.broadcasted_iota(jnp.int32, (SUBLANES, tt), 0)

    def by_sublane(rows):
        out = rows[-1]
        for s in range(len(rows) - 2, -1, -1):
            out = jnp.where(sub == s, rows[s], out)
        return out

    for h in range(PH):
        q1 = pq_ref[h * 2 * HALF: h * 2 * HALF + HALF, :].astype(BF16)
        q2 = pq_ref[h * 2 * HALF + HALF:(h + 1) * 2 * HALF, :].astype(BF16)
        s1 = jnp.dot(keys_ref[h, 0], q1, preferred_element_type=F32)
        s2 = jnp.dot(keys_ref[h, 1], q2, preferred_element_type=F32)
        v1 = _top_sorted(s1, TOPK)
        v2 = _top_sorted(s2, TOPK)
        v2_lo, v2_hi = by_sublane(v2[:SUBLANES]), by_sublane(v2[SUBLANES:])
        cands = [v1[0] + v2_lo, v1[0] + v2_hi, v1[1] + v2_lo, v1[2] + v2_lo, v1[3] + v2_lo, v1[4] + v2_lo,
                 by_sublane(v1[SUBLANES:]) + v2[0],
                 jnp.where(sub < 6,
                           by_sublane([v1[5], v1[5], v1[6], v1[6], v1[7], v1[7], v1[7], v1[7]])
                           + jnp.where(sub % 2 == 0, v2[0], v2[1]), -jnp.inf)]
        top = _merge_sublanes_top(_sort_desc(cands), TOPK)
        tau = top[TOPK - 1]
        m = top[0]
        zsum = None
        for c in cands:
            z = jnp.where(c >= tau, jnp.exp(c - m), 0.0)
            zsum = z if zsum is None else zsum + z
        zsum = jnp.sum(zsum, axis=0, keepdims=True)
        inv_z = 1.0 / zsum
        for g16 in range(s1.shape[0] // ROWS_BF16):
            ranks, gates = [], []
            for g8 in (2 * g16, 2 * g16 + 1):
                rows = slice(g8 * SUBLANES, (g8 + 1) * SUBLANES)
                a, b2 = s1[rows, :], s2[rows, :]
                count1 = _prefix_count(v2, lambda probe: a + probe >= tau)
                rank2 = _prefix_count(v2, lambda probe: probe > b2)
                c1_ref[h, rows, :] = count1
                e1_ref[h, rows, :] = jnp.exp(a - v1[0])
                ranks.append(rank2)
                gates.append(jnp.exp(b2 - v2[0]) * inv_z)
            rows16 = slice(g16 * ROWS_BF16, (g16 + 1) * ROWS_BF16)
            r2_ref[h, rows16, :] = jnp.concatenate(ranks, axis=0).astype(BF16)
            e2_ref[h, rows16, :] = jnp.concatenate(gates, axis=0).astype(BF16)


def _peer_select(pq_t, keys_b, tt=TT_SELECT):
    t = pq_t.shape[1]
    ph, _, nk, half = keys_b.shape
    tt = _pick(t, tt, LANES)
    big = lambda: pl.BlockSpec((ph, nk, tt), lambda i: (0, 0, i))
    shp = lambda dt: jax.ShapeDtypeStruct((ph, nk, t), dt)
    return pl.pallas_call(
        functools.partial(_peer_select_kernel, PH=ph, HALF=half, TOPK=PEER_TOPK),
        grid=(t // tt,),
        in_specs=[pl.BlockSpec((ph * 2 * half, tt), lambda i: (0, i)),
                  pl.BlockSpec(keys_b.shape, lambda i: (0, 0, 0, 0))],
        out_specs=[big(), big(), big(), big()],
        out_shape=[shp(F32), shp(F32), shp(BF16), shp(BF16)],
        compiler_params=_params(("parallel",)),
        name="peer_select",
    )(pq_t, keys_b)


def _gelu(x):
    return 0.5 * x * (1.0 + lax.erf(x * (1.0 / math.sqrt(2.0))))


ROWS_BF16 = 16
GATE_ROWS = 128


def _peer_main_kernel(hn_ref, u_ref, vt_ref, c1_ref, e1_ref, r2_ref, e2_ref, out_ref,
                      ata_ref, atb_ref, cta_ref, ctb_ref, *, PH, NK, NI, NJ):
    s = pl.program_id(0)
    tt = ata_ref.shape[1]

    @pl.when(s == 0)
    def _init():
        for ref in (ata_ref, atb_ref, cta_ref, ctb_ref):
            ref[...] = jnp.zeros(ref.shape, ref.dtype)

    @pl.when(lax.rem(jnp.maximum(s - 2, 0), NJ) == 0)
    def _zero_out():
        out_ref[...] = jnp.zeros(out_ref.shape, F32)

    def step(at_new, at_prev, ct_new, ct_prev):
        d = out_ref.shape[0]
        blk = lax.rem(jnp.maximum(s - 1, 0), NJ)

        def mix_rows(m0, m1):
            out_ref[m0:m1, :] += jnp.dot(vt_ref[m0:m1, :], ct_prev[...], preferred_element_type=F32)

        def score_block(m0, m1, n0, n1):
            at_new[m0:m1, n0:n1] = jnp.dot(u_ref[m0:m1, :], hn_ref[:, n0:n1], preferred_element_type=F32)

        def gate_tile(ii, tc, r0):
            row = blk * NI + ii
            cols = slice(tc * LANES, (tc + 1) * LANES)
            groups = [slice(r0 + k * ROWS_BF16, r0 + (k + 1) * ROWS_BF16) for k in range(GATE_ROWS // ROWS_BF16)]
            g = [None] * len(groups)
            zero = jnp.zeros((ROWS_BF16, LANES), BF16)
            for h in range(PH):
                bcast = lambda r: jnp.broadcast_to(r[:, cols], (ROWS_BF16, LANES)).astype(BF16)
                c1b = bcast(c1_ref[h, pl.ds(row, 1), :])
                e1b = bcast(e1_ref[h, pl.ds(row, 1), :])
                for k, rws in enumerate(groups):
                    term = e1b * jnp.where(r2_ref[h, rws, cols] < c1b, e2_ref[h, rws, cols], zero)
                    g[k] = term if g[k] is None else g[k] + term
            for k, rws in enumerate(groups):
                orow = slice(ii * NK + rws.start, ii * NK + rws.stop)
                ct_new[orow, cols] = _gelu(at_prev[orow, cols]).astype(BF16) * g[k]

        tiles = [(ii, tc, r0) for ii in range(NI) for tc in range(tt // LANES) for r0 in range(0, NK, GATE_ROWS)]
        n_mix = 8
        n_sn = 1
        n_sm = 1
        ne = at_new.shape[0]
        mxu_items = [functools.partial(mix_rows, k * d // n_mix, (k + 1) * d // n_mix) for k in range(n_mix)]
        mxu_items += [functools.partial(score_block, km * ne // n_sm, (km + 1) * ne // n_sm,
                                        kn * tt // n_sn, (kn + 1) * tt // n_sn)
                      for kn in range(n_sn) for km in range(n_sm)]
        weights = [n_sn * n_sm] * n_mix + [n_mix] * (n_sn * n_sm)
        done, total = 0, sum(weights)
        for item, w in zip(mxu_items, weights):
            upto = len(tiles) * (done + w) // total
            for tile_args in tiles[len(tiles) * done // total: upto]:
                gate_tile(*tile_args)
            item()
            done += w

    @pl.when(s % 2 == 0)
    def _even():
        step(ata_ref, atb_ref, ctb_ref, cta_ref)

    @pl.when(s % 2 == 1)
    def _odd():
        step(atb_ref, ata_ref, cta_ref, ctb_ref)


def _peer_main(hn_t, u_b, vt_b, sel, tt=TT_PEER, ne=NE_PEER):
    d, t = hn_t.shape
    c1, e1, r2, e2 = sel
    ph, nk, _ = c1.shape
    nexp = u_b.shape[0]
    tt = _pick(t, tt, LANES)
    ne = _pick(nexp, ne, nk)
    ni = ne // nk
    nj = nexp // ne
    n_tiles = t // tt
    n_blocks = n_tiles * nj
    assert nj % 2 == 0
    once = dict(pipeline_mode=pl.Buffered(1))
    tile = lambda s, lag: jnp.minimum(jnp.maximum(s - lag, 0) // nj, n_tiles - 1)
    big = lambda: pl.BlockSpec((ph, nk, tt), lambda s: (0, 0, tile(s, 1)), **once)
    return pl.pallas_call(
        functools.partial(_peer_main_kernel, PH=ph, NK=nk, NI=ni, NJ=nj),
        grid=(n_blocks + 2,),
        in_specs=[pl.BlockSpec((d, tt), lambda s: (0, tile(s, 0)), **once),
                  pl.BlockSpec((ne, d), lambda s: (lax.rem(s, nj), 0)),
                  pl.BlockSpec((d, ne), lambda s: (0, lax.rem(jnp.maximum(s - 2, 0), nj))),
                  big(), big(), big(), big()],
        out_specs=pl.BlockSpec((d, tt), lambda s: (0, tile(s, 2))),
        out_shape=jax.ShapeDtypeStruct((d, t), F32),
        scratch_shapes=[pltpu.VMEM((ne, tt), F32), pltpu.VMEM((ne, tt), F32),
                        pltpu.VMEM((ne, tt), BF16), pltpu.VMEM((ne, tt), BF16)],
        compiler_params=_params(("arbitrary",)),
        name="peer_main",
    )(hn_t, u_b, vt_b, c1, e1, r2, e2)


def _tadd_kernel(x_ref, yt_ref, o_ref):
    o_ref[...] = x_ref[...] + yt_ref[...].T


def _transpose_add(x2d, yt, tm=TM_ROWS):
    t, d = x2d.shape
    tm = _pick(t, tm, LANES)
    return pl.pallas_call(
        _tadd_kernel,
        grid=(t // tm,),
        in_specs=[pl.BlockSpec((tm, d), lambda i: (i, 0)), pl.BlockSpec((d, tm), lambda i: (0, i))],
        out_specs=pl.BlockSpec((tm, d), lambda i: (i, 0)),
        out_shape=jax.ShapeDtypeStruct((t, d), F32),
        compiler_params=_params(("parallel",)),
        name="peer_residual",
    )(x2d, yt)


def _prepare(norm_mix, w_in, conv_w, conv_b, dt_bias, a_log, d_skip, ssd_norm,
             q_a_norm, w_q_up, kv_a_norm, w_kv_up, q_norm, k_norm, attn_out_norm,
             w_out, norm_ffn, peer_w_q, peer_keys, peer_u, peer_v, rope, N):
    d = w_in.shape[0]
    W = ssd_norm.shape[0]
    H = dt_bias.shape[0]
    P = W // H
    xbc = conv_w.shape[1]
    GN = (xbc - W) // 2
    G = GN // N
    QL = q_a_norm.shape[0]
    KVL = kv_a_norm.shape[0]
    MW = attn_out_norm.shape[0]
    MH = MW // MLA_V_DIM
    QK = MLA_NOPE + rope
    HL = _round_up(H, LANES)
    half = rope // 2
    assert 2 * rope == LANES and MH % 2 == 0 and P == SSD_HEAD_DIM

    s = np.cumsum([0, W, xbc, H, QL, KVL, rope])
    wz, wxbc, wdt, wcq, wckv, wkr = (w_in[:, s[i]:s[i + 1]] for i in range(6))
    wrot = jnp.concatenate([-wkr[:, half:], wkr[:, :half]], axis=1)
    pieces = [wz, wxbc, wcq, wckv, wkr, wkr, wrot, wrot, wdt]
    offs, o = {}, 0
    for name, width in (("z", W), ("xs", W), ("bm", GN), ("cm", GN), ("cq", QL), ("ckv", KVL),
                        ("krr", LANES), ("krot", LANES), ("dt", HL)):
        assert o % width == 0, (name, o, width)
        offs[name] = o
        o += width
    NP = _round_up(o, 1024)
    w_in_p = jnp.concatenate(pieces + [jnp.zeros((d, NP - (o - HL + H)), w_in.dtype)], axis=1).astype(BF16)

    wq3 = w_q_up.reshape(QL, MH, QK)
    wq_nope = wq3[:, :, :MLA_NOPE].reshape(QL, MH * MLA_NOPE)
    wq_rope = wq3[:, :, MLA_NOPE:]
    wq_rot = jnp.concatenate([-wq_rope[:, :, half:], wq_rope[:, :, :half]], axis=2)
    wq = jnp.concatenate([wq_nope, wq_rope.reshape(QL, MH * rope), wq_rot.reshape(QL, MH * rope)], axis=1).astype(BF16)

    wkv3 = w_kv_up.reshape(KVL, MH, MLA_NOPE + MLA_V_DIM)
    wkv = jnp.concatenate([wkv3[:, :, :MLA_NOPE].reshape(KVL, MH * MLA_NOPE),
                           wkv3[:, :, MLA_NOPE:].reshape(KVL, MH * MLA_V_DIM)], axis=1).astype(BF16)

    scale = QK ** -0.5
    zeros_h = jnp.zeros((rope,), F32)

    def gains(g, sc):
        gn = (g[:MLA_NOPE] * sc).reshape(1, LANES)
        gr = jnp.stack([jnp.concatenate([g[MLA_NOPE:] * sc, zeros_h]), jnp.concatenate([zeros_h, g[MLA_NOPE:] * sc])])
        return gn.astype(F32), gr.astype(F32)

    q_gn, q_gr = gains(q_norm, scale)
    k_gn, k_gr = gains(k_norm, 1.0)

    pad_h = lambda v: jnp.concatenate([v.astype(F32), jnp.zeros((HL - H,), F32)]).reshape(1, HL)
    head_of_col = np.arange(W) // P
    expand = (np.arange(HL)[:, None] == head_of_col[None, :]).astype(np.float32)

    return dict(
        H=H, P=P, G=G, N=N, HL=HL, QL=QL, KVL=KVL, MH=MH, QK=QK, ROPE=rope, NP=NP, offs=offs,
        norm_mix=norm_mix, w_in=w_in_p,
        cwx=conv_w[:, :W], cwb=conv_w[:, W:W + GN], cwc=conv_w[:, W + GN:],
        cbx=conv_b[:W].reshape(1, W), cbb=conv_b[W:W + GN].reshape(1, GN), cbc=conv_b[W + GN:].reshape(1, GN),
        dtb=pad_h(dt_bias), alog=pad_h(a_log),
        dskip_x=jnp.repeat(d_skip.astype(F32), P).reshape(1, W), ssd_gn=ssd_norm.reshape(1, W).astype(F32),
        expand=jnp.asarray(expand, BF16),
        q_a_norm=q_a_norm.reshape(1, QL).astype(F32), kv_a_norm=kv_a_norm.reshape(1, KVL).astype(F32),
        wq=wq, wkv=wkv, q_gn=q_gn, q_gr=q_gr, k_gn=k_gn, k_gr=k_gr,
        attn_out_norm=attn_out_norm, w_out_a=w_out[:W].astype(BF16), w_out_b=w_out[W:].astype(BF16),
        norm_ffn=norm_ffn, peer_w_q_t=_transpose_cast(peer_w_q), peer_keys=peer_keys.astype(BF16),
        peer_u=peer_u.astype(BF16), peer_vt=_transpose_cast(peer_v),
    )


def _rope_tables(pos, rope):
    inv = 1.0 / (ROPE_BASE ** (jnp.arange(0, rope, 2, dtype=F32) / rope))
    ang = pos.astype(F32)[:, None] * inv[None, :]
    reps = LANES // (rope // 2)
    return jnp.tile(jnp.cos(ang), (1, reps)), jnp.tile(jnp.sin(ang), (1, reps))


def _layer(x, pos, past_ckv, past_krope, conv_hist, ssm_h0, prm):
    b, l, d = x.shape
    t = b * l
    H, P, G, N = prm["H"], prm["P"], prm["G"], prm["N"]
    W, GN = H * P, G * N
    offs = prm["offs"]
    x2d = x.reshape(t, d)

    h = _rms_cast(x2d, prm["norm_mix"])
    proj = _matmul([(h, prm["w_in"])], name="in_proj")
    proj3 = proj.reshape(b, l, prm["NP"])

    hist = (conv_hist[:, :, :W], conv_hist[:, :, W:W + GN], conv_hist[:, :, W + GN:])
    y_ssd, h_last, nx, nb, ncm = _ssd(proj3, offs, hist, ssm_h0.reshape(b, W, N), prm, _pick(l, SSD_CHUNK, 8))
    conv_new = jnp.concatenate([nx, nb, ncm], axis=-1)
    h_last = h_last.reshape(b, H, P, N)

    cos4, sin4 = _rope_tables(pos, prm["ROPE"])
    c_new, kr2_new, kr_new = _latent(proj3, offs, prm, cos4, sin4)
    q = _q_proj(proj3, offs, prm, cos4, sin4)
    kv_new = _kv_proj(c_new, kr2_new, prm)
    if past_ckv is None:
        o = _attention(q, [kv_new], prm["MH"], tq=_pick(l, TQ_ATTN, CHUNK))
    else:
        lp = past_ckv.shape[1]
        assert l <= CHUNK and lp % CHUNK == 0
        kv_past = _kv_proj(past_ckv, jnp.concatenate([past_krope, past_krope], axis=-1), prm)
        o = _attention(q, [kv_past, kv_new], prm["MH"])
    o_n = _rms_cast(o.reshape(t, -1), prm["attn_out_norm"])

    x_mid = _matmul([(y_ssd.reshape(t, W), prm["w_out_a"]), (o_n, prm["w_out_b"])], res=x2d,
                    name="out_proj")

    hn_t = _rms_cast(x_mid, prm["norm_ffn"], transpose=True)
    pq_t = _matmul([(prm["peer_w_q_t"], hn_t)], tn=TN_PEER_QUERY, name="peer_query")
    sel = _peer_select(pq_t, prm["peer_keys"])
    yt = _peer_main(hn_t, prm["peer_u"], prm["peer_vt"], sel)
    out = _transpose_add(x_mid, yt)
    return out.reshape(b, l, d), c_new, kr_new, h_last.astype(ssm_h0.dtype), conv_new


def kernel(x_prompt, x_sample, cache_mla_ckv, cache_mla_krope, state_ssm, state_conv, norm_mix, w_in, conv_w, conv_b, dt_bias, a_log, d_skip, ssd_norm, q_a_norm, w_q_up, kv_a_norm, w_kv_up, q_norm, k_norm, attn_out_norm, w_out, norm_ffn, peer_w_q, peer_keys, peer_u, peer_v):
    bp, lp, _ = x_prompt.shape
    ls = x_sample.shape[1]
    depth = norm_mix.shape[0]
    past = cache_mla_ckv.shape[2]
    rope = cache_mla_krope.shape[-1]
    nstate = state_ssm.shape[-1]
    xbc = state_conv.shape[-1]
    pos_p = jnp.arange(lp, dtype=jnp.int32)
    pos_s = past + jnp.arange(ls, dtype=jnp.int32)
    yp, ys = x_prompt, x_sample
    outs_p, outs_s = [], []
    for layer in range(depth):
        prm = _prepare(norm_mix[layer], w_in[layer], conv_w[layer], conv_b[layer], dt_bias[layer], a_log[layer],
                       d_skip[layer], ssd_norm[layer], q_a_norm[layer], w_q_up[layer], kv_a_norm[layer],
                       w_kv_up[layer], q_norm[layer], k_norm[layer], attn_out_norm[layer], w_out[layer],
                       norm_ffn[layer], peer_w_q[layer], peer_keys[layer], peer_u[layer], peer_v[layer],
                       rope, nstate)
        zero_conv = jnp.zeros((bp, state_conv.shape[2], xbc), x_prompt.dtype)
        zero_ssm = jnp.zeros((bp,) + state_ssm.shape[2:], state_ssm.dtype)
        yp, *rest_p = _layer(yp, pos_p, None, None, zero_conv, zero_ssm, prm)
        outs_p.append(rest_p)
        ys, *rest_s = _layer(ys, pos_s, cache_mla_ckv[layer], cache_mla_krope[layer],
                             state_conv[layer], state_ssm[layer], prm)
        outs_s.append(rest_s)
    stack = lambda outs, i: jnp.stack([o[i] for o in outs])
    return (yp, ys,
            stack(outs_p, 0), stack(outs_p, 1), stack(outs_p, 2), stack(outs_p, 3),
            stack(outs_s, 0), stack(outs_s, 1), stack(outs_s, 2), stack(outs_s, 3))
```
